```python
import jax, jax.numpy as jnp
from jax import lax
import numpy as np

D_MODEL = 2048
BATCH = 4
SEQ = 4096
DEPTH = 4

CHUNK = 64
N_A_LAYERS = DEPTH // 2
N_B_LAYERS = DEPTH - N_A_LAYERS
HG_KDIM = 128
HG_HEADS = D_MODEL // HG_KDIM
HG_VDIM = D_MODEL // HG_HEADS
HG_FDIM = HG_HEADS * HG_KDIM
HG_BLOCK = CHUNK // 4
FOX_HDIM = 128
FOX_HEADS = D_MODEL // FOX_HDIM
Q_BLOCK = 128
FORGET_BIAS_INIT = 2.0
EPS = 1e-6

kernel_name = "hgrn2_fox_yoco_streaming_trunk"


def rmsnorm(x, w):
    xf = x.astype(jnp.float32)
    y = xf * lax.rsqrt(jnp.mean(xf * xf, axis=-1, keepdims=True) + EPS)
    return (y * w.astype(jnp.float32)).astype(x.dtype)


def head_rmsnorm(x, w):
    xf = x.astype(jnp.float32)
    return xf * lax.rsqrt(jnp.mean(xf * xf, axis=-1, keepdims=True) + EPS) * w.astype(jnp.float32)


def hgrn2_chunkwise(q, k, v, g):
    B, H, T, dk = q.shape
    dv = v.shape[-1]
    n = T // HG_BLOCK

    def to_blocks(a):
        return jnp.moveaxis(a.reshape(B, H, n, HG_BLOCK, a.shape[-1]), 2, 0)

    mask = jnp.tril(jnp.ones((HG_BLOCK, HG_BLOCK), dtype=bool))[:, :, None]

    def step(S, blk):
        qc, kc, vc, gc = blk
        b = jnp.cumsum(gc, axis=-2)
        diff = b[..., :, None, :] - b[..., None, :, :]
        decay = jnp.exp(jnp.where(mask, diff, -jnp.inf))
        A = jnp.einsum('bhtd,bhsd,bhtsd->bhts', qc, kc, decay)
        o = (jnp.einsum('bhts,bhse->bhte', A, vc)
             + jnp.einsum('bhtd,bhde->bhte', qc * jnp.exp(b), S))
        b_last = b[..., -1:, :]
        S = (jnp.exp(b_last[..., 0, :])[..., None] * S
             + jnp.einsum('bhsd,bhse->bhde', kc * jnp.exp(b_last - b), vc))
        return S, o

    S0 = jnp.zeros((B, H, dk, dv), jnp.float32)
    _, o = lax.scan(step, S0, (to_blocks(q), to_blocks(k), to_blocks(v), to_blocks(g)))
    return jnp.moveaxis(o, 0, 2).reshape(B, H, T, dv)


def hgrn2_layer(x, norm_w, w_in, lb, out_norm_w, w_out):
    B, T, _ = x.shape
    h = rmsnorm(x, norm_w)
    proj = h @ w_in
    q = proj[..., :HG_FDIM]
    fz = proj[..., HG_FDIM:2 * HG_FDIM]
    i = proj[..., 2 * HG_FDIM:2 * HG_FDIM + D_MODEL]
    gate = proj[..., 2 * HG_FDIM + D_MODEL:]
    heads_k = lambda a: a.reshape(B, T, HG_HEADS, HG_KDIM).astype(jnp.float32).transpose(0, 2, 1, 3)
    q, fz = heads_k(q), heads_k(fz)
    i = i.reshape(B, T, HG_HEADS, HG_VDIM).astype(jnp.float32).transpose(0, 2, 1, 3)
    lb = lb.reshape(HG_HEADS, 1, HG_KDIM)
    log_f = jnp.logaddexp(jnp.log(lb), jnp.log1p(-lb) + jax.nn.log_sigmoid(fz))
    k = (1.0 - lb) * jax.nn.sigmoid(-fz)
    o = hgrn2_chunkwise(q, k, i, log_f)
    o = head_rmsnorm(o.transpose(0, 2, 1, 3), out_norm_w.reshape(HG_HEADS, HG_VDIM))
    o = o.reshape(B, T, D_MODEL) * jax.nn.silu(gate.astype(jnp.float32))
    return x + (o.astype(x.dtype) @ w_out)


def shared_kv(x, norm_w, w_kvf, f_bias, k_norm_w):
    B, T, _ = x.shape
    h = rmsnorm(x, norm_w)
    kvf = h @ w_kvf
    k = kvf[..., :D_MODEL].reshape(B, T, FOX_HEADS, FOX_HDIM)
    v = kvf[..., D_MODEL:2 * D_MODEL].reshape(B, T, FOX_HEADS, FOX_HDIM)
    fz = kvf[..., 2 * D_MODEL:].astype(jnp.float32) + f_bias.astype(jnp.float32)
    k = head_rmsnorm(k, k_norm_w).transpose(0, 2, 1, 3)
    v = v.astype(jnp.float32).transpose(0, 2, 1, 3)
    F = jnp.cumsum(jax.nn.log_sigmoid(fz), axis=1).transpose(0, 2, 1)
    return k, v, F


def forgetting_attention(q, k, v, F):
    T = q.shape[2]
    scale = FOX_HDIM ** -0.5
    local = jnp.arange(Q_BLOCK)
    outs = []
    for blk in range(T // Q_BLOCK):
        q0 = blk * Q_BLOCK
        kend = q0 + Q_BLOCK
        s = (jnp.einsum('bhqd,bhkd->bhqk', q[:, :, q0:kend], k[:, :, :kend]) * scale
             + F[:, :, q0:kend, None] - F[:, :, None, :kend])
        causal = (q0 + local)[:, None] >= jnp.arange(kend)[None, :]
        p = jax.nn.softmax(jnp.where(causal, s, -jnp.inf), axis=-1)
        outs.append(jnp.einsum('bhqk,bhkd->bhqd', p, v[:, :, :kend]))
    return jnp.concatenate(outs, axis=2)


def fox_layer(x, norm_w, w_in, q_norm_w, out_norm_w, w_out, k, v, F):
    B, T, _ = x.shape
    h = rmsnorm(x, norm_w)
    proj = h @ w_in
    q = proj[..., :D_MODEL].reshape(B, T, FOX_HEADS, FOX_HDIM)
    gate = proj[..., D_MODEL:]
    q = head_rmsnorm(q, q_norm_w).transpose(0, 2, 1, 3)
    o = forgetting_attention(q, k, v, F)
    o = head_rmsnorm(o.transpose(0, 2, 1, 3), out_norm_w.reshape(FOX_HEADS, FOX_HDIM))
    o = o.reshape(B, T, D_MODEL) * jax.nn.silu(gate.astype(jnp.float32))
    return x + (o.astype(x.dtype) @ w_out)


def setup_inputs(seed: int = 0) -> dict:
    key = jax.random.key(seed)
    ks = jax.random.split(key, 16)
    D = D_MODEL
    s = D ** -0.5
    nrm = jax.random.normal
    return {
        "x": nrm(ks[0], (BATCH, SEQ, D), jnp.float32),
        "a_norm_w": 1.0 + 0.02 * nrm(ks[1], (N_A_LAYERS, D), jnp.float32),
        "a_w_in": s * nrm(ks[2], (N_A_LAYERS, D, 2 * HG_FDIM + 2 * D), jnp.float32),
        "a_lb_logits": 0.1 * nrm(ks[3], (N_A_LAYERS, HG_FDIM), jnp.float32),
        "a_out_norm_w": 1.0 + 0.02 * nrm(ks[4], (N_A_LAYERS, D), jnp.float32),
        "a_w_out": s * nrm(ks[5], (N_A_LAYERS, D, D), jnp.float32),
        "kv_norm_w": 1.0 + 0.02 * nrm(ks[6], (D,), jnp.float32),
        "kv_w": s * nrm(ks[7], (D, 2 * D + FOX_HEADS), jnp.float32),
        "kv_f_bias": FORGET_BIAS_INIT + 0.1 * nrm(ks[8], (FOX_HEADS,), jnp.float32),
        "k_norm_w": 1.0 + 0.02 * nrm(ks[9], (FOX_HDIM,), jnp.float32),
        "b_norm_w": 1.0 + 0.02 * nrm(ks[10], (N_B_LAYERS, D), jnp.float32),
        "b_w_in": s * nrm(ks[11], (N_B_LAYERS, D, 2 * D), jnp.float32),
        "b_q_norm_w": 1.0 + 0.02 * nrm(ks[12], (N_B_LAYERS, FOX_HDIM), jnp.float32),
        "b_out_norm_w": 1.0 + 0.02 * nrm(ks[13], (N_B_LAYERS, D), jnp.float32),
        "b_w_out": s * nrm(ks[14], (N_B_LAYERS, D, D), jnp.float32),
    }


def reference(x, a_norm_w, a_w_in, a_lb_logits, a_out_norm_w, a_w_out,
              kv_norm_w, kv_w, kv_f_bias, k_norm_w,
              b_norm_w, b_w_in, b_q_norm_w, b_out_norm_w, b_w_out):
    lb_all = jnp.cumsum(jax.nn.softmax(a_lb_logits.astype(jnp.float32), axis=0), axis=0)
    lb_all = lb_all - lb_all[0:1]
    k = v = F = None
    for layer in range(DEPTH):
        if layer < N_A_LAYERS:
            x = hgrn2_layer(x, a_norm_w[layer], a_w_in[layer], lb_all[layer],
                            a_out_norm_w[layer], a_w_out[layer])
        else:
            if layer == N_A_LAYERS:
                k, v, F = shared_kv(x, kv_norm_w, kv_w, kv_f_bias, k_norm_w)
            j = layer - N_A_LAYERS
            x = fox_layer(x, b_norm_w[j], b_w_in[j], b_q_norm_w[j], b_out_norm_w[j],
                          b_w_out[j], k, v, F)
    return x
```

```python
import functools

import jax
import jax.numpy as jnp
from jax import lax
from jax.experimental import pallas as pl
from jax.experimental.pallas import tpu as pltpu

HEAD_DIM = 128
SUB_BLOCK = 16
EPS = 1e-6
VMEM_LIMIT_BYTES = 56 * 1024 * 1024
NEG_BIG = -1e30

F32 = jnp.float32
BF16 = jnp.bfloat16


def _dot(a, b):
    return jnp.dot(a, b, preferred_element_type=F32)


def _dot_nt(a, b):
    return lax.dot_general(a, b, (((1,), (1,)), ((), ())), preferred_element_type=F32)


def _dot_tn(a, b):
    return lax.dot_general(a, b, (((0,), (0,)), ((), ())), preferred_element_type=F32)


def _split3(x):
    hi = x.astype(BF16)
    r1 = x - hi.astype(F32)
    mid = r1.astype(BF16)
    lo = (r1 - mid.astype(F32)).astype(BF16)
    return hi, mid, lo


def _rms_rows(x, w):
    ms = jnp.mean(x * x, axis=-1, keepdims=True)
    return x * lax.rsqrt(ms + EPS) * w


def _proj_kernel(*refs, kinds, scale):
    n = len(kinds)
    x_ref, nw_ref = refs[0], refs[1]
    w_refs = refs[2:2 + n]
    aux_refs = refs[2 + n:2 + 2 * n]
    n_out = sum(2 if kd == "hgate" else 1 for kd in kinds)
    out_refs = refs[2 + 2 * n:2 + 2 * n + n_out]
    h_ref = refs[2 + 2 * n + n_out]

    @pl.when(pl.program_id(1) == 0)
    def _():
        h_ref[...] = _rms_rows(x_ref[...], nw_ref[...]).astype(BF16)

    h = h_ref[...]
    oi = 0
    for s, kind in enumerate(kinds):
        acc = _dot(h, w_refs[s][...])
        aux = aux_refs[s][...]
        if kind == "cast":
            out_refs[oi][...] = acc.astype(out_refs[oi].dtype)
            oi += 1
        elif kind == "silu":
            out_refs[oi][...] = (acc / (1.0 + jnp.exp(-acc))).astype(out_refs[oi].dtype)
            oi += 1
        elif kind == "headnorm":
            tn = acc.shape[1]
            for c in range(tn // HEAD_DIM):
                sl = slice(c * HEAD_DIM, (c + 1) * HEAD_DIM)
                a = acc[:, sl]
                ms = jnp.mean(a * a, axis=-1, keepdims=True)
                y = a * lax.rsqrt(ms + EPS) * aux[:, sl]
                if scale != 1.0:
                    y = y * scale
                out_refs[oi][:, sl] = y.astype(out_refs[oi].dtype)
            oi += 1
        elif kind == "hgate":
            lb = aux
            e = jnp.exp(-jnp.abs(acc))
            log_sig = jnp.minimum(acc, 0.0) - jnp.log1p(e)
            a = jnp.log(lb)
            c = jnp.log1p(-lb) + log_sig
            g = jnp.maximum(a, c) + jnp.log1p(jnp.exp(-jnp.abs(a - c)))
            r = 1.0 / (1.0 + e)
            sig_neg = jnp.where(acc >= 0.0, e * r, r)
            out_refs[oi][...] = g.astype(out_refs[oi].dtype)
            out_refs[oi + 1][...] = ((1.0 - lb) * sig_neg).astype(out_refs[oi + 1].dtype)
            oi += 2
        else:
            raise ValueError(kind)


def _rms_proj(x, norm_w, streams, *, scale=1.0, tm=1024, tn=256):
    n_rows, d = x.shape
    n_cols = streams[0][3].shape[1]
    tm = min(tm, n_rows)
    tn = min(tn, n_cols)
    assert n_rows % tm == 0 and n_cols % tn == 0
    kinds = tuple(s[4] for s in streams)

    in_specs = [pl.BlockSpec((tm, d), lambda i, j: (i, 0)),
                pl.BlockSpec((1, d), lambda i, j: (0, 0))]
    args = [x, norm_w.reshape(1, d)]
    for (w, layer, off, aux, kind, _) in streams:
        assert off % tn == 0
        in_specs.append(pl.BlockSpec((None, d, tn),
                                     functools.partial(lambda i, j, l, o: (l, 0, j + o), l=layer, o=off // tn)))
        args.append(w)
    for (w, layer, off, aux, kind, _) in streams:
        in_specs.append(pl.BlockSpec((1, tn), lambda i, j: (0, j)))
        args.append(aux)
    out_shapes, out_specs = [], []
    for (w, layer, off, aux, kind, dts) in streams:
        for dt in dts:
            out_shapes.append(jax.ShapeDtypeStruct((n_rows, n_cols), dt))
            out_specs.append(pl.BlockSpec((tm, tn), lambda i, j: (i, j)))

    return pl.pallas_call(
        functools.partial(_proj_kernel, kinds=kinds, scale=scale),
        grid=(n_rows // tm, n_cols // tn),
        in_specs=in_specs,
        out_specs=out_specs,
        out_shape=out_shapes,
        scratch_shapes=[pltpu.VMEM((tm, d), BF16)],
        compiler_params=pltpu.CompilerParams(
            dimension_semantics=("parallel", "arbitrary"),
            vmem_limit_bytes=VMEM_LIMIT_BYTES),
    )(*args)


def _out_proj_kernel(o_ref, w_ref, x_ref, y_ref):
    y_ref[...] = x_ref[...] + _dot(o_ref[...], w_ref[...])


def _out_proj(o, w3d, layer, x, *, tm=1024, tn=512):
    n_rows, d = x.shape
    tm = min(tm, n_rows)
    tn = min(tn, d)
    return pl.pallas_call(
        _out_proj_kernel,
        grid=(n_rows // tm, d // tn),
        in_specs=[pl.BlockSpec((tm, d), lambda i, j: (i, 0)),
                  pl.BlockSpec((None, d, tn), functools.partial(lambda i, j, l: (l, 0, j), l=layer)),
                  pl.BlockSpec((tm, tn), lambda i, j: (i, j))],
        out_specs=pl.BlockSpec((tm, tn), lambda i, j: (i, j)),
        out_shape=jax.ShapeDtypeStruct((n_rows, d), F32),
        compiler_params=pltpu.CompilerParams(
            dimension_semantics=("parallel", "arbitrary"),
            vmem_limit_bytes=VMEM_LIMIT_BYTES),
    )(o, w3d, x)


def _hgrn_chunk(q, g, k, v, st, masks):
    l_stack, row_blk, lane_blk, lane_idx, t_in, n_sub = masks
    c = q.shape[0]
    ghi, gmid, glo = _split3(g)
    cums = _dot(l_stack, ghi) + _dot(l_stack, gmid) + _dot(l_stack, glo)
    w, suf, bsf = cums[:c], cums[c:2 * c], cums[2 * c:]
    tot = bsf[c - 1:c] + w[c - 1:c]

    qt = q * jnp.exp(w)
    kh = k * jnp.exp(suf)
    kh_b = kh.astype(BF16)

    o = _dot_nt((qt * jnp.exp(bsf)).astype(BF16), st.astype(BF16))

    w3 = w.reshape(n_sub, SUB_BLOCK, HEAD_DIM)
    q3 = q.reshape(n_sub, SUB_BLOCK, HEAD_DIM)
    k3 = k.reshape(n_sub, SUB_BLOCK, HEAD_DIM)
    a = jnp.zeros((c, c), F32)
    for s in range(SUB_BLOCK):
        ws = w3[:, s:s + 1, :]
        ks = k3[:, s:s + 1, :]
        e = jnp.exp(jnp.where(t_in >= s, w3 - ws, -jnp.inf))
        col = jnp.sum(q3 * ks * e, axis=-1, keepdims=True).reshape(c, 1)
        a = jnp.where(lane_idx == row_blk * SUB_BLOCK + s, col, a)

    for j in range(n_sub - 1):
        ref_row = bsf[(j + 1) * SUB_BLOCK:(j + 1) * SUB_BLOCK + 1]
        dj = jnp.exp(jnp.where(row_blk[:, :1] > j, bsf - ref_row, -jnp.inf))
        aj = _dot_nt((qt * dj).astype(BF16), kh_b)
        a = jnp.where((lane_blk == j) & (row_blk > j), aj, a)

    o = o + _dot(a.astype(BF16), v)

    kbar = kh * jnp.exp(tot - bsf - w - suf)
    st_new = st * jnp.exp(tot) + _dot_tn(v, kbar.astype(BF16))
    return o, st_new


def _hgrn_kernel(q_ref, g_ref, k_ref, v_ref, sg_ref, onw_ref, o_ref, st_ref, *, chunk, heads):
    tb = q_ref.shape[0]
    n_chunks = tb // chunk
    n_sub = chunk // SUB_BLOCK

    @pl.when(pl.program_id(2) == 0)
    def _():
        st_ref[...] = jnp.zeros_like(st_ref)

    r = lax.broadcasted_iota(jnp.int32, (chunk, chunk), 0)
    cidx = lax.broadcasted_iota(jnp.int32, (chunk, chunk), 1)
    row_blk = r // SUB_BLOCK
    lane_blk = cidx // SUB_BLOCK
    same = row_blk == lane_blk
    l_sub = (same & (cidx <= r)).astype(BF16)
    l_suf = (same & (cidx > r)).astype(BF16)
    l_blk = (lane_blk < row_blk).astype(BF16)
    l_stack = jnp.concatenate([l_sub, l_suf, l_blk], axis=0)
    t_in = lax.broadcasted_iota(jnp.int32, (n_sub, SUB_BLOCK, HEAD_DIM), 1)
    masks = (l_stack, row_blk, lane_blk, cidx, t_in, n_sub)

    def body(ci, carry):
        rows = pl.ds(pl.multiple_of(ci * chunk, chunk), chunk)
        for h in range(heads):
            lanes = slice(h * HEAD_DIM, (h + 1) * HEAD_DIM)
            q = q_ref[rows, lanes].astype(F32)
            g = g_ref[rows, lanes]
            k = k_ref[rows, lanes].astype(F32)
            v = v_ref[rows, lanes]
            o, st_new = _hgrn_chunk(q, g, k, v, st_ref[h], masks)
            st_ref[h] = st_new
            ms = jnp.mean(o * o, axis=-1, keepdims=True)
            y = o * lax.rsqrt(ms + EPS) * onw_ref[:, lanes]
            o_ref[rows, lanes] = (y * sg_ref[rows, lanes].astype(F32)).astype(o_ref.dtype)
        return carry

    lax.fori_loop(0, n_chunks, body, 0)


def _hgrn_mix(q, g, k, v, sg, out_norm_w, *, batch, seq, chunk=128, heads=2, tb=512):
    n_rows, d = q.shape
    n_heads = d // HEAD_DIM
    heads = min(heads, n_heads)
    tb = min(tb, seq)
    chunk = min(chunk, tb)
    assert seq % tb == 0 and tb % chunk == 0 and n_heads % heads == 0 and chunk % SUB_BLOCK == 0
    nt = seq // tb
    blk = pl.BlockSpec((tb, heads * HEAD_DIM), lambda b, h, t: (b * nt + t, h))
    return pl.pallas_call(
        functools.partial(_hgrn_kernel, chunk=chunk, heads=heads),
        grid=(batch, n_heads // heads, nt),
        in_specs=[blk, blk, blk, blk, blk,
                  pl.BlockSpec((1, heads * HEAD_DIM), lambda b, h, t: (0, h))],
        out_specs=blk,
        out_shape=jax.ShapeDtypeStruct((n_rows, d), BF16),
        scratch_shapes=[pltpu.VMEM((heads, HEAD_DIM, HEAD_DIM), F32)],
        compiler_params=pltpu.CompilerParams(
            dimension_semantics=("parallel", "parallel", "arbitrary"),
            vmem_limit_bytes=VMEM_LIMIT_BYTES),
    )(q, g, k, v, sg, out_norm_w.reshape(1, d))


def _fgate_kernel(x_ref, nw_ref, wf_ref, bias_ref, f_ref, carry_ref):
    @pl.when(pl.program_id(1) == 0)
    def _():
        carry_ref[...] = jnp.zeros_like(carry_ref)

    h = _rms_rows(x_ref[...], nw_ref[...]).astype(BF16)
    z = _dot(h, wf_ref[...]) + bias_ref[...]
    ls = jnp.minimum(z, 0.0) - jnp.log1p(jnp.exp(-jnp.abs(z)))
    tt = z.shape[0]
    r = lax.broadcasted_iota(jnp.int32, (tt, tt), 0)
    c = lax.broadcasted_iota(jnp.int32, (tt, tt), 1)
    tri = (c <= r).astype(BF16)
    hi, mid, lo = _split3(ls)
    cum = _dot(tri, hi) + _dot(tri, mid) + _dot(tri, lo)
    f = cum + carry_ref[...]
    f_ref[...] = f
    carry_ref[...] = f[tt - 1:tt]


def _fgate(x, norm_w, wf, bias, *, batch, seq, tt=512):
    n_rows, d = x.shape
    tt = min(tt, seq)
    nt = seq // tt
    return pl.pallas_call(
        _fgate_kernel,
        grid=(batch, nt),
        in_specs=[pl.BlockSpec((tt, d), lambda b, t: (b * nt + t, 0)),
                  pl.BlockSpec((1, d), lambda b, t: (0, 0)),
                  pl.BlockSpec((d, HEAD_DIM), lambda b, t: (0, 0)),
                  pl.BlockSpec((1, HEAD_DIM), lambda b, t: (0, 0))],
        out_specs=pl.BlockSpec((tt, HEAD_DIM), lambda b, t: (b * nt + t, 0)),
        out_shape=jax.ShapeDtypeStruct((n_rows, HEAD_DIM), F32),
        scratch_shapes=[pltpu.VMEM((1, HEAD_DIM), F32)],
        compiler_params=pltpu.CompilerParams(
            dimension_semantics=("parallel", "arbitrary"),
            vmem_limit_bytes=VMEM_LIMIT_BYTES),
    )(x, norm_w.reshape(1, d), wf, bias)


def _fox_kernel(q_ref, k_ref, v_ref, fcol_ref, frow_ref, sg_ref, onw_ref, o_ref, *, tq, n_heads):
    h = pl.program_id(1)
    qi = pl.program_id(2)
    q = q_ref[...]
    lane = lax.broadcasted_iota(jnp.int32, (tq, HEAD_DIM), 1)
    fq = jnp.sum(jnp.where(lane == h, fcol_ref[...], 0.0), axis=-1, keepdims=True)

    def step(kb, carry, masked):
        m, l, acc = carry
        rows = pl.ds(pl.multiple_of(kb * tq, tq), tq)
        kblk = k_ref[rows, :]
        vblk = v_ref[rows, :]
        fk = frow_ref[:, rows]
        s = _dot_nt(q, kblk) + (fq - fk)
        if masked:
            rr = lax.broadcasted_iota(jnp.int32, (tq, tq), 0)
            cc = lax.broadcasted_iota(jnp.int32, (tq, tq), 1)
            s = jnp.where(rr >= cc, s, -jnp.inf)
        m_new = jnp.maximum(m, jnp.max(s, axis=-1, keepdims=True))
        alpha = jnp.exp(m - m_new)
        p = jnp.exp(s - m_new)
        l_new = alpha * l + jnp.sum(p, axis=-1, keepdims=True)
        acc_new = alpha * acc + _dot(p.astype(BF16), vblk)
        return m_new, l_new, acc_new

    init = (jnp.full((tq, 1), NEG_BIG, F32), jnp.zeros((tq, 1), F32), jnp.zeros((tq, HEAD_DIM), F32))
    carry = lax.fori_loop(0, qi, functools.partial(step, masked=False), init)
    m, l, acc = step(qi, carry, True)
    o = acc / l
    ms = jnp.mean(o * o, axis=-1, keepdims=True)
    y = o * lax.rsqrt(ms + EPS) * onw_ref[...]
    o_ref[...] = (y * sg_ref[...].astype(F32)).astype(o_ref.dtype)


def _fox_mix(q, k, v, fcol, frow, sg, out_norm_w, *, batch, seq, tq=512):
    n_rows, d = q.shape
    n_heads = d // HEAD_DIM
    tq = min(tq, seq)
    nq = seq // tq
    qblk = pl.BlockSpec((tq, HEAD_DIM), lambda b, h, i: (b * nq + i, h))
    kvblk = pl.BlockSpec((seq, HEAD_DIM), lambda b, h, i: (b, h))
    return pl.pallas_call(
        functools.partial(_fox_kernel, tq=tq, n_heads=n_heads),
        grid=(batch, n_heads, nq),
        in_specs=[qblk, kvblk, kvblk,
                  pl.BlockSpec((tq, HEAD_DIM), lambda b, h, i: (b * nq + i, 0)),
                  pl.BlockSpec((None, 1, seq), lambda b, h, i: (b * n_heads + h, 0, 0)),
                  qblk,
                  pl.BlockSpec((1, HEAD_DIM), lambda b, h, i: (0, h))],
        out_specs=qblk,
        out_shape=jax.ShapeDtypeStruct((n_rows, d), BF16),
        compiler_params=pltpu.CompilerParams(
            dimension_semantics=("parallel", "parallel", "arbitrary"),
            vmem_limit_bytes=VMEM_LIMIT_BYTES),
    )(q, k, v, fcol, frow, sg, out_norm_w.reshape(1, d))


def kernel(x, a_norm_w, a_w_in, a_lb_logits, a_out_norm_w, a_w_out, kv_norm_w, kv_w, kv_f_bias, k_norm_w,
           b_norm_w, b_w_in, b_q_norm_w, b_out_norm_w, b_w_out):
    batch, seq, d = x.shape
    n_heads = d // HEAD_DIM
    n_a = a_w_in.shape[0]
    n_b = b_w_in.shape[0]
    xr = x.reshape(batch * seq, d)

    a_w_in_b = a_w_in.astype(BF16)
    a_w_out_b = a_w_out.astype(BF16)
    b_w_in_b = b_w_in.astype(BF16)
    b_w_out_b = b_w_out.astype(BF16)
    kv_w_b = kv_w[:, :2 * d].astype(BF16)[None]
    wf_b = jnp.pad(kv_w[:, 2 * d:], ((0, 0), (0, HEAD_DIM - n_heads))).astype(BF16)
    f_bias = jnp.pad(kv_f_bias.astype(F32), (0, HEAD_DIM - n_heads)).reshape(1, HEAD_DIM)

    lb_all = jnp.cumsum(jax.nn.softmax(a_lb_logits.astype(F32), axis=0), axis=0)
    lb_all = lb_all - lb_all[0:1]
    ones_row = jnp.ones((1, d), F32)

    for layer in range(n_a):
        q, g, k, v, sg = _rms_proj(
            xr, a_norm_w[layer],
            [(a_w_in_b, layer, 0, ones_row, "cast", (BF16,)),
             (a_w_in_b, layer, d, lb_all[layer].reshape(1, d), "hgate", (F32, BF16)),
             (a_w_in_b, layer, 2 * d, ones_row, "cast", (BF16,)),
             (a_w_in_b, layer, 3 * d, ones_row, "silu", (BF16,))])
        og = _hgrn_mix(q, g, k, v, sg, a_out_norm_w[layer], batch=batch, seq=seq)
        xr = _out_proj(og, a_w_out_b, layer, xr)

    k_norm_row = jnp.tile(k_norm_w.astype(F32), n_heads).reshape(1, d)
    kk, vv = _rms_proj(
        xr, kv_norm_w,
        [(kv_w_b, 0, 0, k_norm_row, "headnorm", (BF16,)),
         (kv_w_b, 0, d, ones_row, "cast", (BF16,))])
    fcol = _fgate(xr, kv_norm_w, wf_b, f_bias, batch=batch, seq=seq)
    frow = fcol[:, :n_heads].reshape(batch, seq, n_heads).transpose(0, 2, 1).reshape(batch * n_heads, 1, seq)

    for j in range(n_b):
        q_norm_row = jnp.tile(b_q_norm_w[j].astype(F32), n_heads).reshape(1, d)
        q, sg = _rms_proj(
            xr, b_norm_w[j],
            [(b_w_in_b, j, 0, q_norm_row, "headnorm", (BF16,)),
             (b_w_in_b, j, d, ones_row, "silu", (BF16,))],
            scale=HEAD_DIM ** -0.5)
        og = _fox_mix(q, kk, vv, fcol, frow, sg, b_out_norm_w[j], batch=batch, seq=seq)
        xr = _out_proj(og, b_w_out_b, j, xr)

    return xr.reshape(batch, seq, d)
```

```python
import functools

import jax
import jax.numpy as jnp
from jax import lax
from jax.experimental import pallas as pl
from jax.experimental.pallas import tpu as pltpu

HEAD_DIM = 128
SUB_BLOCK = 16
HALF_SUB = SUB_BLOCK // 2
EPS = 1e-6
VMEM_LIMIT_BYTES = 56 * 1024 * 1024
NEG_BIG = -1e30
LOG2E = 1.4426950408889634
FOX_TQ = 512
FOX_ROWS = 64
FOX_HEADS = 2

F32 = jnp.float32
BF16 = jnp.bfloat16


def _dot(a, b):
    return jnp.dot(a, b, preferred_element_type=F32)


def _dot_nt(a, b):
    return lax.dot_general(a, b, (((1,), (1,)), ((), ())), preferred_element_type=F32)


def _dot_tn(a, b):
    return lax.dot_general(a, b, (((0,), (0,)), ((), ())), preferred_element_type=F32)


def _split3(x):
    hi = x.astype(BF16)
    r1 = x - hi.astype(F32)
    mid = r1.astype(BF16)
    lo = (r1 - mid.astype(F32)).astype(BF16)
    return hi, mid, lo


def _scan_sub_block(x):
    pos = lax.broadcasted_iota(jnp.int32, x.shape, 0) % SUB_BLOCK
    shift = 1
    while shift < SUB_BLOCK:
        x = x + jnp.where(pos >= shift, pltpu.roll(x, shift, axis=0), 0.0)
        shift *= 2
    return x


def _rms_rows(x, w):
    ms = jnp.mean(x * x, axis=-1, keepdims=True)
    return x * lax.rsqrt(ms + EPS) * w


def _proj_kernel(*refs, kinds, scale):
    n = len(kinds)
    x_ref, nw_ref = refs[0], refs[1]
    w_refs = refs[2:2 + n]
    aux_refs = refs[2 + n:2 + 2 * n]
    n_out = sum(2 if kd == "hgate" else 1 for kd in kinds)
    out_refs = refs[2 + 2 * n:2 + 2 * n + n_out]
    h_ref = refs[2 + 2 * n + n_out]

    @pl.when(pl.program_id(1) == 0)
    def _():
        h_ref[...] = _rms_rows(x_ref[...], nw_ref[...]).astype(BF16)

    h = h_ref[...]
    oi = 0
    for s, kind in enumerate(kinds):
        acc = _dot(h, w_refs[s][...])
        aux = aux_refs[s][...]
        if kind == "cast":
            out_refs[oi][...] = acc.astype(out_refs[oi].dtype)
            oi += 1
        elif kind == "silu":
            out_refs[oi][...] = (acc / (1.0 + jnp.exp(-acc))).astype(out_refs[oi].dtype)
            oi += 1
        elif kind == "headnorm":
            tn = acc.shape[1]
            for c in range(tn // HEAD_DIM):
                sl = slice(c * HEAD_DIM, (c + 1) * HEAD_DIM)
                a = acc[:, sl]
                ms = jnp.mean(a * a, axis=-1, keepdims=True)
                y = a * lax.rsqrt(ms + EPS) * aux[:, sl]
                if scale != 1.0:
                    y = y * scale
                out_refs[oi][:, sl] = y.astype(out_refs[oi].dtype)
            oi += 1
        elif kind == "hgate":
            lb = aux
            e = jnp.exp(-jnp.abs(acc))
            log_sig = jnp.minimum(acc, 0.0) - jnp.log1p(e)
            a = jnp.log(lb)
            c = jnp.log1p(-lb) + log_sig
            g = jnp.maximum(a, c) + jnp.log1p(jnp.exp(-jnp.abs(a - c)))
            r = 1.0 / (1.0 + e)
            sig_neg = jnp.where(acc >= 0.0, e * r, r)
            out_refs[oi][...] = _scan_sub_block(g * LOG2E).astype(out_refs[oi].dtype)
            out_refs[oi + 1][...] = ((1.0 - lb) * sig_neg).astype(out_refs[oi + 1].dtype)
            oi += 2
        else:
            raise ValueError(kind)


def _rms_proj(x, norm_w, streams, *, name, scale=1.0, tm=1024, tn=256):
    n_rows, d = x.shape
    n_cols = streams[0][3].shape[1]
    tm = min(tm, n_rows)
    tn = min(tn, n_cols)
    assert n_rows % tm == 0 and n_cols % tn == 0 and tm % SUB_BLOCK == 0
    kinds = tuple(s[4] for s in streams)

    in_specs = [pl.BlockSpec((tm, d), lambda i, j: (i, 0)),
                pl.BlockSpec((1, d), lambda i, j: (0, 0))]
    args = [x, norm_w.reshape(1, d)]
    for (w, layer, off, aux, kind, _) in streams:
        assert off % tn == 0
        in_specs.append(pl.BlockSpec((None, d, tn),
                                     functools.partial(lambda i, j, l, o: (l, 0, j + o), l=layer, o=off // tn)))
        args.append(w)
    for (w, layer, off, aux, kind, _) in streams:
        in_specs.append(pl.BlockSpec((1, tn), lambda i, j: (0, j)))
        args.append(aux)
    out_shapes, out_specs = [], []
    for (w, layer, off, aux, kind, dts) in streams:
        for dt in dts:
            out_shapes.append(jax.ShapeDtypeStruct((n_rows, n_cols), dt))
            out_specs.append(pl.BlockSpec((tm, tn), lambda i, j: (i, j)))

    return pl.pallas_call(
        functools.partial(_proj_kernel, kinds=kinds, scale=scale),
        grid=(n_rows // tm, n_cols // tn),
        in_specs=in_specs,
        out_specs=out_specs,
        out_shape=out_shapes,
        scratch_shapes=[pltpu.VMEM((tm, d), BF16)],
        compiler_params=pltpu.CompilerParams(
            dimension_semantics=("parallel", "arbitrary"),
            vmem_limit_bytes=VMEM_LIMIT_BYTES),
        name=name,
    )(*args)


def _out_proj_kernel(o_ref, w_ref, x_ref, y_ref):
    y_ref[...] = x_ref[...] + _dot(o_ref[...], w_ref[...])


def _out_proj(o, w3d, layer, x, *, tm=1024, tn=512):
    n_rows, d = x.shape
    tm = min(tm, n_rows)
    tn = min(tn, d)
    return pl.pallas_call(
        _out_proj_kernel,
        grid=(n_rows // tm, d // tn),
        in_specs=[pl.BlockSpec((tm, d), lambda i, j: (i, 0)),
                  pl.BlockSpec((None, d, tn), functools.partial(lambda i, j, l: (l, 0, j), l=layer)),
                  pl.BlockSpec((tm, tn), lambda i, j: (i, j))],
        out_specs=pl.BlockSpec((tm, tn), lambda i, j: (i, j)),
        out_shape=jax.ShapeDtypeStruct((n_rows, d), F32),
        compiler_params=pltpu.CompilerParams(
            dimension_semantics=("parallel", "arbitrary"),
            vmem_limit_bytes=VMEM_LIMIT_BYTES),
        name="out_proj",
    )(o, w3d, x)


def _excl_prefix_rows(x):
    pos = lax.broadcasted_iota(jnp.int32, x.shape, 0)
    inc = x
    shift = 1
    while shift < x.shape[0]:
        inc = inc + jnp.where(pos >= shift, pltpu.roll(inc, shift, axis=0), 0.0)
        shift *= 2
    return inc - x


def _rows_of(x, idx):
    return jnp.concatenate([x[i:i + 1] for i in idx], axis=0)


def _hgrn_chunk(q, k, w2, v, st, fac_ref, ck_ref, consts):
    lane8, sub_row, lvl_masks, same_blk, expand_r, jrow = consts
    c = q.shape[0]
    n_sub = c // SUB_BLOCK
    n_lvl = n_sub.bit_length() - 1

    gt2 = _rows_of(w2, [j * SUB_BLOCK + SUB_BLOCK - 1 for j in range(n_sub)])
    bs2 = _excl_prefix_rows(gt2)
    be2 = bs2 + gt2
    tot2 = be2[n_sub - 1:n_sub]
    fac_ref[0] = gt2
    fac_ref[1] = jnp.exp2(bs2)
    fac_ref[2] = jnp.exp2(tot2 - be2)
    for lvl in range(n_lvl):
        mid = _rows_of(bs2, [((i >> (lvl + 1)) << (lvl + 1)) + (1 << lvl) for i in range(n_sub)])
        upper = ((jrow >> lvl) & 1) == 1
        fac_ref[3 + 2 * lvl] = jnp.exp2(jnp.where(upper, bs2 - mid, -jnp.inf))
        fac_ref[4 + 2 * lvl] = jnp.exp2(jnp.where(upper, -jnp.inf, mid - be2))
    ck_ref[...] = jnp.log2(k) - w2

    qs_parts, kbar_parts, a_parts = [], [], []
    ql_parts = [[] for _ in range(n_lvl)]
    kl_parts = [[] for _ in range(n_lvl)]
    for j in range(n_sub):
        r0 = j * SUB_BLOCK
        sl = slice(r0, r0 + SUB_BLOCK)
        w_j, q_j = w2[sl], q[sl]
        qt = q_j * jnp.exp2(w_j)
        kh = k[sl] * jnp.exp2(fac_ref[0, j:j + 1, :] - w_j)
        qs_parts.append(qt * fac_ref[1, j:j + 1, :])
        kbar_parts.append(kh * fac_ref[2, j:j + 1, :])
        for lvl in range(n_lvl):
            ql_parts[lvl].append(qt * fac_ref[3 + 2 * lvl, j:j + 1, :])
            kl_parts[lvl].append(kh * fac_ref[4 + 2 * lvl, j:j + 1, :])

        w_lo, w_hi = w_j[:HALF_SUB], w_j[HALF_SUB:]
        q_lo, q_hi = q_j[:HALF_SUB], q_j[HALF_SUB:]
        a_lo = jnp.zeros((HALF_SUB, HEAD_DIM), F32)
        a_hi = jnp.zeros((HALF_SUB, HEAD_DIM), F32)
        for s in range(SUB_BLOCK):
            cs = ck_ref[r0 + s:r0 + s + 1, :]
            if s < HALF_SUB:
                col = jnp.sum(q_lo * jnp.exp2(w_lo + cs), axis=-1, keepdims=True)
                a_lo = jnp.where(lane8 == s, col, a_lo)
            col = jnp.sum(q_hi * jnp.exp2(w_hi + cs), axis=-1, keepdims=True)
            a_hi = jnp.where(lane8 == s, col, a_hi)
        a_parts.append(jnp.where(lane8 <= sub_row, a_lo, 0.0))
        a_parts.append(jnp.where(lane8 <= sub_row + HALF_SUB, a_hi, 0.0))

    cat = lambda parts: jnp.concatenate(parts, axis=0)
    o = _dot_nt(cat(qs_parts).astype(BF16), st.astype(BF16))
    a = _dot_nt(cat(ql_parts[n_lvl - 1]).astype(BF16), cat(kl_parts[n_lvl - 1]).astype(BF16))
    for lvl in range(n_lvl - 2, -1, -1):
        a_l = _dot_nt(cat(ql_parts[lvl]).astype(BF16), cat(kl_parts[lvl]).astype(BF16))
        a = jnp.where(lvl_masks[lvl], a_l, a)
    a_diag = _dot(cat(a_parts).astype(BF16), expand_r)
    a = jnp.where(same_blk, a_diag, a)
    o = o + _dot(a.astype(BF16), v)
    st_new = st * jnp.exp2(tot2) + _dot_tn(v, cat(kbar_parts).astype(BF16))
    return o, st_new


def _hgrn_kernel(q_ref, w2_ref, k_ref, v_ref, sg_ref, onw_ref, o_ref, st_ref, fac_ref, ck_ref, *, chunk, heads):
    tb = q_ref.shape[0]
    n_chunks = tb // chunk
    n_sub = chunk // SUB_BLOCK
    n_lvl = n_sub.bit_length() - 1

    @pl.when(pl.program_id(2) == 0)
    def _():
        st_ref[...] = jnp.zeros_like(st_ref)

    r = lax.broadcasted_iota(jnp.int32, (chunk, chunk), 0) // SUB_BLOCK
    cidx = lax.broadcasted_iota(jnp.int32, (chunk, chunk), 1)
    lane_blk = cidx // SUB_BLOCK
    same_blk = r == lane_blk
    lvl_masks = [(r >> (lvl + 1)) == (lane_blk >> (lvl + 1)) for lvl in range(n_lvl)]
    lane8 = lax.broadcasted_iota(jnp.int32, (HALF_SUB, HEAD_DIM), 1)
    sub_row = lax.broadcasted_iota(jnp.int32, (HALF_SUB, HEAD_DIM), 0)
    er = lax.broadcasted_iota(jnp.int32, (HEAD_DIM, chunk), 0)
    ec = lax.broadcasted_iota(jnp.int32, (HEAD_DIM, chunk), 1)
    expand_r = (ec % SUB_BLOCK == er).astype(BF16)
    jrow = lax.broadcasted_iota(jnp.int32, (n_sub, HEAD_DIM), 0)
    consts = (lane8, sub_row, lvl_masks, same_blk, expand_r, jrow)

    def body(ci, carry):
        rows = pl.ds(pl.multiple_of(ci * chunk, chunk), chunk)
        for h in range(heads):
            lanes = slice(h * HEAD_DIM, (h + 1) * HEAD_DIM)
            q = q_ref[rows, lanes].astype(F32)
            k = k_ref[rows, lanes].astype(F32)
            o, st_new = _hgrn_chunk(q, k, w2_ref[rows, lanes], v_ref[rows, lanes], st_ref[h],
                                    fac_ref.at[h], ck_ref.at[h], consts)
            st_ref[h] = st_new
            ms = jnp.mean(o * o, axis=-1, keepdims=True)
            y = o * lax.rsqrt(ms + EPS) * onw_ref[:, lanes]
            o_ref[rows, lanes] = (y * sg_ref[rows, lanes].astype(F32)).astype(o_ref.dtype)
        return carry

    lax.fori_loop(0, n_chunks, body, 0)


def _hgrn_mix(q, w2, k, v, sg, out_norm_w, *, batch, seq, chunk=128, heads=2, tb=512):
    n_rows, d = q.shape
    n_heads = d // HEAD_DIM
    heads = min(heads, n_heads)
    tb = min(tb, seq)
    chunk = min(chunk, tb)
    n_sub = chunk // SUB_BLOCK
    assert seq % tb == 0 and tb % chunk == 0 and n_heads % heads == 0
    assert chunk % SUB_BLOCK == 0 and n_sub & (n_sub - 1) == 0
    nt = seq // tb
    n_fac = 3 + 2 * (n_sub.bit_length() - 1)
    blk = pl.BlockSpec((tb, heads * HEAD_DIM), lambda b, h, t: (b * nt + t, h))
    return pl.pallas_call(
        functools.partial(_hgrn_kernel, chunk=chunk, heads=heads),
        grid=(batch, n_heads // heads, nt),
        in_specs=[blk, blk, blk, blk, blk,
                  pl.BlockSpec((1, heads * HEAD_DIM), lambda b, h, t: (0, h))],
        out_specs=blk,
        out_shape=jax.ShapeDtypeStruct((n_rows, d), BF16),
        scratch_shapes=[pltpu.VMEM((heads, HEAD_DIM, HEAD_DIM), F32),
                        pltpu.VMEM((heads, n_fac, n_sub, HEAD_DIM), F32),
                        pltpu.VMEM((heads, chunk, HEAD_DIM), F32)],
        compiler_params=pltpu.CompilerParams(
            dimension_semantics=("parallel", "parallel", "arbitrary"),
            vmem_limit_bytes=VMEM_LIMIT_BYTES),
        name="hgrn_mix",
    )(q, w2, k, v, sg, out_norm_w.reshape(1, d))


def _fgate_kernel(x_ref, nw_ref, wf_ref, bias_ref, f_ref, fk_ref, carry_ref):
    @pl.when(pl.program_id(1) == 0)
    def _():
        carry_ref[...] = jnp.zeros_like(carry_ref)

    h = _rms_rows(x_ref[...], nw_ref[...]).astype(BF16)
    z = _dot(h, wf_ref[...]) + bias_ref[...]
    ls = jnp.minimum(z, 0.0) - jnp.log1p(jnp.exp(-jnp.abs(z)))
    tt = z.shape[0]
    r = lax.broadcasted_iota(jnp.int32, (tt, tt), 0)
    c = lax.broadcasted_iota(jnp.int32, (tt, tt), 1)
    tri = (c <= r).astype(BF16)
    hi, mid, lo = _split3(ls)
    cum = _dot(tri, hi) + _dot(tri, mid) + _dot(tri, lo)
    f = cum + carry_ref[...]
    f_ref[...] = f
    carry_ref[...] = f[tt - 1:tt]
    for i, term in enumerate(_split3(f * (-LOG2E))):
        fk_ref[:, i * HEAD_DIM:(i + 1) * HEAD_DIM] = term


def _fgate(x, norm_w, wf, bias, *, batch, seq, tt=512):
    n_rows, d = x.shape
    tt = min(tt, seq)
    nt = seq // tt
    return pl.pallas_call(
        _fgate_kernel,
        grid=(batch, nt),
        in_specs=[pl.BlockSpec((tt, d), lambda b, t: (b * nt + t, 0)),
                  pl.BlockSpec((1, d), lambda b, t: (0, 0)),
                  pl.BlockSpec((d, HEAD_DIM), lambda b, t: (0, 0)),
                  pl.BlockSpec((1, HEAD_DIM), lambda b, t: (0, 0))],
        out_specs=[pl.BlockSpec((tt, HEAD_DIM), lambda b, t: (b * nt + t, 0)),
                   pl.BlockSpec((tt, 3 * HEAD_DIM), lambda b, t: (b * nt + t, 0))],
        out_shape=[jax.ShapeDtypeStruct((n_rows, HEAD_DIM), F32),
                   jax.ShapeDtypeStruct((n_rows, 3 * HEAD_DIM), BF16)],
        scratch_shapes=[pltpu.VMEM((1, HEAD_DIM), F32)],
        compiler_params=pltpu.CompilerParams(
            dimension_semantics=("parallel", "arbitrary"),
            vmem_limit_bytes=VMEM_LIMIT_BYTES),
        name="fox_forget_gate",
    )(x, norm_w.reshape(1, d), wf, bias)


def _fox_scores(qa_ref, kt_ref, s_ref, h, cols):
    s_ref[h] = _dot(qa_ref[h], kt_ref[h, :, cols])


def _fox_softmax(s_ref, p_ref, m_ref, l_ref, alpha_ref, h, diag):
    tq, tk = s_ref.shape[1], s_ref.shape[2]
    for r in range(tq // FOX_ROWS):
        row0 = r * FOX_ROWS
        rows = pl.ds(row0, FOX_ROWS)
        n_chunks = (row0 + FOX_ROWS - 1) // HEAD_DIM + 1 if diag else tk // HEAD_DIM
        chunks = [s_ref[h, rows, c * HEAD_DIM:(c + 1) * HEAD_DIM] for c in range(n_chunks)]
        if diag:
            last = n_chunks - 1
            rr = lax.broadcasted_iota(jnp.int32, (FOX_ROWS, HEAD_DIM), 0) + row0
            cc = lax.broadcasted_iota(jnp.int32, (FOX_ROWS, HEAD_DIM), 1) + last * HEAD_DIM
            chunks[last] = jnp.where(rr >= cc, chunks[last], -jnp.inf)
        mx = chunks[0]
        for ch in chunks[1:]:
            mx = jnp.maximum(mx, ch)
        m_prev = m_ref[h, rows, :]
        m_new = jnp.maximum(m_prev, jnp.max(mx, axis=-1, keepdims=True))
        alpha = jnp.exp2(m_prev - m_new)
        psum = None
        for c, ch in enumerate(chunks):
            part = jnp.exp2(ch - m_new)
            psum = part if psum is None else psum + part
            p_ref[h, rows, c * HEAD_DIM:(c + 1) * HEAD_DIM] = part.astype(BF16)
        if n_chunks * HEAD_DIM < tk:
            p_ref[h, rows, n_chunks * HEAD_DIM:] = jnp.zeros((FOX_ROWS, tk - n_chunks * HEAD_DIM), BF16)
        l_ref[h, rows, :] = alpha * l_ref[h, rows, :] + psum
        alpha_ref[h, rows, :] = alpha
        m_ref[h, rows, :] = m_new


def _fox_values(p_ref, v_ref, alpha_ref, acc_ref, h, cols):
    lanes = slice(h * HEAD_DIM, (h + 1) * HEAD_DIM)
    acc_ref[h] = alpha_ref[h] * acc_ref[h] + _dot(p_ref[h], v_ref[cols, lanes])


def _fox_kernel(q_ref, kt_ref, v_ref, fcol_ref, sg_ref, onw_ref, o_ref,
                qa_ref, s_ref, p_ref, m_ref, l_ref, alpha_ref, acc_ref, *, heads):
    hp = pl.program_id(1)
    qi = pl.program_id(2)
    tq = q_ref.shape[0]

    lane = lax.broadcasted_iota(jnp.int32, (tq, HEAD_DIM), 1)
    fcol = fcol_ref[...]
    for h in range(heads):
        fq = jnp.sum(jnp.where(lane == hp * heads + h, fcol, 0.0), axis=-1, keepdims=True) * LOG2E
        hi, mid, lo = _split3(fq)
        faug = jnp.where(lane == 0, hi.astype(F32),
                         jnp.where(lane == 1, mid.astype(F32),
                                   jnp.where(lane == 2, lo.astype(F32),
                                             jnp.where(lane < 6, 1.0, 0.0))))
        qa_ref[h, :, :HEAD_DIM] = q_ref[:, h * HEAD_DIM:(h + 1) * HEAD_DIM]
        qa_ref[h, :, HEAD_DIM:] = faug.astype(BF16)
    m_ref[...] = jnp.full(m_ref.shape, NEG_BIG, F32)
    l_ref[...] = jnp.zeros(l_ref.shape, F32)
    acc_ref[...] = jnp.zeros(acc_ref.shape, F32)

    def block(cols, diag):
        for h in range(heads):
            _fox_scores(qa_ref, kt_ref, s_ref, h, cols)
        for h in range(heads):
            _fox_softmax(s_ref, p_ref, m_ref, l_ref, alpha_ref, h, diag)
        for h in range(heads):
            _fox_values(p_ref, v_ref, alpha_ref, acc_ref, h, cols)

    def body(kb, carry):
        block(pl.ds(pl.multiple_of(kb * tq, tq), tq), False)
        return carry

    lax.fori_loop(0, qi, body, 0)
    block(pl.ds(pl.multiple_of(qi * tq, tq), tq), True)

    for h in range(heads):
        lanes = slice(h * HEAD_DIM, (h + 1) * HEAD_DIM)
        l = jnp.sum(l_ref[h], axis=-1, keepdims=True)
        o = acc_ref[h] / l
        ms = jnp.mean(o * o, axis=-1, keepdims=True)
        y = o * lax.rsqrt(ms + EPS) * onw_ref[:, lanes]
        o_ref[:, lanes] = (y * sg_ref[:, lanes].astype(F32)).astype(o_ref.dtype)


def _fox_mix(q, kt_aug, v, fcol, sg, out_norm_w, *, batch, seq):
    n_rows, d = q.shape
    n_heads = d // HEAD_DIM
    heads = min(FOX_HEADS, n_heads)
    tq = min(FOX_TQ, seq)
    assert seq % tq == 0 and tq % FOX_ROWS == 0 and tq % HEAD_DIM == 0 and n_heads % heads == 0
    nq = seq // tq
    n_pairs = n_heads // heads
    qblk = pl.BlockSpec((tq, heads * HEAD_DIM), lambda b, h, i: (b * nq + i, h))
    return pl.pallas_call(
        functools.partial(_fox_kernel, heads=heads),
        grid=(batch, n_pairs, nq),
        in_specs=[qblk,
                  pl.BlockSpec((heads, 2 * HEAD_DIM, seq), lambda b, h, i: (b * n_pairs + h, 0, 0)),
                  pl.BlockSpec((seq, heads * HEAD_DIM), lambda b, h, i: (b, h)),
                  pl.BlockSpec((tq, HEAD_DIM), lambda b, h, i: (b * nq + i, 0)),
                  qblk,
                  pl.BlockSpec((1, heads * HEAD_DIM), lambda b, h, i: (0, h))],
        out_specs=qblk,
        out_shape=jax.ShapeDtypeStruct((n_rows, d), BF16),
        scratch_shapes=[pltpu.VMEM((heads, tq, 2 * HEAD_DIM), BF16),
                        pltpu.VMEM((heads, tq, tq), F32),
                        pltpu.VMEM((heads, tq, tq), BF16),
                        pltpu.VMEM((heads, tq, HEAD_DIM), F32),
                        pltpu.VMEM((heads, tq, HEAD_DIM), F32),
                        pltpu.VMEM((heads, tq, HEAD_DIM), F32),
                        pltpu.VMEM((heads, tq, HEAD_DIM), F32)],
        compiler_params=pltpu.CompilerParams(
            dimension_semantics=("parallel", "parallel", "arbitrary"),
            vmem_limit_bytes=VMEM_LIMIT_BYTES),
        name="fox_attention",
    )(q, kt_aug, v, fcol, sg, out_norm_w.reshape(1, d))


def _fox_keys(kk, fk3, *, batch, seq):
    n_rows, d = kk.shape
    n_heads = d // HEAD_DIM
    kt = kk.reshape(batch, seq, n_heads, HEAD_DIM).transpose(0, 2, 3, 1)
    terms = fk3.reshape(batch, seq, 3, HEAD_DIM)[:, :, :, :n_heads].transpose(0, 3, 2, 1)
    ones = jnp.ones_like(terms)
    extra = jnp.concatenate([ones, terms], axis=2)
    extra = jnp.pad(extra, ((0, 0), (0, 0), (0, HEAD_DIM - 6), (0, 0)))
    return jnp.concatenate([kt, extra], axis=2).reshape(batch * n_heads, 2 * HEAD_DIM, seq)


def kernel(x, a_norm_w, a_w_in, a_lb_logits, a_out_norm_w, a_w_out, kv_norm_w, kv_w, kv_f_bias, k_norm_w,
           b_norm_w, b_w_in, b_q_norm_w, b_out_norm_w, b_w_out):
    batch, seq, d = x.shape
    n_heads = d // HEAD_DIM
    n_a = a_w_in.shape[0]
    n_b = b_w_in.shape[0]
    xr = x.reshape(batch * seq, d)

    a_w_in_b = a_w_in.astype(BF16)
    a_w_out_b = a_w_out.astype(BF16)
    b_w_in_b = b_w_in.astype(BF16)
    b_w_out_b = b_w_out.astype(BF16)
    kv_w_b = kv_w[:, :2 * d].astype(BF16)[None]
    wf_b = jnp.pad(kv_w[:, 2 * d:], ((0, 0), (0, HEAD_DIM - n_heads))).astype(BF16)
    f_bias = jnp.pad(kv_f_bias.astype(F32), (0, HEAD_DIM - n_heads)).reshape(1, HEAD_DIM)

    lb_all = jnp.cumsum(jax.nn.softmax(a_lb_logits.astype(F32), axis=0), axis=0)
    lb_all = lb_all - lb_all[0:1]
    ones_row = jnp.ones((1, d), F32)

    for layer in range(n_a):
        q, w2, k, v, sg = _rms_proj(
            xr, a_norm_w[layer],
            [(a_w_in_b, layer, 0, ones_row, "cast", (BF16,)),
             (a_w_in_b, layer, d, lb_all[layer].reshape(1, d), "hgate", (F32, BF16)),
             (a_w_in_b, layer, 2 * d, ones_row, "cast", (BF16,)),
             (a_w_in_b, layer, 3 * d, ones_row, "silu", (BF16,))],
            name="hgrn_in_proj")
        og = _hgrn_mix(q, w2, k, v, sg, a_out_norm_w[layer], batch=batch, seq=seq)
        xr = _out_proj(og, a_w_out_b, layer, xr)

    k_norm_row = jnp.tile(k_norm_w.astype(F32), n_heads).reshape(1, d)
    kk, vv = _rms_proj(
        xr, kv_norm_w,
        [(kv_w_b, 0, 0, k_norm_row, "headnorm", (BF16,)),
         (kv_w_b, 0, d, ones_row, "cast", (BF16,))],
        name="fox_kv_proj")
    fcol, fk3 = _fgate(xr, kv_norm_w, wf_b, f_bias, batch=batch, seq=seq)
    kt_aug = _fox_keys(kk, fk3, batch=batch, seq=seq)

    for j in range(n_b):
        q_norm_row = jnp.tile(b_q_norm_w[j].astype(F32), n_heads).reshape(1, d)
        q, sg = _rms_proj(
            xr, b_norm_w[j],
            [(b_w_in_b, j, 0, q_norm_row, "headnorm", (BF16,)),
             (b_w_in_b, j, d, ones_row, "silu", (BF16,))],
            scale=HEAD_DIM ** -0.5 * LOG2E, name="fox_in_proj")
        og = _fox_mix(q, kt_aug, vv, fcol, sg, b_out_norm_w[j], batch=batch, seq=seq)
        xr = _out_proj(og, b_w_out_b, j, xr)

    return xr.reshape(batch, seq, d)
```

```python
import functools

import jax
import jax.numpy as jnp
from jax import lax
from jax.experimental import pallas as pl
from jax.experimental.pallas import tpu as pltpu

HEAD_DIM = 128
SUB_BLOCK = 16
HALF_SUB = SUB_BLOCK // 2
BF16_SUBLANES = 16
EPS = 1e-6
VMEM_LIMIT_BYTES = 56 * 1024 * 1024
NEG_BIG = -1e30
LOG2E = 1.4426950408889634
FOX_TQ = 512
FOX_ROWS = 64
FOX_HEADS = 2

F32 = jnp.float32
BF16 = jnp.bfloat16


def _dot(a, b):
    return jnp.dot(a, b, preferred_element_type=F32)


def _dot_nt(a, b):
    return lax.dot_general(a, b, (((1,), (1,)), ((), ())), preferred_element_type=F32)


def _dot_tn(a, b):
    return lax.dot_general(a, b, (((0,), (0,)), ((), ())), preferred_element_type=F32)


def _split3(x):
    hi = x.astype(BF16)
    r1 = x - hi.astype(F32)
    mid = r1.astype(BF16)
    lo = (r1 - mid.astype(F32)).astype(BF16)
    return hi, mid, lo


def _scan_sub_block(x):
    n, c = x.shape
    x3 = x.reshape(n // HALF_SUB, HALF_SUB, c)
    pos = lax.broadcasted_iota(jnp.int32, x3.shape, 1)
    shift = 1
    while shift < HALF_SUB:
        x3 = x3 + jnp.where(pos >= shift, pltpu.roll(x3, shift, axis=1), 0.0)
        shift *= 2
    tile = lax.broadcasted_iota(jnp.int32, x3.shape, 0)
    carry = jnp.roll(jnp.broadcast_to(x3[:, HALF_SUB - 1:, :], x3.shape), 1, axis=0)
    x3 = x3 + jnp.where(tile % 2 == 1, carry, 0.0)
    return x3.reshape(n, c)


def _rms_rows(x, w):
    ms = jnp.mean(x * x, axis=-1, keepdims=True)
    return x * lax.rsqrt(ms + EPS) * w


def _proj_kernel(*refs, kinds, scale):
    n = len(kinds)
    h_ref = refs[0]
    w_refs = refs[1:1 + n]
    aux_refs = refs[1 + n:1 + 2 * n]
    out_refs = refs[1 + 2 * n:]

    h = h_ref[...]
    oi = 0
    for s, kind in enumerate(kinds):
        acc = _dot(h, w_refs[s][...])
        aux = aux_refs[s][...]
        if kind == "cast":
            out_refs[oi][...] = acc.astype(out_refs[oi].dtype)
            oi += 1
        elif kind == "silu":
            out_refs[oi][...] = (acc / (1.0 + jnp.exp(-acc))).astype(out_refs[oi].dtype)
            oi += 1
        elif kind in ("headnorm", "headnorm_t"):
            tn = acc.shape[1]
            for c in range(tn // HEAD_DIM):
                sl = slice(c * HEAD_DIM, (c + 1) * HEAD_DIM)
                a = acc[:, sl]
                ms = jnp.mean(a * a, axis=-1, keepdims=True)
                y = a * lax.rsqrt(ms + EPS) * aux[:, sl]
                if scale != 1.0:
                    y = y * scale
                if kind == "headnorm_t":
                    out_refs[oi][c] = y.T.astype(out_refs[oi].dtype)
                else:
                    out_refs[oi][:, sl] = y.astype(out_refs[oi].dtype)
            oi += 1
        elif kind == "hgate":
            lb = aux
            e = jnp.exp2(jnp.abs(acc) * (-LOG2E))
            one_e = 1.0 + e
            log2_sig = jnp.minimum(acc, 0.0) * LOG2E - jnp.log2(one_e)
            a = jnp.log2(lb)
            c = jnp.log2(1.0 - lb) + log2_sig
            g2 = jnp.maximum(a, c) + jnp.log2(1.0 + jnp.exp2(-jnp.abs(a - c)))
            r = 1.0 / one_e
            sig_neg = jnp.where(acc >= 0.0, e * r, r)
            out_refs[oi][...] = _scan_sub_block(g2).astype(out_refs[oi].dtype)
            out_refs[oi + 1][...] = ((1.0 - lb) * sig_neg).astype(out_refs[oi + 1].dtype)
            oi += 2
        else:
            raise ValueError(kind)


def _proj(h, streams, *, name, seq, scale=1.0, tm=2048, tn=256):
    n_rows, d = h.shape
    n_cols = streams[0][3].shape[1]
    tm = min(tm, seq)
    tn = min(tn, n_cols)
    assert seq % tm == 0 and n_rows % seq == 0 and n_cols % tn == 0 and tm % SUB_BLOCK == 0
    kinds = tuple(s[4] for s in streams)
    t_tiles = seq // tm
    heads_per_tile = tn // HEAD_DIM
    n_head_tiles = n_cols // tn

    in_specs = [pl.BlockSpec((tm, d), lambda i, j: (i, 0))]
    args = [h]
    for (w, layer, off, aux, kind, _) in streams:
        assert off % tn == 0
        in_specs.append(pl.BlockSpec((None, d, tn),
                                     functools.partial(lambda i, j, l, o: (l, 0, j + o), l=layer, o=off // tn)))
        args.append(w)
    for (w, layer, off, aux, kind, _) in streams:
        in_specs.append(pl.BlockSpec((1, tn), lambda i, j: (0, j)))
        args.append(aux)
    out_shapes, out_specs = [], []
    for (w, layer, off, aux, kind, dts) in streams:
        for dt in dts:
            if kind == "headnorm_t":
                out_shapes.append(jax.ShapeDtypeStruct((n_rows // seq * n_cols // HEAD_DIM, HEAD_DIM, seq), dt))
                out_specs.append(pl.BlockSpec(
                    (heads_per_tile, HEAD_DIM, tm),
                    lambda i, j: ((i // t_tiles) * n_head_tiles + j, 0, i % t_tiles)))
            else:
                out_shapes.append(jax.ShapeDtypeStruct((n_rows, n_cols), dt))
                out_specs.append(pl.BlockSpec((tm, tn), lambda i, j: (i, j)))

    return pl.pallas_call(
        functools.partial(_proj_kernel, kinds=kinds, scale=scale),
        grid=(n_rows // tm, n_cols // tn),
        in_specs=in_specs,
        out_specs=out_specs,
        out_shape=out_shapes,
        compiler_params=pltpu.CompilerParams(
            dimension_semantics=("parallel", "parallel"),
            vmem_limit_bytes=VMEM_LIMIT_BYTES),
        name=name,
    )(*args)


def _norm_kernel(x_ref, nw_ref, h_ref):
    h_ref[...] = _rms_rows(x_ref[...], nw_ref[...]).astype(h_ref.dtype)


def _norm(x, norm_w, *, tm=1024):
    n_rows, d = x.shape
    tm = min(tm, n_rows)
    assert n_rows % tm == 0
    return pl.pallas_call(
        _norm_kernel,
        grid=(n_rows // tm,),
        in_specs=[pl.BlockSpec((tm, d), lambda i: (i, 0)),
                  pl.BlockSpec((1, d), lambda i: (0, 0))],
        out_specs=pl.BlockSpec((tm, d), lambda i: (i, 0)),
        out_shape=jax.ShapeDtypeStruct((n_rows, d), BF16),
        compiler_params=pltpu.CompilerParams(
            dimension_semantics=("parallel",),
            vmem_limit_bytes=VMEM_LIMIT_BYTES),
        name="input_norm",
    )(x, norm_w.reshape(1, d))


def _out_proj_kernel(o_ref, w_ref, x_ref, *refs):
    n_next = (len(refs) - 1) // 2
    nw_refs, y_ref, h_refs = refs[:n_next], refs[n_next], refs[n_next + 1:]
    y = x_ref[...] + _dot(o_ref[...], w_ref[...])
    y_ref[...] = y
    if n_next:
        yn = y * lax.rsqrt(jnp.mean(y * y, axis=-1, keepdims=True) + EPS)
        for nw_ref, h_ref in zip(nw_refs, h_refs):
            h_ref[...] = (yn * nw_ref[...]).astype(h_ref.dtype)


def _out_proj(o, w3d, layer, x, next_norm_ws, *, tm=512):
    n_rows, d = x.shape
    tm = min(tm, n_rows)
    assert n_rows % tm == 0
    rows = pl.BlockSpec((tm, d), lambda i: (i, 0))
    gain = pl.BlockSpec((1, d), lambda i: (0, 0))
    n_next = len(next_norm_ws)
    outs = pl.pallas_call(
        _out_proj_kernel,
        grid=(n_rows // tm,),
        in_specs=[rows, pl.BlockSpec((None, d, d), functools.partial(lambda i, l: (l, 0, 0), l=layer)), rows]
                 + [gain] * n_next,
        out_specs=[rows] * (1 + n_next),
        out_shape=[jax.ShapeDtypeStruct((n_rows, d), F32)]
                  + [jax.ShapeDtypeStruct((n_rows, d), BF16)] * n_next,
        compiler_params=pltpu.CompilerParams(
            dimension_semantics=("parallel",),
            vmem_limit_bytes=VMEM_LIMIT_BYTES),
        name="out_proj",
    )(o, w3d, x, *[w.reshape(1, d) for w in next_norm_ws])
    return outs[0], list(outs[1:])


def _excl_prefix_rows(x):
    pos = lax.broadcasted_iota(jnp.int32, x.shape, 0)
    inc = x
    shift = 1
    while shift < x.shape[0]:
        inc = inc + jnp.where(pos >= shift, pltpu.roll(inc, shift, axis=0), 0.0)
        shift *= 2
    return inc - x


def _rows_of(x, idx):
    return jnp.concatenate([x[i:i + 1] for i in idx], axis=0)


def _hgrn_chunk(q, k, w2, v, st, fac_ref, ck_ref, consts):
    lane8, sub_row, lvl_masks, same_blk, expand_r, jrow = consts
    c = q.shape[0]
    n_sub = c // SUB_BLOCK
    n_lvl = n_sub.bit_length() - 1

    gt2 = _rows_of(w2, [j * SUB_BLOCK + SUB_BLOCK - 1 for j in range(n_sub)])
    bs2 = _excl_prefix_rows(gt2)
    be2 = bs2 + gt2
    tot2 = be2[n_sub - 1:n_sub]
    fac_ref[0] = gt2
    fac_ref[1] = jnp.exp2(bs2)
    fac_ref[2] = jnp.exp2(tot2 - be2)
    for lvl in range(n_lvl):
        mid = _rows_of(bs2, [((i >> (lvl + 1)) << (lvl + 1)) + (1 << lvl) for i in range(n_sub)])
        upper = ((jrow >> lvl) & 1) == 1
        fac_ref[3 + 2 * lvl] = jnp.exp2(jnp.where(upper, bs2 - mid, -jnp.inf))
        fac_ref[4 + 2 * lvl] = jnp.exp2(jnp.where(upper, -jnp.inf, mid - be2))
    ck_ref[...] = jnp.log2(k) - w2

    qs_parts, kbar_parts, a_parts = [], [], []
    ql_parts = [[] for _ in range(n_lvl)]
    kl_parts = [[] for _ in range(n_lvl)]
    for j in range(n_sub):
        r0 = j * SUB_BLOCK
        sl = slice(r0, r0 + SUB_BLOCK)
        w_j, q_j = w2[sl], q[sl]
        qt = q_j * jnp.exp2(w_j)
        kh = k[sl] * jnp.exp2(fac_ref[0, j:j + 1, :] - w_j)
        qs_parts.append(qt * fac_ref[1, j:j + 1, :])
        kbar_parts.append(kh * fac_ref[2, j:j + 1, :])
        for lvl in range(n_lvl):
            ql_parts[lvl].append(qt * fac_ref[3 + 2 * lvl, j:j + 1, :])
            kl_parts[lvl].append(kh * fac_ref[4 + 2 * lvl, j:j + 1, :])

        w_lo, w_hi = w_j[:HALF_SUB], w_j[HALF_SUB:]
        q_lo, q_hi = q_j[:HALF_SUB], q_j[HALF_SUB:]
        a_lo = jnp.zeros((HALF_SUB, HEAD_DIM), F32)
        a_hi = jnp.zeros((HALF_SUB, HEAD_DIM), F32)
        for s in range(SUB_BLOCK):
            cs = ck_ref[r0 + s:r0 + s + 1, :]
            if s < HALF_SUB:
                col = jnp.sum(q_lo * jnp.exp2(w_lo + cs), axis=-1, keepdims=True)
                a_lo = jnp.where(lane8 == s, col, a_lo)
            col = jnp.sum(q_hi * jnp.exp2(w_hi + cs), axis=-1, keepdims=True)
            a_hi = jnp.where(lane8 == s, col, a_hi)
        a_parts.append(jnp.where(lane8 <= sub_row, a_lo, 0.0))
        a_parts.append(jnp.where(lane8 <= sub_row + HALF_SUB, a_hi, 0.0))

    cat = lambda parts: jnp.concatenate(parts, axis=0)
    o = _dot_nt(cat(qs_parts).astype(BF16), st.astype(BF16))
    a = _dot_nt(cat(ql_parts[n_lvl - 1]).astype(BF16), cat(kl_parts[n_lvl - 1]).astype(BF16))
    for lvl in range(n_lvl - 2, -1, -1):
        a_l = _dot_nt(cat(ql_parts[lvl]).astype(BF16), cat(kl_parts[lvl]).astype(BF16))
        a = jnp.where(lvl_masks[lvl], a_l, a)
    a_diag = _dot(cat(a_parts).astype(BF16), expand_r)
    a = jnp.where(same_blk, a_diag, a)
    o = o + _dot(a.astype(BF16), v)
    st_new = st * jnp.exp2(tot2) + _dot_tn(v, cat(kbar_parts).astype(BF16))
    return o, st_new


def _hgrn_kernel(q_ref, w2_ref, k_ref, v_ref, sg_ref, onw_ref, o_ref, st_ref, fac_ref, ck_ref, *, chunk, heads):
    tb = q_ref.shape[0]
    n_chunks = tb // chunk
    n_sub = chunk // SUB_BLOCK
    n_lvl = n_sub.bit_length() - 1

    @pl.when(pl.program_id(2) == 0)
    def _():
        st_ref[...] = jnp.zeros_like(st_ref)

    r = lax.broadcasted_iota(jnp.int32, (chunk, chunk), 0) // SUB_BLOCK
    cidx = lax.broadcasted_iota(jnp.int32, (chunk, chunk), 1)
    lane_blk = cidx // SUB_BLOCK
    same_blk = r == lane_blk
    lvl_masks = [(r >> (lvl + 1)) == (lane_blk >> (lvl + 1)) for lvl in range(n_lvl)]
    lane8 = lax.broadcasted_iota(jnp.int32, (HALF_SUB, HEAD_DIM), 1)
    sub_row = lax.broadcasted_iota(jnp.int32, (HALF_SUB, HEAD_DIM), 0)
    er = lax.broadcasted_iota(jnp.int32, (HEAD_DIM, chunk), 0)
    ec = lax.broadcasted_iota(jnp.int32, (HEAD_DIM, chunk), 1)
    expand_r = (ec % SUB_BLOCK == er).astype(BF16)
    jrow = lax.broadcasted_iota(jnp.int32, (n_sub, HEAD_DIM), 0)
    consts = (lane8, sub_row, lvl_masks, same_blk, expand_r, jrow)

    def body(ci, carry):
        rows = pl.ds(pl.multiple_of(ci * chunk, chunk), chunk)
        for h in range(heads):
            lanes = slice(h * HEAD_DIM, (h + 1) * HEAD_DIM)
            q = q_ref[rows, lanes].astype(F32)
            k = k_ref[rows, lanes].astype(F32)
            o, st_new = _hgrn_chunk(q, k, w2_ref[rows, lanes], v_ref[rows, lanes], st_ref[h],
                                    fac_ref.at[h], ck_ref.at[h], consts)
            st_ref[h] = st_new
            ms = jnp.mean(o * o, axis=-1, keepdims=True)
            y = o * lax.rsqrt(ms + EPS) * onw_ref[:, lanes]
            o_ref[rows, lanes] = (y * sg_ref[rows, lanes].astype(F32)).astype(o_ref.dtype)
        return carry

    lax.fori_loop(0, n_chunks, body, 0)


def _hgrn_mix(q, w2, k, v, sg, out_norm_w, *, batch, seq, chunk=128, heads=2, tb=512):
    n_rows, d = q.shape
    n_heads = d // HEAD_DIM
    heads = min(heads, n_heads)
    tb = min(tb, seq)
    chunk = min(chunk, tb)
    n_sub = chunk // SUB_BLOCK
    assert seq % tb == 0 and tb % chunk == 0 and n_heads % heads == 0
    assert chunk % SUB_BLOCK == 0 and n_sub & (n_sub - 1) == 0
    nt = seq // tb
    n_fac = 3 + 2 * (n_sub.bit_length() - 1)
    blk = pl.BlockSpec((tb, heads * HEAD_DIM), lambda b, h, t: (b * nt + t, h))
    return pl.pallas_call(
        functools.partial(_hgrn_kernel, chunk=chunk, heads=heads),
        grid=(batch, n_heads // heads, nt),
        in_specs=[blk, blk, blk, blk, blk,
                  pl.BlockSpec((1, heads * HEAD_DIM), lambda b, h, t: (0, h))],
        out_specs=blk,
        out_shape=jax.ShapeDtypeStruct((n_rows, d), BF16),
        scratch_shapes=[pltpu.VMEM((heads, HEAD_DIM, HEAD_DIM), F32),
                        pltpu.VMEM((heads, n_fac, n_sub, HEAD_DIM), F32),
                        pltpu.VMEM((heads, chunk, HEAD_DIM), F32)],
        compiler_params=pltpu.CompilerParams(
            dimension_semantics=("parallel", "parallel", "arbitrary"),
            vmem_limit_bytes=VMEM_LIMIT_BYTES),
        name="hgrn_mix",
    )(q, w2, k, v, sg, out_norm_w.reshape(1, d))


def _fgate_kernel(h_ref, wf_ref, bias_ref, f_ref, fk_ref, carry_ref):
    @pl.when(pl.program_id(1) == 0)
    def _():
        carry_ref[...] = jnp.zeros_like(carry_ref)

    z = _dot(h_ref[...], wf_ref[...]) + bias_ref[...]
    ls = jnp.minimum(z, 0.0) - jnp.log1p(jnp.exp(-jnp.abs(z)))
    tt = z.shape[0]
    r = lax.broadcasted_iota(jnp.int32, (tt, tt), 0)
    c = lax.broadcasted_iota(jnp.int32, (tt, tt), 1)
    tri = (c <= r).astype(BF16)
    hi, mid, lo = _split3(ls)
    cum = _dot(tri, hi) + _dot(tri, mid) + _dot(tri, lo)
    f = cum + carry_ref[...]
    f_ref[...] = f
    carry_ref[...] = f[tt - 1:tt]
    for i, term in enumerate(_split3(f * (-LOG2E))):
        fk_ref[:, i * HEAD_DIM:(i + 1) * HEAD_DIM] = term


def _fgate(h, wf, bias, *, batch, seq, tt=512):
    n_rows, d = h.shape
    tt = min(tt, seq)
    nt = seq // tt
    return pl.pallas_call(
        _fgate_kernel,
        grid=(batch, nt),
        in_specs=[pl.BlockSpec((tt, d), lambda b, t: (b * nt + t, 0)),
                  pl.BlockSpec((d, HEAD_DIM), lambda b, t: (0, 0)),
                  pl.BlockSpec((1, HEAD_DIM), lambda b, t: (0, 0))],
        out_specs=[pl.BlockSpec((tt, HEAD_DIM), lambda b, t: (b * nt + t, 0)),
                   pl.BlockSpec((tt, 3 * HEAD_DIM), lambda b, t: (b * nt + t, 0))],
        out_shape=[jax.ShapeDtypeStruct((n_rows, HEAD_DIM), F32),
                   jax.ShapeDtypeStruct((n_rows, 3 * HEAD_DIM), BF16)],
        scratch_shapes=[pltpu.VMEM((1, HEAD_DIM), F32)],
        compiler_params=pltpu.CompilerParams(
            dimension_semantics=("parallel", "arbitrary"),
            vmem_limit_bytes=VMEM_LIMIT_BYTES),
        name="fox_forget_gate",
    )(h, wf, bias)


def _fox_scores(qa_ref, kt_ref, s_ref, h, cols):
    s_ref[h] = _dot(qa_ref[h], kt_ref[h, :, cols])


def _fox_softmax(s_ref, p_ref, m_ref, l_ref, alpha_ref, h, diag):
    tq, tk = s_ref.shape[1], s_ref.shape[2]
    for r in range(tq // FOX_ROWS):
        row0 = r * FOX_ROWS
        rows = pl.ds(row0, FOX_ROWS)
        n_chunks = (row0 + FOX_ROWS - 1) // HEAD_DIM + 1 if diag else tk // HEAD_DIM
        chunks = [s_ref[h, rows, c * HEAD_DIM:(c + 1) * HEAD_DIM] for c in range(n_chunks)]
        if diag:
            last = n_chunks - 1
            rr = lax.broadcasted_iota(jnp.int32, (FOX_ROWS, HEAD_DIM), 0) + row0
            cc = lax.broadcasted_iota(jnp.int32, (FOX_ROWS, HEAD_DIM), 1) + last * HEAD_DIM
            chunks[last] = jnp.where(rr >= cc, chunks[last], -jnp.inf)
        mx = chunks[0]
        for ch in chunks[1:]:
            mx = jnp.maximum(mx, ch)
        m_prev = m_ref[h, rows, :]
        m_new = jnp.maximum(m_prev, jnp.max(mx, axis=-1, keepdims=True))
        alpha = jnp.exp2(m_prev - m_new)
        psum = None
        for c, ch in enumerate(chunks):
            part = jnp.exp2(ch - m_new)
            psum = part if psum is None else psum + part
            p_ref[h, rows, c * HEAD_DIM:(c + 1) * HEAD_DIM] = part.astype(BF16)
        if n_chunks * HEAD_DIM < tk:
            p_ref[h, rows, n_chunks * HEAD_DIM:] = jnp.zeros((FOX_ROWS, tk - n_chunks * HEAD_DIM), BF16)
        l_ref[h, rows, :] = alpha * l_ref[h, rows, :] + psum
        alpha_ref[h, rows, :] = alpha
        m_ref[h, rows, :] = m_new


def _fox_values(p_ref, v_ref, alpha_ref, acc_ref, h, cols):
    lanes = slice(h * HEAD_DIM, (h + 1) * HEAD_DIM)
    acc_ref[h] = alpha_ref[h] * acc_ref[h] + _dot(p_ref[h], v_ref[cols, lanes])


def _fox_kernel(q_ref, ktr_ref, fk_ref, v_ref, fcol_ref, sg_ref, onw_ref, o_ref,
                kt_ref, qa_ref, s_ref, p_ref, m_ref, l_ref, alpha_ref, acc_ref, *, heads):
    hp = pl.program_id(1)
    qi = pl.program_id(2)
    tq = q_ref.shape[0]

    @pl.when(qi == 0)
    def _():
        n_bias = fk_ref.shape[1]
        kt_ref[:, :HEAD_DIM, :] = ktr_ref[...]
        kt_ref[:, HEAD_DIM:HEAD_DIM + n_bias, :] = fk_ref[...]
        kt_ref[:, HEAD_DIM + n_bias:, :] = jnp.zeros(
            (heads, HEAD_DIM - n_bias, kt_ref.shape[2]), BF16)

    lane = lax.broadcasted_iota(jnp.int32, (tq, HEAD_DIM), 1)
    fcol = fcol_ref[...]
    for h in range(heads):
        fq = jnp.sum(jnp.where(lane == hp * heads + h, fcol, 0.0), axis=-1, keepdims=True) * LOG2E
        hi, mid, lo = _split3(fq)
        faug = jnp.where(lane == 0, hi.astype(F32),
                         jnp.where(lane == 1, mid.astype(F32),
                                   jnp.where(lane == 2, lo.astype(F32),
                                             jnp.where(lane < 6, 1.0, 0.0))))
        qa_ref[h, :, :HEAD_DIM] = q_ref[:, h * HEAD_DIM:(h + 1) * HEAD_DIM]
        qa_ref[h, :, HEAD_DIM:] = faug.astype(BF16)
    m_ref[...] = jnp.full(m_ref.shape, NEG_BIG, F32)
    l_ref[...] = jnp.zeros(l_ref.shape, F32)
    acc_ref[...] = jnp.zeros(acc_ref.shape, F32)

    def block(cols, diag):
        for h in range(heads):
            _fox_scores(qa_ref, kt_ref, s_ref, h, cols)
        for h in range(heads):
            _fox_softmax(s_ref, p_ref, m_ref, l_ref, alpha_ref, h, diag)
        for h in range(heads):
            _fox_values(p_ref, v_ref, alpha_ref, acc_ref, h, cols)

    def body(kb, carry):
        block(pl.ds(pl.multiple_of(kb * tq, tq), tq), False)
        return carry

    lax.fori_loop(0, qi, body, 0)
    block(pl.ds(pl.multiple_of(qi * tq, tq), tq), True)

    for h in range(heads):
        lanes = slice(h * HEAD_DIM, (h + 1) * HEAD_DIM)
        l = jnp.sum(l_ref[h], axis=-1, keepdims=True)
        o = acc_ref[h] / l
        ms = jnp.mean(o * o, axis=-1, keepdims=True)
        y = o * lax.rsqrt(ms + EPS) * onw_ref[:, lanes]
        o_ref[:, lanes] = (y * sg_ref[:, lanes].astype(F32)).astype(o_ref.dtype)


def _fox_mix(q, kt, fk_rows, v, fcol, sg, out_norm_w, *, batch, seq):
    n_rows, d = q.shape
    n_heads = d // HEAD_DIM
    heads = min(FOX_HEADS, n_heads)
    tq = min(FOX_TQ, seq)
    assert seq % tq == 0 and tq % FOX_ROWS == 0 and tq % HEAD_DIM == 0 and n_heads % heads == 0
    nq = seq // tq
    n_pairs = n_heads // heads
    qblk = pl.BlockSpec((tq, heads * HEAD_DIM), lambda b, h, i: (b * nq + i, h))
    return pl.pallas_call(
        functools.partial(_fox_kernel, heads=heads),
        grid=(batch, n_pairs, nq),
        in_specs=[qblk,
                  pl.BlockSpec((heads, HEAD_DIM, seq), lambda b, h, i: (b * n_pairs + h, 0, 0)),
                  pl.BlockSpec((heads, fk_rows.shape[1], seq), lambda b, h, i: (b * n_pairs + h, 0, 0)),
                  pl.BlockSpec((seq, heads * HEAD_DIM), lambda b, h, i: (b, h)),
                  pl.BlockSpec((tq, HEAD_DIM), lambda b, h, i: (b * nq + i, 0)),
                  qblk,
                  pl.BlockSpec((1, heads * HEAD_DIM), lambda b, h, i: (0, h))],
        out_specs=qblk,
        out_shape=jax.ShapeDtypeStruct((n_rows, d), BF16),
        scratch_shapes=[pltpu.VMEM((heads, 2 * HEAD_DIM, seq), BF16),
                        pltpu.VMEM((heads, tq, 2 * HEAD_DIM), BF16),
                        pltpu.VMEM((heads, tq, tq), F32),
                        pltpu.VMEM((heads, tq, tq), BF16),
                        pltpu.VMEM((heads, tq, HEAD_DIM), F32),
                        pltpu.VMEM((heads, tq, HEAD_DIM), F32),
                        pltpu.VMEM((heads, tq, HEAD_DIM), F32),
                        pltpu.VMEM((heads, tq, HEAD_DIM), F32)],
        compiler_params=pltpu.CompilerParams(
            dimension_semantics=("parallel", "parallel", "arbitrary"),
            vmem_limit_bytes=VMEM_LIMIT_BYTES),
        name="fox_attention",
    )(q, kt, fk_rows, v, fcol, sg, out_norm_w.reshape(1, d))


def _fox_bias_rows(fk3, *, batch, seq, n_heads):
    terms = fk3.reshape(batch, seq, 3, HEAD_DIM)[:, :, :, :n_heads].transpose(0, 3, 2, 1)
    rows = jnp.concatenate([jnp.ones_like(terms), terms], axis=2)
    rows = jnp.pad(rows, ((0, 0), (0, 0), (0, BF16_SUBLANES - 6), (0, 0)))
    return rows.reshape(batch * n_heads, BF16_SUBLANES, seq)


def kernel(x, a_norm_w, a_w_in, a_lb_logits, a_out_norm_w, a_w_out, kv_norm_w, kv_w, kv_f_bias, k_norm_w,
           b_norm_w, b_w_in, b_q_norm_w, b_out_norm_w, b_w_out):
    batch, seq, d = x.shape
    n_heads = d // HEAD_DIM
    n_a = a_w_in.shape[0]
    n_b = b_w_in.shape[0]
    xr = x.reshape(batch * seq, d)

    a_w_in_b = a_w_in.astype(BF16)
    a_w_out_b = a_w_out.astype(BF16)
    b_w_in_b = b_w_in.astype(BF16)
    b_w_out_b = b_w_out.astype(BF16)
    kv_w_b = kv_w[:, :2 * d].astype(BF16)[None]
    wf_b = jnp.pad(kv_w[:, 2 * d:], ((0, 0), (0, HEAD_DIM - n_heads))).astype(BF16)
    f_bias = jnp.pad(kv_f_bias.astype(F32), (0, HEAD_DIM - n_heads)).reshape(1, HEAD_DIM)

    lb_all = jnp.cumsum(jax.nn.softmax(a_lb_logits.astype(F32), axis=0), axis=0)
    lb_all = lb_all - lb_all[0:1]
    ones_row = jnp.ones((1, d), F32)

    h = _norm(xr, a_norm_w[0])
    for layer in range(n_a):
        q, w2, k, v, sg = _proj(
            h,
            [(a_w_in_b, layer, 0, ones_row, "cast", (BF16,)),
             (a_w_in_b, layer, d, lb_all[layer].reshape(1, d), "hgate", (F32, BF16)),
             (a_w_in_b, layer, 2 * d, ones_row, "cast", (BF16,)),
             (a_w_in_b, layer, 3 * d, ones_row, "silu", (BF16,))],
            name="hgrn_in_proj", seq=seq, tm=1024)
        og = _hgrn_mix(q, w2, k, v, sg, a_out_norm_w[layer], batch=batch, seq=seq)
        if layer + 1 < n_a:
            xr, (h,) = _out_proj(og, a_w_out_b, layer, xr, [a_norm_w[layer + 1]])
        else:
            xr, (h_kv, h) = _out_proj(og, a_w_out_b, layer, xr, [kv_norm_w, b_norm_w[0]])

    k_norm_row = jnp.tile(k_norm_w.astype(F32), n_heads).reshape(1, d)
    kt, vv = _proj(
        h_kv,
        [(kv_w_b, 0, 0, k_norm_row, "headnorm_t", (BF16,)),
         (kv_w_b, 0, d, ones_row, "cast", (BF16,))],
        name="fox_kv_proj", seq=seq)
    fcol, fk3 = _fgate(h_kv, wf_b, f_bias, batch=batch, seq=seq)
    fk_rows = _fox_bias_rows(fk3, batch=batch, seq=seq, n_heads=n_heads)

    for j in range(n_b):
        q_norm_row = jnp.tile(b_q_norm_w[j].astype(F32), n_heads).reshape(1, d)
        q, sg = _proj(
            h,
            [(b_w_in_b, j, 0, q_norm_row, "headnorm", (BF16,)),
             (b_w_in_b, j, d, ones_row, "silu", (BF16,))],
            scale=HEAD_DIM ** -0.5 * LOG2E, name="fox_in_proj", seq=seq)
        og = _fox_mix(q, kt, fk_rows, vv, fcol, sg, b_out_norm_w[j], batch=batch, seq=seq)
        if j + 1 < n_b:
            xr, (h,) = _out_proj(og, b_w_out_b, j, xr, [b_norm_w[j + 1]])
        else:
            xr, _ = _out_proj(og, b_w_out_b, j, xr, [])

    return xr.reshape(batch, seq, d)
```

```python
import functools

import jax
import jax.numpy as jnp
from jax import lax
from jax.experimental import pallas as pl
from jax.experimental.pallas import tpu as pltpu

HEAD_DIM = 128
SUB_BLOCK = 16
HALF_SUB = SUB_BLOCK // 2
BF16_SUBLANES = 16
EPS = 1e-6
VMEM_LIMIT_BYTES = 56 * 1024 * 1024
NEG_BIG = -1e30
LOG2E = 1.4426950408889634
FOX_TQ = 512
FOX_ROWS = 64
FOX_HEADS = 2
FOX_SKIP_BITS = 152.0

F32 = jnp.float32
BF16 = jnp.bfloat16


def _dot(a, b):
    return jnp.dot(a, b, preferred_element_type=F32)


def _dot_nt(a, b):
    return lax.dot_general(a, b, (((1,), (1,)), ((), ())), preferred_element_type=F32)


def _dot_tn(a, b):
    return lax.dot_general(a, b, (((0,), (0,)), ((), ())), preferred_element_type=F32)


def _split3(x):
    hi = x.astype(BF16)
    r1 = x - hi.astype(F32)
    mid = r1.astype(BF16)
    lo = (r1 - mid.astype(F32)).astype(BF16)
    return hi, mid, lo


def _scan_sub_block(x):
    n, c = x.shape
    x3 = x.reshape(n // HALF_SUB, HALF_SUB, c)
    pos = lax.broadcasted_iota(jnp.int32, x3.shape, 1)
    shift = 1
    while shift < HALF_SUB:
        x3 = x3 + jnp.where(pos >= shift, pltpu.roll(x3, shift, axis=1), 0.0)
        shift *= 2
    tile = lax.broadcasted_iota(jnp.int32, x3.shape, 0)
    carry = jnp.roll(jnp.broadcast_to(x3[:, HALF_SUB - 1:, :], x3.shape), 1, axis=0)
    x3 = x3 + jnp.where(tile % 2 == 1, carry, 0.0)
    return x3.reshape(n, c)


def _rms_rows(x, w):
    ms = jnp.mean(x * x, axis=-1, keepdims=True)
    return x * lax.rsqrt(ms + EPS) * w


def _proj_kernel(*refs, kinds, scale):
    n = len(kinds)
    h_ref = refs[0]
    w_refs = refs[1:1 + n]
    aux_refs = refs[1 + n:1 + 2 * n]
    out_refs = refs[1 + 2 * n:]

    h = h_ref[...]
    oi = 0
    for s, kind in enumerate(kinds):
        acc = _dot(h, w_refs[s][...])
        aux = aux_refs[s][...]
        if kind == "cast":
            out_refs[oi][...] = acc.astype(out_refs[oi].dtype)
            oi += 1
        elif kind == "silu":
            out_refs[oi][...] = (acc / (1.0 + jnp.exp(-acc))).astype(out_refs[oi].dtype)
            oi += 1
        elif kind in ("headnorm", "headnorm_t"):
            tn = acc.shape[1]
            for c in range(tn // HEAD_DIM):
                sl = slice(c * HEAD_DIM, (c + 1) * HEAD_DIM)
                a = acc[:, sl]
                ms = jnp.mean(a * a, axis=-1, keepdims=True)
                y = a * lax.rsqrt(ms + EPS) * aux[:, sl]
                if scale != 1.0:
                    y = y * scale
                if kind == "headnorm_t":
                    out_refs[oi][c] = y.T.astype(out_refs[oi].dtype)
                else:
                    out_refs[oi][:, sl] = y.astype(out_refs[oi].dtype)
            oi += 1
        elif kind == "hgate":
            lb = aux
            e = jnp.exp2(jnp.abs(acc) * (-LOG2E))
            one_e = 1.0 + e
            log2_sig = jnp.minimum(acc, 0.0) * LOG2E - jnp.log2(one_e)
            a = jnp.log2(lb)
            c = jnp.log2(1.0 - lb) + log2_sig
            g2 = jnp.maximum(a, c) + jnp.log2(1.0 + jnp.exp2(-jnp.abs(a - c)))
            r = 1.0 / one_e
            sig_neg = jnp.where(acc >= 0.0, e * r, r)
            out_refs[oi][...] = _scan_sub_block(g2).astype(out_refs[oi].dtype)
            out_refs[oi + 1][...] = ((1.0 - lb) * sig_neg).astype(out_refs[oi + 1].dtype)
            oi += 2
        else:
            raise ValueError(kind)


def _proj(h, streams, *, name, seq, scale=1.0, tm=2048, tn=256):
    n_rows, d = h.shape
    n_cols = streams[0][3].shape[1]
    tm = min(tm, seq)
    tn = min(tn, n_cols)
    assert seq % tm == 0 and n_rows % seq == 0 and n_cols % tn == 0 and tm % SUB_BLOCK == 0
    kinds = tuple(s[4] for s in streams)
    t_tiles = seq // tm
    heads_per_tile = tn // HEAD_DIM
    n_head_tiles = n_cols // tn

    in_specs = [pl.BlockSpec((tm, d), lambda i, j: (i, 0))]
    args = [h]
    for (w, layer, off, aux, kind, _) in streams:
        assert off % tn == 0
        in_specs.append(pl.BlockSpec((None, d, tn),
                                     functools.partial(lambda i, j, l, o: (l, 0, j + o), l=layer, o=off // tn)))
        args.append(w)
    for (w, layer, off, aux, kind, _) in streams:
        in_specs.append(pl.BlockSpec((1, tn), lambda i, j: (0, j)))
        args.append(aux)
    out_shapes, out_specs = [], []
    for (w, layer, off, aux, kind, dts) in streams:
        for dt in dts:
            if kind == "headnorm_t":
                out_shapes.append(jax.ShapeDtypeStruct((n_rows // seq * n_cols // HEAD_DIM, HEAD_DIM, seq), dt))
                out_specs.append(pl.BlockSpec(
                    (heads_per_tile, HEAD_DIM, tm),
                    lambda i, j: ((i // t_tiles) * n_head_tiles + j, 0, i % t_tiles)))
            else:
                out_shapes.append(jax.ShapeDtypeStruct((n_rows, n_cols), dt))
                out_specs.append(pl.BlockSpec((tm, tn), lambda i, j: (i, j)))

    return pl.pallas_call(
        functools.partial(_proj_kernel, kinds=kinds, scale=scale),
        grid=(n_rows // tm, n_cols // tn),
        in_specs=in_specs,
        out_specs=out_specs,
        out_shape=out_shapes,
        compiler_params=pltpu.CompilerParams(
            dimension_semantics=("parallel", "parallel"),
            vmem_limit_bytes=VMEM_LIMIT_BYTES),
        name=name,
    )(*args)


def _norm_kernel(x_ref, nw_ref, h_ref):
    h_ref[...] = _rms_rows(x_ref[...], nw_ref[...]).astype(h_ref.dtype)


def _norm(x, norm_w, *, tm=1024):
    n_rows, d = x.shape
    tm = min(tm, n_rows)
    assert n_rows % tm == 0
    return pl.pallas_call(
        _norm_kernel,
        grid=(n_rows // tm,),
        in_specs=[pl.BlockSpec((tm, d), lambda i: (i, 0)),
                  pl.BlockSpec((1, d), lambda i: (0, 0))],
        out_specs=pl.BlockSpec((tm, d), lambda i: (i, 0)),
        out_shape=jax.ShapeDtypeStruct((n_rows, d), BF16),
        compiler_params=pltpu.CompilerParams(
            dimension_semantics=("parallel",),
            vmem_limit_bytes=VMEM_LIMIT_BYTES),
        name="input_norm",
    )(x, norm_w.reshape(1, d))


def _out_proj_kernel(o_ref, w_ref, x_ref, *refs):
    n_next = (len(refs) - 1) // 2
    nw_refs, y_ref, h_refs = refs[:n_next], refs[n_next], refs[n_next + 1:]
    y = x_ref[...] + _dot(o_ref[...], w_ref[...])
    y_ref[...] = y
    if n_next:
        yn = y * lax.rsqrt(jnp.mean(y * y, axis=-1, keepdims=True) + EPS)
        for nw_ref, h_ref in zip(nw_refs, h_refs):
            h_ref[...] = (yn * nw_ref[...]).astype(h_ref.dtype)


def _out_proj(o, w3d, layer, x, next_norm_ws, *, tm=512):
    n_rows, d = x.shape
    tm = min(tm, n_rows)
    assert n_rows % tm == 0
    rows = pl.BlockSpec((tm, d), lambda i: (i, 0))
    gain = pl.BlockSpec((1, d), lambda i: (0, 0))
    n_next = len(next_norm_ws)
    outs = pl.pallas_call(
        _out_proj_kernel,
        grid=(n_rows // tm,),
        in_specs=[rows, pl.BlockSpec((None, d, d), functools.partial(lambda i, l: (l, 0, 0), l=layer)), rows]
                 + [gain] * n_next,
        out_specs=[rows] * (1 + n_next),
        out_shape=[jax.ShapeDtypeStruct((n_rows, d), F32)]
                  + [jax.ShapeDtypeStruct((n_rows, d), BF16)] * n_next,
        compiler_params=pltpu.CompilerParams(
            dimension_semantics=("parallel",),
            vmem_limit_bytes=VMEM_LIMIT_BYTES),
        name="out_proj",
    )(o, w3d, x, *[w.reshape(1, d) for w in next_norm_ws])
    return outs[0], list(outs[1:])


def _excl_prefix_rows(x):
    pos = lax.broadcasted_iota(jnp.int32, x.shape, 0)
    inc = x
    shift = 1
    while shift < x.shape[0]:
        inc = inc + jnp.where(pos >= shift, pltpu.roll(inc, shift, axis=0), 0.0)
        shift *= 2
    return inc - x


def _rows_of(x, idx):
    return jnp.concatenate([x[i:i + 1] for i in idx], axis=0)


def _hgrn_chunk(q, k, w2, v, st, fac_ref, ck_ref, consts):
    lane8, sub_row, lvl_masks, same_blk, expand_r, jrow = consts
    c = q.shape[0]
    n_sub = c // SUB_BLOCK
    n_lvl = n_sub.bit_length() - 1

    gt2 = _rows_of(w2, [j * SUB_BLOCK + SUB_BLOCK - 1 for j in range(n_sub)])
    bs2 = _excl_prefix_rows(gt2)
    be2 = bs2 + gt2
    tot2 = be2[n_sub - 1:n_sub]
    fac_ref[0] = gt2
    fac_ref[1] = jnp.exp2(bs2)
    fac_ref[2] = jnp.exp2(tot2 - be2)
    for lvl in range(n_lvl):
        mid = _rows_of(bs2, [((i >> (lvl + 1)) << (lvl + 1)) + (1 << lvl) for i in range(n_sub)])
        upper = ((jrow >> lvl) & 1) == 1
        fac_ref[3 + 2 * lvl] = jnp.exp2(jnp.where(upper, bs2 - mid, -jnp.inf))
        fac_ref[4 + 2 * lvl] = jnp.exp2(jnp.where(upper, -jnp.inf, mid - be2))
    ck_ref[...] = jnp.log2(k) - w2

    qs_parts, kbar_parts, a_parts = [], [], []
    ql_parts = [[] for _ in range(n_lvl)]
    kl_parts = [[] for _ in range(n_lvl)]
    for j in range(n_sub):
        r0 = j * SUB_BLOCK
        sl = slice(r0, r0 + SUB_BLOCK)
        w_j, q_j = w2[sl], q[sl]
        qt = q_j * jnp.exp2(w_j)
        kh = k[sl] * jnp.exp2(fac_ref[0, j:j + 1, :] - w_j)
        qs_parts.append(qt * fac_ref[1, j:j + 1, :])
        kbar_parts.append(kh * fac_ref[2, j:j + 1, :])
        for lvl in range(n_lvl):
            ql_parts[lvl].append(qt * fac_ref[3 + 2 * lvl, j:j + 1, :])
            kl_parts[lvl].append(kh * fac_ref[4 + 2 * lvl, j:j + 1, :])

        w_lo, w_hi = w_j[:HALF_SUB], w_j[HALF_SUB:]
        q_lo, q_hi = q_j[:HALF_SUB], q_j[HALF_SUB:]
        a_lo = jnp.zeros((HALF_SUB, HEAD_DIM), F32)
        a_hi = jnp.zeros((HALF_SUB, HEAD_DIM), F32)
        for s in range(SUB_BLOCK):
            cs = ck_ref[r0 + s:r0 + s + 1, :]
            if s < HALF_SUB:
                col = jnp.sum(q_lo * jnp.exp2(w_lo + cs), axis=-1, keepdims=True)
                a_lo = jnp.where(lane8 == s, col, a_lo)
            col = jnp.sum(q_hi * jnp.exp2(w_hi + cs), axis=-1, keepdims=True)
            a_hi = jnp.where(lane8 == s, col, a_hi)
        a_parts.append(jnp.where(lane8 <= sub_row, a_lo, 0.0))
        a_parts.append(jnp.where(lane8 <= sub_row + HALF_SUB, a_hi, 0.0))

    cat = lambda parts: jnp.concatenate(parts, axis=0)
    o = _dot_nt(cat(qs_parts).astype(BF16), st.astype(BF16))
    a = _dot_nt(cat(ql_parts[n_lvl - 1]).astype(BF16), cat(kl_parts[n_lvl - 1]).astype(BF16))
    for lvl in range(n_lvl - 2, -1, -1):
        a_l = _dot_nt(cat(ql_parts[lvl]).astype(BF16), cat(kl_parts[lvl]).astype(BF16))
        a = jnp.where(lvl_masks[lvl], a_l, a)
    a_diag = _dot(cat(a_parts).astype(BF16), expand_r)
    a = jnp.where(same_blk, a_diag, a)
    o = o + _dot(a.astype(BF16), v)
    st_new = st * jnp.exp2(tot2) + _dot_tn(v, cat(kbar_parts).astype(BF16))
    return o, st_new


def _hgrn_kernel(q_ref, w2_ref, k_ref, v_ref, sg_ref, onw_ref, o_ref, st_ref, fac_ref, ck_ref, *, chunk, heads):
    tb = q_ref.shape[0]
    n_chunks = tb // chunk
    n_sub = chunk // SUB_BLOCK
    n_lvl = n_sub.bit_length() - 1

    @pl.when(pl.program_id(2) == 0)
    def _():
        st_ref[...] = jnp.zeros_like(st_ref)

    r = lax.broadcasted_iota(jnp.int32, (chunk, chunk), 0) // SUB_BLOCK
    cidx = lax.broadcasted_iota(jnp.int32, (chunk, chunk), 1)
    lane_blk = cidx // SUB_BLOCK
    same_blk = r == lane_blk
    lvl_masks = [(r >> (lvl + 1)) == (lane_blk >> (lvl + 1)) for lvl in range(n_lvl)]
    lane8 = lax.broadcasted_iota(jnp.int32, (HALF_SUB, HEAD_DIM), 1)
    sub_row = lax.broadcasted_iota(jnp.int32, (HALF_SUB, HEAD_DIM), 0)
    er = lax.broadcasted_iota(jnp.int32, (HEAD_DIM, chunk), 0)
    ec = lax.broadcasted_iota(jnp.int32, (HEAD_DIM, chunk), 1)
    expand_r = (ec % SUB_BLOCK == er).astype(BF16)
    jrow = lax.broadcasted_iota(jnp.int32, (n_sub, HEAD_DIM), 0)
    consts = (lane8, sub_row, lvl_masks, same_blk, expand_r, jrow)

    def body(ci, carry):
        rows = pl.ds(pl.multiple_of(ci * chunk, chunk), chunk)
        for h in range(heads):
            lanes = slice(h * HEAD_DIM, (h + 1) * HEAD_DIM)
            q = q_ref[rows, lanes].astype(F32)
            k = k_ref[rows, lanes].astype(F32)
            o, st_new = _hgrn_chunk(q, k, w2_ref[rows, lanes], v_ref[rows, lanes], st_ref[h],
                                    fac_ref.at[h], ck_ref.at[h], consts)
            st_ref[h] = st_new
            ms = jnp.mean(o * o, axis=-1, keepdims=True)
            y = o * lax.rsqrt(ms + EPS) * onw_ref[:, lanes]
            o_ref[rows, lanes] = (y * sg_ref[rows, lanes].astype(F32)).astype(o_ref.dtype)
        return carry

    lax.fori_loop(0, n_chunks, body, 0)


def _hgrn_mix(q, w2, k, v, sg, out_norm_w, *, batch, seq, chunk=128, heads=2, tb=512):
    n_rows, d = q.shape
    n_heads = d // HEAD_DIM
    heads = min(heads, n_heads)
    tb = min(tb, seq)
    chunk = min(chunk, tb)
    n_sub = chunk // SUB_BLOCK
    assert seq % tb == 0 and tb % chunk == 0 and n_heads % heads == 0
    assert chunk % SUB_BLOCK == 0 and n_sub & (n_sub - 1) == 0
    nt = seq // tb
    n_fac = 3 + 2 * (n_sub.bit_length() - 1)
    blk = pl.BlockSpec((tb, heads * HEAD_DIM), lambda b, h, t: (b * nt + t, h))
    return pl.pallas_call(
        functools.partial(_hgrn_kernel, chunk=chunk, heads=heads),
        grid=(batch, n_heads // heads, nt),
        in_specs=[blk, blk, blk, blk, blk,
                  pl.BlockSpec((1, heads * HEAD_DIM), lambda b, h, t: (0, h))],
        out_specs=blk,
        out_shape=jax.ShapeDtypeStruct((n_rows, d), BF16),
        scratch_shapes=[pltpu.VMEM((heads, HEAD_DIM, HEAD_DIM), F32),
                        pltpu.VMEM((heads, n_fac, n_sub, HEAD_DIM), F32),
                        pltpu.VMEM((heads, chunk, HEAD_DIM), F32)],
        compiler_params=pltpu.CompilerParams(
            dimension_semantics=("parallel", "parallel", "arbitrary"),
            vmem_limit_bytes=VMEM_LIMIT_BYTES),
        name="hgrn_mix",
    )(q, w2, k, v, sg, out_norm_w.reshape(1, d))


def _fgate_kernel(h_ref, wf_ref, bias_ref, f_ref, fk_ref, carry_ref):
    @pl.when(pl.program_id(1) == 0)
    def _():
        carry_ref[...] = jnp.zeros_like(carry_ref)

    z = _dot(h_ref[...], wf_ref[...]) + bias_ref[...]
    ls = jnp.minimum(z, 0.0) - jnp.log1p(jnp.exp(-jnp.abs(z)))
    tt = z.shape[0]
    r = lax.broadcasted_iota(jnp.int32, (tt, tt), 0)
    c = lax.broadcasted_iota(jnp.int32, (tt, tt), 1)
    tri = (c <= r).astype(BF16)
    hi, mid, lo = _split3(ls)
    cum = _dot(tri, hi) + _dot(tri, mid) + _dot(tri, lo)
    f = cum + carry_ref[...]
    f_ref[...] = f
    carry_ref[...] = f[tt - 1:tt]
    for i, term in enumerate(_split3(f * (-LOG2E))):
        fk_ref[:, i * HEAD_DIM:(i + 1) * HEAD_DIM] = term


def _fgate(h, wf, bias, *, batch, seq, tt=512):
    n_rows, d = h.shape
    tt = min(tt, seq)
    nt = seq // tt
    return pl.pallas_call(
        _fgate_kernel,
        grid=(batch, nt),
        in_specs=[pl.BlockSpec((tt, d), lambda b, t: (b * nt + t, 0)),
                  pl.BlockSpec((d, HEAD_DIM), lambda b, t: (0, 0)),
                  pl.BlockSpec((1, HEAD_DIM), lambda b, t: (0, 0))],
        out_specs=[pl.BlockSpec((tt, HEAD_DIM), lambda b, t: (b * nt + t, 0)),
                   pl.BlockSpec((tt, 3 * HEAD_DIM), lambda b, t: (b * nt + t, 0))],
        out_shape=[jax.ShapeDtypeStruct((n_rows, HEAD_DIM), F32),
                   jax.ShapeDtypeStruct((n_rows, 3 * HEAD_DIM), BF16)],
        scratch_shapes=[pltpu.VMEM((1, HEAD_DIM), F32)],
        compiler_params=pltpu.CompilerParams(
            dimension_semantics=("parallel", "arbitrary"),
            vmem_limit_bytes=VMEM_LIMIT_BYTES),
        name="fox_forget_gate",
    )(h, wf, bias)


def _fox_scores(qa_ref, kt_ref, s_ref, slot, h, cols):
    s_ref[slot] = _dot(qa_ref[h], kt_ref[h, :, cols])


def _fox_softmax(s_ref, p_ref, m_ref, l_ref, alpha_ref, slot, h, diag):
    tq, tk = s_ref.shape[1], s_ref.shape[2]
    for r in range(tq // FOX_ROWS):
        row0 = r * FOX_ROWS
        rows = pl.ds(row0, FOX_ROWS)
        n_chunks = (row0 + FOX_ROWS - 1) // HEAD_DIM + 1 if diag else tk // HEAD_DIM
        chunks = [s_ref[slot, rows, c * HEAD_DIM:(c + 1) * HEAD_DIM] for c in range(n_chunks)]
        if diag:
            last = n_chunks - 1
            rr = lax.broadcasted_iota(jnp.int32, (FOX_ROWS, HEAD_DIM), 0) + row0
            cc = lax.broadcasted_iota(jnp.int32, (FOX_ROWS, HEAD_DIM), 1) + last * HEAD_DIM
            chunks[last] = jnp.where(rr >= cc, chunks[last], -jnp.inf)
        mx = chunks[0]
        for ch in chunks[1:]:
            mx = jnp.maximum(mx, ch)
        m_prev = m_ref[h, rows, :]
        m_new = jnp.maximum(m_prev, jnp.max(mx, axis=-1, keepdims=True))
        alpha = jnp.exp2(m_prev - m_new)
        psum = None
        for c, ch in enumerate(chunks):
            part = jnp.exp2(ch - m_new)
            psum = part if psum is None else psum + part
            p_ref[h, rows, c * HEAD_DIM:(c + 1) * HEAD_DIM] = part.astype(BF16)
        if n_chunks * HEAD_DIM < tk:
            p_ref[h, rows, n_chunks * HEAD_DIM:] = jnp.zeros((FOX_ROWS, tk - n_chunks * HEAD_DIM), BF16)
        l_ref[h, rows, :] = alpha * l_ref[h, rows, :] + psum
        alpha_ref[h, rows, :] = alpha
        m_ref[h, rows, :] = m_new


def _fox_values(p_ref, v_ref, alpha_ref, acc_ref, h, cols):
    lanes = slice(h * HEAD_DIM, (h + 1) * HEAD_DIM)
    acc_ref[h] = alpha_ref[h] * acc_ref[h] + _dot(p_ref[h], v_ref[cols, lanes])


def _fox_kernel(q_ref, ktr_ref, fk_ref, v_ref, fcol_ref, fstart_ref, fend_ref, qkb_ref, sg_ref, onw_ref, o_ref,
                kt_ref, qa_ref, s_ref, p_ref, m_ref, l_ref, alpha_ref, acc_ref, *, heads):
    hp = pl.program_id(1)
    qi = pl.program_id(2)
    tq = q_ref.shape[0]

    @pl.when(qi == 0)
    def _():
        n_bias = fk_ref.shape[1]
        kt_ref[:, :HEAD_DIM, :] = ktr_ref[...]
        kt_ref[:, HEAD_DIM:HEAD_DIM + n_bias, :] = fk_ref[...]
        kt_ref[:, HEAD_DIM + n_bias:, :] = jnp.zeros(
            (heads, HEAD_DIM - n_bias, kt_ref.shape[2]), BF16)

    lane = lax.broadcasted_iota(jnp.int32, (tq, HEAD_DIM), 1)
    fcol = fcol_ref[...]
    for h in range(heads):
        fq = jnp.sum(jnp.where(lane == hp * heads + h, fcol, 0.0), axis=-1, keepdims=True) * LOG2E
        hi, mid, lo = _split3(fq)
        faug = jnp.where(lane == 0, hi.astype(F32),
                         jnp.where(lane == 1, mid.astype(F32),
                                   jnp.where(lane == 2, lo.astype(F32),
                                             jnp.where(lane < 6, 1.0, 0.0))))
        qa_ref[h, :, :HEAD_DIM] = q_ref[:, h * HEAD_DIM:(h + 1) * HEAD_DIM]
        qa_ref[h, :, HEAD_DIM:] = faug.astype(BF16)
    m_ref[...] = jnp.full(m_ref.shape, NEG_BIG, F32)
    l_ref[...] = jnp.zeros(l_ref.shape, F32)
    acc_ref[...] = jnp.zeros(acc_ref.shape, F32)

    def block(kb, diag):
        cols = pl.ds(pl.multiple_of(kb * tq, tq), tq)
        for h in range(heads):
            _fox_scores(qa_ref, kt_ref, s_ref, h, h, cols)
        for h in range(heads):
            _fox_softmax(s_ref, p_ref, m_ref, l_ref, alpha_ref, h, h, diag)
        for h in range(heads):
            _fox_values(p_ref, v_ref, alpha_ref, acc_ref, h, cols)

    block(qi, True)
    lane8 = lax.broadcasted_iota(jnp.int32, fend_ref.shape, 1)
    kb8 = lax.broadcasted_iota(jnp.int32, fend_ref.shape, 0)
    gap = 2.0 * qkb_ref[...] + LOG2E * (fstart_ref[pl.ds(qi, 1), :] - fend_ref[...])
    mine = (lane8 >= hp * heads) & (lane8 < (hp + 1) * heads) & (kb8 < qi)
    live = jnp.where(mine & (gap >= -FOX_SKIP_BITS), 1.0, 0.0)
    n_live = jnp.sum(jnp.max(live, axis=1, keepdims=True)).astype(jnp.int32)

    def body(j, carry):
        block(qi - 1 - j, False)
        return carry

    lax.fori_loop(0, n_live, body, 0)

    for h in range(heads):
        lanes = slice(h * HEAD_DIM, (h + 1) * HEAD_DIM)
        l = jnp.sum(l_ref[h], axis=-1, keepdims=True)
        o = acc_ref[h] / l
        ms = jnp.mean(o * o, axis=-1, keepdims=True)
        y = o * lax.rsqrt(ms + EPS) * onw_ref[:, lanes]
        o_ref[:, lanes] = (y * sg_ref[:, lanes].astype(F32)).astype(o_ref.dtype)


def _fox_mix(q, kt, fk_rows, v, fcol, qk_bound, sg, out_norm_w, *, batch, seq):
    n_rows, d = q.shape
    n_heads = d // HEAD_DIM
    heads = min(FOX_HEADS, n_heads)
    tq = min(FOX_TQ, seq)
    assert seq % tq == 0 and tq % FOX_ROWS == 0 and tq % HEAD_DIM == 0 and n_heads % heads == 0
    nq = seq // tq
    n_pairs = n_heads // heads
    qblk = pl.BlockSpec((tq, heads * HEAD_DIM), lambda b, h, i: (b * nq + i, h))
    fblocks = fcol.reshape(batch, nq, tq, HEAD_DIM)
    fstart, fend = fblocks[:, :, 0, :], fblocks[:, :, tq - 1, :]
    fedge = pl.BlockSpec((None, nq, HEAD_DIM), lambda b, h, i: (b, 0, 0))
    qkb_row = jnp.full((1, HEAD_DIM), qk_bound, F32)
    return pl.pallas_call(
        functools.partial(_fox_kernel, heads=heads),
        grid=(batch, n_pairs, nq),
        in_specs=[qblk,
                  pl.BlockSpec((heads, HEAD_DIM, seq), lambda b, h, i: (b * n_pairs + h, 0, 0)),
                  pl.BlockSpec((heads, fk_rows.shape[1], seq), lambda b, h, i: (b * n_pairs + h, 0, 0)),
                  pl.BlockSpec((seq, heads * HEAD_DIM), lambda b, h, i: (b, h)),
                  pl.BlockSpec((tq, HEAD_DIM), lambda b, h, i: (b * nq + i, 0)),
                  fedge, fedge,
                  pl.BlockSpec((1, HEAD_DIM), lambda b, h, i: (0, 0)),
                  qblk,
                  pl.BlockSpec((1, heads * HEAD_DIM), lambda b, h, i: (0, h))],
        out_specs=qblk,
        out_shape=jax.ShapeDtypeStruct((n_rows, d), BF16),
        scratch_shapes=[pltpu.VMEM((heads, 2 * HEAD_DIM, seq), BF16),
                        pltpu.VMEM((heads, tq, 2 * HEAD_DIM), BF16),
                        pltpu.VMEM((heads, tq, tq), F32),
                        pltpu.VMEM((heads, tq, tq), BF16),
                        pltpu.VMEM((heads, tq, HEAD_DIM), F32),
                        pltpu.VMEM((heads, tq, HEAD_DIM), F32),
                        pltpu.VMEM((heads, tq, HEAD_DIM), F32),
                        pltpu.VMEM((heads, tq, HEAD_DIM), F32)],
        compiler_params=pltpu.CompilerParams(
            dimension_semantics=("parallel", "parallel", "arbitrary"),
            vmem_limit_bytes=VMEM_LIMIT_BYTES),
        name="fox_attention",
    )(q, kt, fk_rows, v, fcol, fstart, fend, qkb_row, sg, out_norm_w.reshape(1, d))


def _fox_bias_rows(fk3, *, batch, seq, n_heads):
    terms = fk3.reshape(batch, seq, 3, HEAD_DIM)[:, :, :, :n_heads].transpose(0, 3, 2, 1)
    rows = jnp.concatenate([jnp.ones_like(terms), terms], axis=2)
    rows = jnp.pad(rows, ((0, 0), (0, 0), (0, BF16_SUBLANES - 6), (0, 0)))
    return rows.reshape(batch * n_heads, BF16_SUBLANES, seq)


def kernel(x, a_norm_w, a_w_in, a_lb_logits, a_out_norm_w, a_w_out, kv_norm_w, kv_w, kv_f_bias, k_norm_w,
           b_norm_w, b_w_in, b_q_norm_w, b_out_norm_w, b_w_out):
    batch, seq, d = x.shape
    n_heads = d // HEAD_DIM
    n_a = a_w_in.shape[0]
    n_b = b_w_in.shape[0]
    xr = x.reshape(batch * seq, d)

    a_w_in_b = a_w_in.astype(BF16)
    a_w_out_b = a_w_out.astype(BF16)
    b_w_in_b = b_w_in.astype(BF16)
    b_w_out_b = b_w_out.astype(BF16)
    kv_w_b = kv_w[:, :2 * d].astype(BF16)[None]
    wf_b = jnp.pad(kv_w[:, 2 * d:], ((0, 0), (0, HEAD_DIM - n_heads))).astype(BF16)
    f_bias = jnp.pad(kv_f_bias.astype(F32), (0, HEAD_DIM - n_heads)).reshape(1, HEAD_DIM)

    lb_all = jnp.cumsum(jax.nn.softmax(a_lb_logits.astype(F32), axis=0), axis=0)
    lb_all = lb_all - lb_all[0:1]
    ones_row = jnp.ones((1, d), F32)

    h = _norm(xr, a_norm_w[0])
    for layer in range(n_a):
        q, w2, k, v, sg = _proj(
            h,
            [(a_w_in_b, layer, 0, ones_row, "cast", (BF16,)),
             (a_w_in_b, layer, d, lb_all[layer].reshape(1, d), "hgate", (F32, BF16)),
             (a_w_in_b, layer, 2 * d, ones_row, "cast", (BF16,)),
             (a_w_in_b, layer, 3 * d, ones_row, "silu", (BF16,))],
            name="hgrn_in_proj", seq=seq, tm=1024)
        og = _hgrn_mix(q, w2, k, v, sg, a_out_norm_w[layer], batch=batch, seq=seq)
        if layer + 1 < n_a:
            xr, (h,) = _out_proj(og, a_w_out_b, layer, xr, [a_norm_w[layer + 1]])
        else:
            xr, (h_kv, h) = _out_proj(og, a_w_out_b, layer, xr, [kv_norm_w, b_norm_w[0]])

    k_norm_row = jnp.tile(k_norm_w.astype(F32), n_heads).reshape(1, d)
    kt, vv = _proj(
        h_kv,
        [(kv_w_b, 0, 0, k_norm_row, "headnorm_t", (BF16,)),
         (kv_w_b, 0, d, ones_row, "cast", (BF16,))],
        name="fox_kv_proj", seq=seq)
    fcol, fk3 = _fgate(h_kv, wf_b, f_bias, batch=batch, seq=seq)
    fk_rows = _fox_bias_rows(fk3, batch=batch, seq=seq, n_heads=n_heads)

    for j in range(n_b):
        q_norm_row = jnp.tile(b_q_norm_w[j].astype(F32), n_heads).reshape(1, d)
        q, sg = _proj(
            h,
            [(b_w_in_b, j, 0, q_norm_row, "headnorm", (BF16,)),
             (b_w_in_b, j, d, ones_row, "silu", (BF16,))],
            scale=HEAD_DIM ** -0.5 * LOG2E, name="fox_in_proj", seq=seq)
        qk_bound = (HEAD_DIM * jnp.max(jnp.abs(b_q_norm_w[j])) * jnp.max(jnp.abs(k_norm_w))
                    * (HEAD_DIM ** -0.5 * LOG2E * 1.02))
        og = _fox_mix(q, kt, fk_rows, vv, fcol, qk_bound, sg, b_out_norm_w[j], batch=batch, seq=seq)
        if j + 1 < n_b:
            xr, (h,) = _out_proj(og, b_w_out_b, j, xr, [b_norm_w[j + 1]])
        else:
            xr, _ = _out_proj(og, b_w_out_b, j, xr, [])

    return xr.reshape(batch, seq, d)
```

```python
import functools

import jax
import jax.numpy as jnp
from jax import lax
from jax.experimental import pallas as pl
from jax.experimental.pallas import tpu as pltpu

HEAD_DIM = 128
VREG_ROWS = 8
SUB_BLOCK = 8
BF16_SUBLANES = 16
EPS = 1e-6
VMEM_LIMIT_BYTES = 56 * 1024 * 1024
NEG_BIG = -1e30
LOG2E = 1.4426950408889634
FOX_TQ = 512
FOX_ROWS = 64
FOX_HEADS = 2
FOX_SKIP_BITS = 152.0

F32 = jnp.float32
BF16 = jnp.bfloat16


def _dot(a, b):
    return jnp.dot(a, b, preferred_element_type=F32)


def _dot_nt(a, b):
    return lax.dot_general(a, b, (((1,), (1,)), ((), ())), preferred_element_type=F32)


def _dot_tn(a, b):
    return lax.dot_general(a, b, (((0,), (0,)), ((), ())), preferred_element_type=F32)


def _split3(x):
    hi = x.astype(BF16)
    r1 = x - hi.astype(F32)
    mid = r1.astype(BF16)
    lo = (r1 - mid.astype(F32)).astype(BF16)
    return hi, mid, lo


def _scan_sub_block(x):
    n, c = x.shape
    x3 = x.reshape(n // VREG_ROWS, VREG_ROWS, c)
    pos = lax.broadcasted_iota(jnp.int32, x3.shape, 1)
    shift = 1
    while shift < VREG_ROWS:
        x3 = x3 + jnp.where(pos >= shift, pltpu.roll(x3, shift, axis=1), 0.0)
        shift *= 2
    if SUB_BLOCK == 2 * VREG_ROWS:
        tile = lax.broadcasted_iota(jnp.int32, x3.shape, 0)
        carry = jnp.roll(jnp.broadcast_to(x3[:, VREG_ROWS - 1:, :], x3.shape), 1, axis=0)
        x3 = x3 + jnp.where(tile % 2 == 1, carry, 0.0)
    return x3.reshape(n, c)


def _rms_rows(x, w):
    ms = jnp.mean(x * x, axis=-1, keepdims=True)
    return x * lax.rsqrt(ms + EPS) * w


def _proj_kernel(*refs, kinds, scale):
    n = len(kinds)
    h_ref = refs[0]
    w_refs = refs[1:1 + n]
    aux_refs = refs[1 + n:1 + 2 * n]
    out_refs = refs[1 + 2 * n:]

    h = h_ref[...]
    oi = 0
    for s, kind in enumerate(kinds):
        acc = _dot(h, w_refs[s][...])
        aux = aux_refs[s][...]
        if kind == "cast":
            out_refs[oi][...] = acc.astype(out_refs[oi].dtype)
            oi += 1
        elif kind == "silu":
            out_refs[oi][...] = (acc / (1.0 + jnp.exp(-acc))).astype(out_refs[oi].dtype)
            oi += 1
        elif kind in ("headnorm", "headnorm_t"):
            tn = acc.shape[1]
            for c in range(tn // HEAD_DIM):
                sl = slice(c * HEAD_DIM, (c + 1) * HEAD_DIM)
                a = acc[:, sl]
                ms = jnp.mean(a * a, axis=-1, keepdims=True)
                y = a * lax.rsqrt(ms + EPS) * aux[:, sl]
                if scale != 1.0:
                    y = y * scale
                if kind == "headnorm_t":
                    out_refs[oi][c] = y.T.astype(out_refs[oi].dtype)
                else:
                    out_refs[oi][:, sl] = y.astype(out_refs[oi].dtype)
            oi += 1
        elif kind == "hgate":
            lb = aux
            e = jnp.exp2(jnp.abs(acc) * (-LOG2E))
            one_e = 1.0 + e
            log2_sig = jnp.minimum(acc, 0.0) * LOG2E - jnp.log2(one_e)
            a = jnp.log2(lb)
            c = jnp.log2(1.0 - lb) + log2_sig
            g2 = jnp.maximum(a, c) + jnp.log2(1.0 + jnp.exp2(-jnp.abs(a - c)))
            r = 1.0 / one_e
            sig_neg = jnp.where(acc >= 0.0, e * r, r)
            out_refs[oi][...] = _scan_sub_block(g2).astype(out_refs[oi].dtype)
            out_refs[oi + 1][...] = ((1.0 - lb) * sig_neg).astype(out_refs[oi + 1].dtype)
            oi += 2
        else:
            raise ValueError(kind)


def _proj(h, streams, *, name, seq, scale=1.0, tm=2048, tn=256):
    n_rows, d = h.shape
    n_cols = streams[0][3].shape[1]
    tm = min(tm, seq)
    tn = min(tn, n_cols)
    assert seq % tm == 0 and n_rows % seq == 0 and n_cols % tn == 0 and tm % SUB_BLOCK == 0
    kinds = tuple(s[4] for s in streams)
    t_tiles = seq // tm
    heads_per_tile = tn // HEAD_DIM
    n_head_tiles = n_cols // tn

    in_specs = [pl.BlockSpec((tm, d), lambda i, j: (i, 0))]
    args = [h]
    for (w, layer, off, aux, kind, _) in streams:
        assert off % tn == 0
        in_specs.append(pl.BlockSpec((None, d, tn),
                                     functools.partial(lambda i, j, l, o: (l, 0, j + o), l=layer, o=off // tn)))
        args.append(w)
    for (w, layer, off, aux, kind, _) in streams:
        in_specs.append(pl.BlockSpec((1, tn), lambda i, j: (0, j)))
        args.append(aux)
    out_shapes, out_specs = [], []
    for (w, layer, off, aux, kind, dts) in streams:
        for dt in dts:
            if kind == "headnorm_t":
                out_shapes.append(jax.ShapeDtypeStruct((n_rows // seq * n_cols // HEAD_DIM, HEAD_DIM, seq), dt))
                out_specs.append(pl.BlockSpec(
                    (heads_per_tile, HEAD_DIM, tm),
                    lambda i, j: ((i // t_tiles) * n_head_tiles + j, 0, i % t_tiles)))
            else:
                out_shapes.append(jax.ShapeDtypeStruct((n_rows, n_cols), dt))
                out_specs.append(pl.BlockSpec((tm, tn), lambda i, j: (i, j)))

    return pl.pallas_call(
        functools.partial(_proj_kernel, kinds=kinds, scale=scale),
        grid=(n_rows // tm, n_cols // tn),
        in_specs=in_specs,
        out_specs=out_specs,
        out_shape=out_shapes,
        compiler_params=pltpu.CompilerParams(
            dimension_semantics=("parallel", "parallel"),
            vmem_limit_bytes=VMEM_LIMIT_BYTES),
        name=name,
    )(*args)


def _norm_kernel(x_ref, nw_ref, h_ref):
    h_ref[...] = _rms_rows(x_ref[...], nw_ref[...]).astype(h_ref.dtype)


def _norm(x, norm_w, *, tm=1024):
    n_rows, d = x.shape
    tm = min(tm, n_rows)
    assert n_rows % tm == 0
    return pl.pallas_call(
        _norm_kernel,
        grid=(n_rows // tm,),
        in_specs=[pl.BlockSpec((tm, d), lambda i: (i, 0)),
                  pl.BlockSpec((1, d), lambda i: (0, 0))],
        out_specs=pl.BlockSpec((tm, d), lambda i: (i, 0)),
        out_shape=jax.ShapeDtypeStruct((n_rows, d), BF16),
        compiler_params=pltpu.CompilerParams(
            dimension_semantics=("parallel",),
            vmem_limit_bytes=VMEM_LIMIT_BYTES),
        name="input_norm",
    )(x, norm_w.reshape(1, d))


def _out_proj_kernel(o_ref, w_ref, x_ref, *refs):
    n_next = (len(refs) - 1) // 2
    nw_refs, y_ref, h_refs = refs[:n_next], refs[n_next], refs[n_next + 1:]
    y = x_ref[...] + _dot(o_ref[...], w_ref[...])
    y_ref[...] = y
    if n_next:
        yn = y * lax.rsqrt(jnp.mean(y * y, axis=-1, keepdims=True) + EPS)
        for nw_ref, h_ref in zip(nw_refs, h_refs):
            h_ref[...] = (yn * nw_ref[...]).astype(h_ref.dtype)


def _out_proj(o, w3d, layer, x, next_norm_ws, *, tm=512):
    n_rows, d = x.shape
    tm = min(tm, n_rows)
    assert n_rows % tm == 0
    rows = pl.BlockSpec((tm, d), lambda i: (i, 0))
    gain = pl.BlockSpec((1, d), lambda i: (0, 0))
    n_next = len(next_norm_ws)
    outs = pl.pallas_call(
        _out_proj_kernel,
        grid=(n_rows // tm,),
        in_specs=[rows, pl.BlockSpec((None, d, d), functools.partial(lambda i, l: (l, 0, 0), l=layer)), rows]
                 + [gain] * n_next,
        out_specs=[rows] * (1 + n_next),
        out_shape=[jax.ShapeDtypeStruct((n_rows, d), F32)]
                  + [jax.ShapeDtypeStruct((n_rows, d), BF16)] * n_next,
        compiler_params=pltpu.CompilerParams(
            dimension_semantics=("parallel",),
            vmem_limit_bytes=VMEM_LIMIT_BYTES),
        name="out_proj",
    )(o, w3d, x, *[w.reshape(1, d) for w in next_norm_ws])
    return outs[0], list(outs[1:])


def _excl_prefix_rows(x):
    pos = lax.broadcasted_iota(jnp.int32, x.shape, 0)
    inc = x
    shift = 1
    while shift < x.shape[0]:
        inc = inc + jnp.where(pos >= shift, pltpu.roll(inc, shift, axis=0), 0.0)
        shift *= 2
    return inc - x


def _rows_of(x, idx):
    return jnp.concatenate([x[i:i + 1] for i in idx], axis=0)


def _hgrn_chunk(q, k, w2, v, st, fac_ref, ck_ref, consts):
    lane8, sub_row, lvl_masks, same_blk, expand_r, jrow = consts
    c = q.shape[0]
    n_sub = c // SUB_BLOCK
    n_lvl = n_sub.bit_length() - 1

    gt2 = _rows_of(w2, [j * SUB_BLOCK + SUB_BLOCK - 1 for j in range(n_sub)])
    bs2 = _excl_prefix_rows(gt2)
    be2 = bs2 + gt2
    tot2 = be2[n_sub - 1:n_sub]
    fac_ref[0] = gt2
    fac_ref[1] = jnp.exp2(bs2)
    fac_ref[2] = jnp.exp2(tot2 - be2)
    for lvl in range(n_lvl):
        mid = _rows_of(bs2, [((i >> (lvl + 1)) << (lvl + 1)) + (1 << lvl) for i in range(n_sub)])
        upper = ((jrow >> lvl) & 1) == 1
        fac_ref[3 + 2 * lvl] = jnp.exp2(jnp.where(upper, bs2 - mid, -jnp.inf))
        fac_ref[4 + 2 * lvl] = jnp.exp2(jnp.where(upper, -jnp.inf, mid - be2))
    ck_ref[...] = jnp.log2(k) - w2

    qs_parts, kbar_parts, a_parts = [], [], []
    ql_parts = [[] for _ in range(n_lvl)]
    kl_parts = [[] for _ in range(n_lvl)]
    for j in range(n_sub):
        r0 = j * SUB_BLOCK
        sl = slice(r0, r0 + SUB_BLOCK)
        w_j, q_j = w2[sl], q[sl]
        qt = q_j * jnp.exp2(w_j)
        kh = k[sl] * jnp.exp2(fac_ref[0, j:j + 1, :] - w_j)
        qs_parts.append(qt * fac_ref[1, j:j + 1, :])
        kbar_parts.append(kh * fac_ref[2, j:j + 1, :])
        for lvl in range(n_lvl):
            ql_parts[lvl].append(qt * fac_ref[3 + 2 * lvl, j:j + 1, :])
            kl_parts[lvl].append(kh * fac_ref[4 + 2 * lvl, j:j + 1, :])

        for t0 in range(0, SUB_BLOCK, VREG_ROWS):
            w_t, q_t = w_j[t0:t0 + VREG_ROWS], q_j[t0:t0 + VREG_ROWS]
            a_t = jnp.zeros((VREG_ROWS, HEAD_DIM), F32)
            for s in range(t0 + VREG_ROWS):
                cs = ck_ref[r0 + s:r0 + s + 1, :]
                col = jnp.sum(q_t * jnp.exp2(w_t + cs), axis=-1, keepdims=True)
                a_t = jnp.where(lane8 == s, col, a_t)
            a_parts.append(jnp.where(lane8 <= sub_row + t0, a_t, 0.0))

    cat = lambda parts: jnp.concatenate(parts, axis=0)
    o = _dot_nt(cat(qs_parts).astype(BF16), st.astype(BF16))
    a = _dot_nt(cat(ql_parts[n_lvl - 1]).astype(BF16), cat(kl_parts[n_lvl - 1]).astype(BF16))
    for lvl in range(n_lvl - 2, -1, -1):
        a_l = _dot_nt(cat(ql_parts[lvl]).astype(BF16), cat(kl_parts[lvl]).astype(BF16))
        a = jnp.where(lvl_masks[lvl], a_l, a)
    a_diag = _dot(cat(a_parts).astype(BF16), expand_r)
    a = jnp.where(same_blk, a_diag, a)
    o = o + _dot(a.astype(BF16), v)
    st_new = st * jnp.exp2(tot2) + _dot_tn(v, cat(kbar_parts).astype(BF16))
    return o, st_new


def _hgrn_kernel(q_ref, w2_ref, k_ref, v_ref, sg_ref, onw_ref, o_ref, st_ref, fac_ref, ck_ref, *, chunk, heads):
    tb = q_ref.shape[0]
    n_chunks = tb // chunk
    n_sub = chunk // SUB_BLOCK
    n_lvl = n_sub.bit_length() - 1

    @pl.when(pl.program_id(2) == 0)
    def _():
        st_ref[...] = jnp.zeros_like(st_ref)

    r = lax.broadcasted_iota(jnp.int32, (chunk, chunk), 0) // SUB_BLOCK
    cidx = lax.broadcasted_iota(jnp.int32, (chunk, chunk), 1)
    lane_blk = cidx // SUB_BLOCK
    same_blk = r == lane_blk
    lvl_masks = [(r >> (lvl + 1)) == (lane_blk >> (lvl + 1)) for lvl in range(n_lvl)]
    lane8 = lax.broadcasted_iota(jnp.int32, (VREG_ROWS, HEAD_DIM), 1)
    sub_row = lax.broadcasted_iota(jnp.int32, (VREG_ROWS, HEAD_DIM), 0)
    er = lax.broadcasted_iota(jnp.int32, (HEAD_DIM, chunk), 0)
    ec = lax.broadcasted_iota(jnp.int32, (HEAD_DIM, chunk), 1)
    expand_r = (ec % SUB_BLOCK == er).astype(BF16)
    jrow = lax.broadcasted_iota(jnp.int32, (n_sub, HEAD_DIM), 0)
    consts = (lane8, sub_row, lvl_masks, same_blk, expand_r, jrow)

    def body(ci, carry):
        rows = pl.ds(pl.multiple_of(ci * chunk, chunk), chunk)
        for h in range(heads):
            lanes = slice(h * HEAD_DIM, (h + 1) * HEAD_DIM)
            q = q_ref[rows, lanes].astype(F32)
            k = k_ref[rows, lanes].astype(F32)
            o, st_new = _hgrn_chunk(q, k, w2_ref[rows, lanes], v_ref[rows, lanes], st_ref[h],
                                    fac_ref.at[h], ck_ref.at[h], consts)
            st_ref[h] = st_new
            ms = jnp.mean(o * o, axis=-1, keepdims=True)
            y = o * lax.rsqrt(ms + EPS) * onw_ref[:, lanes]
            o_ref[rows, lanes] = (y * sg_ref[rows, lanes].astype(F32)).astype(o_ref.dtype)
        return carry

    lax.fori_loop(0, n_chunks, body, 0)


def _hgrn_mix(q, w2, k, v, sg, out_norm_w, *, batch, seq, chunk=128, heads=4, tb=512):
    n_rows, d = q.shape
    n_heads = d // HEAD_DIM
    heads = min(heads, n_heads)
    tb = min(tb, seq)
    chunk = min(chunk, tb)
    n_sub = chunk // SUB_BLOCK
    assert seq % tb == 0 and tb % chunk == 0 and n_heads % heads == 0
    assert chunk % SUB_BLOCK == 0 and n_sub & (n_sub - 1) == 0
    nt = seq // tb
    n_fac = 3 + 2 * (n_sub.bit_length() - 1)
    blk = pl.BlockSpec((tb, heads * HEAD_DIM), lambda b, h, t: (b * nt + t, h))
    return pl.pallas_call(
        functools.partial(_hgrn_kernel, chunk=chunk, heads=heads),
        grid=(batch, n_heads // heads, nt),
        in_specs=[blk, blk, blk, blk, blk,
                  pl.BlockSpec((1, heads * HEAD_DIM), lambda b, h, t: (0, h))],
        out_specs=blk,
        out_shape=jax.ShapeDtypeStruct((n_rows, d), BF16),
        scratch_shapes=[pltpu.VMEM((heads, HEAD_DIM, HEAD_DIM), F32),
                        pltpu.VMEM((heads, n_fac, n_sub, HEAD_DIM), F32),
                        pltpu.VMEM((heads, chunk, HEAD_DIM), F32)],
        compiler_params=pltpu.CompilerParams(
            dimension_semantics=("parallel", "parallel", "arbitrary"),
            vmem_limit_bytes=VMEM_LIMIT_BYTES),
        name="hgrn_mix",
    )(q, w2, k, v, sg, out_norm_w.reshape(1, d))


def _fgate_kernel(h_ref, wf_ref, bias_ref, f_ref, fk_ref, carry_ref):
    @pl.when(pl.program_id(1) == 0)
    def _():
        carry_ref[...] = jnp.zeros_like(carry_ref)

    z = _dot(h_ref[...], wf_ref[...]) + bias_ref[...]
    ls = jnp.minimum(z, 0.0) - jnp.log1p(jnp.exp(-jnp.abs(z)))
    tt = z.shape[0]
    r = lax.broadcasted_iota(jnp.int32, (tt, tt), 0)
    c = lax.broadcasted_iota(jnp.int32, (tt, tt), 1)
    tri = (c <= r).astype(BF16)
    hi, mid, lo = _split3(ls)
    cum = _dot(tri, hi) + _dot(tri, mid) + _dot(tri, lo)
    f = cum + carry_ref[...]
    f_ref[...] = f
    carry_ref[...] = f[tt - 1:tt]
    for i, term in enumerate(_split3(f * (-LOG2E))):
        fk_ref[:, i * HEAD_DIM:(i + 1) * HEAD_DIM] = term


def _fgate(h, wf, bias, *, batch, seq, tt=512):
    n_rows, d = h.shape
    tt = min(tt, seq)
    nt = seq // tt
    return pl.pallas_call(
        _fgate_kernel,
        grid=(batch, nt),
        in_specs=[pl.BlockSpec((tt, d), lambda b, t: (b * nt + t, 0)),
                  pl.BlockSpec((d, HEAD_DIM), lambda b, t: (0, 0)),
                  pl.BlockSpec((1, HEAD_DIM), lambda b, t: (0, 0))],
        out_specs=[pl.BlockSpec((tt, HEAD_DIM), lambda b, t: (b * nt + t, 0)),
                   pl.BlockSpec((tt, 3 * HEAD_DIM), lambda b, t: (b * nt + t, 0))],
        out_shape=[jax.ShapeDtypeStruct((n_rows, HEAD_DIM), F32),
                   jax.ShapeDtypeStruct((n_rows, 3 * HEAD_DIM), BF16)],
        scratch_shapes=[pltpu.VMEM((1, HEAD_DIM), F32)],
        compiler_params=pltpu.CompilerParams(
            dimension_semantics=("parallel", "arbitrary"),
            vmem_limit_bytes=VMEM_LIMIT_BYTES),
        name="fox_forget_gate",
    )(h, wf, bias)


def _fox_scores(qa_ref, kt_ref, s_ref, slot, h, cols):
    s_ref[slot] = _dot(qa_ref[h], kt_ref[h, :, cols])


def _fox_softmax(s_ref, p_ref, m_ref, l_ref, alpha_ref, slot, h, diag):
    tq, tk = s_ref.shape[1], s_ref.shape[2]
    for r in range(tq // FOX_ROWS):
        row0 = r * FOX_ROWS
        rows = pl.ds(row0, FOX_ROWS)
        n_chunks = (row0 + FOX_ROWS - 1) // HEAD_DIM + 1 if diag else tk // HEAD_DIM
        chunks = [s_ref[slot, rows, c * HEAD_DIM:(c + 1) * HEAD_DIM] for c in range(n_chunks)]
        if diag:
            last = n_chunks - 1
            rr = lax.broadcasted_iota(jnp.int32, (FOX_ROWS, HEAD_DIM), 0) + row0
            cc = lax.broadcasted_iota(jnp.int32, (FOX_ROWS, HEAD_DIM), 1) + last * HEAD_DIM
            chunks[last] = jnp.where(rr >= cc, chunks[last], -jnp.inf)
        mx = chunks[0]
        for ch in chunks[1:]:
            mx = jnp.maximum(mx, ch)
        m_prev = m_ref[h, rows, :]
        m_new = jnp.maximum(m_prev, jnp.max(mx, axis=-1, keepdims=True))
        alpha = jnp.exp2(m_prev - m_new)
        psum = None
        for c, ch in enumerate(chunks):
            part = jnp.exp2(ch - m_new)
            psum = part if psum is None else psum + part
            p_ref[h, rows, c * HEAD_DIM:(c + 1) * HEAD_DIM] = part.astype(BF16)
        if n_chunks * HEAD_DIM < tk:
            p_ref[h, rows, n_chunks * HEAD_DIM:] = jnp.zeros((FOX_ROWS, tk - n_chunks * HEAD_DIM), BF16)
        l_ref[h, rows, :] = alpha * l_ref[h, rows, :] + psum
        alpha_ref[h, rows, :] = alpha
        m_ref[h, rows, :] = m_new


def _fox_values(p_ref, v_ref, alpha_ref, acc_ref, h, cols):
    lanes = slice(h * HEAD_DIM, (h + 1) * HEAD_DIM)
    acc_ref[h] = alpha_ref[h] * acc_ref[h] + _dot(p_ref[h], v_ref[cols, lanes])


def _fox_kernel(q_ref, ktr_ref, fk_ref, v_ref, fcol_ref, fstart_ref, fend_ref, qkb_ref, sg_ref, onw_ref, o_ref,
                kt_ref, qa_ref, s_ref, p_ref, m_ref, l_ref, alpha_ref, acc_ref, *, heads):
    hp = pl.program_id(1)
    qi = pl.program_id(2)
    tq = q_ref.shape[0]

    @pl.when(qi == 0)
    def _():
        n_bias = fk_ref.shape[1]
        kt_ref[:, :HEAD_DIM, :] = ktr_ref[...]
        kt_ref[:, HEAD_DIM:HEAD_DIM + n_bias, :] = fk_ref[...]
        kt_ref[:, HEAD_DIM + n_bias:, :] = jnp.zeros(
            (heads, HEAD_DIM - n_bias, kt_ref.shape[2]), BF16)

    lane = lax.broadcasted_iota(jnp.int32, (tq, HEAD_DIM), 1)
    fcol = fcol_ref[...]
    for h in range(heads):
        fq = jnp.sum(jnp.where(lane == hp * heads + h, fcol, 0.0), axis=-1, keepdims=True) * LOG2E
        hi, mid, lo = _split3(fq)
        faug = jnp.where(lane == 0, hi.astype(F32),
                         jnp.where(lane == 1, mid.astype(F32),
                                   jnp.where(lane == 2, lo.astype(F32),
                                             jnp.where(lane < 6, 1.0, 0.0))))
        qa_ref[h, :, :HEAD_DIM] = q_ref[:, h * HEAD_DIM:(h + 1) * HEAD_DIM]
        qa_ref[h, :, HEAD_DIM:] = faug.astype(BF16)
    m_ref[...] = jnp.full(m_ref.shape, NEG_BIG, F32)
    l_ref[...] = jnp.zeros(l_ref.shape, F32)
    acc_ref[...] = jnp.zeros(acc_ref.shape, F32)

    def block(kb, diag):
        cols = pl.ds(pl.multiple_of(kb * tq, tq), tq)
        for h in range(heads):
            _fox_scores(qa_ref, kt_ref, s_ref, h, h, cols)
        for h in range(heads):
            _fox_softmax(s_ref, p_ref, m_ref, l_ref, alpha_ref, h, h, diag)
        for h in range(heads):
            _fox_values(p_ref, v_ref, alpha_ref, acc_ref, h, cols)

    block(qi, True)
    lane8 = lax.broadcasted_iota(jnp.int32, fend_ref.shape, 1)
    kb8 = lax.broadcasted_iota(jnp.int32, fend_ref.shape, 0)
    gap = 2.0 * qkb_ref[...] + LOG2E * (fstart_ref[pl.ds(qi, 1), :] - fend_ref[...])
    mine = (lane8 >= hp * heads) & (lane8 < (hp + 1) * heads) & (kb8 < qi)
    live = jnp.where(mine & (gap >= -FOX_SKIP_BITS), 1.0, 0.0)
    n_live = jnp.sum(jnp.max(live, axis=1, keepdims=True)).astype(jnp.int32)

    def body(j, carry):
        block(qi - 1 - j, False)
        return carry

    lax.fori_loop(0, n_live, body, 0)

    for h in range(heads):
        lanes = slice(h * HEAD_DIM, (h + 1) * HEAD_DIM)
        l = jnp.sum(l_ref[h], axis=-1, keepdims=True)
        o = acc_ref[h] / l
        ms = jnp.mean(o * o, axis=-1, keepdims=True)
        y = o * lax.rsqrt(ms + EPS) * onw_ref[:, lanes]
        o_ref[:, lanes] = (y * sg_ref[:, lanes].astype(F32)).astype(o_ref.dtype)


def _fox_mix(q, kt, fk_rows, v, fcol, qk_bound, sg, out_norm_w, *, batch, seq):
    n_rows, d = q.shape
    n_heads = d // HEAD_DIM
    heads = min(FOX_HEADS, n_heads)
    tq = min(FOX_TQ, seq)
    assert seq % tq == 0 and tq % FOX_ROWS == 0 and tq % HEAD_DIM == 0 and n_heads % heads == 0
    nq = seq // tq
    n_pairs = n_heads // heads
    qblk = pl.BlockSpec((tq, heads * HEAD_DIM), lambda b, h, i: (b * nq + i, h))
    fblocks = fcol.reshape(batch, nq, tq, HEAD_DIM)
    fstart, fend = fblocks[:, :, 0, :], fblocks[:, :, tq - 1, :]
    fedge = pl.BlockSpec((None, nq, HEAD_DIM), lambda b, h, i: (b, 0, 0))
    qkb_row = jnp.full((1, HEAD_DIM), qk_bound, F32)
    return pl.pallas_call(
        functools.partial(_fox_kernel, heads=heads),
        grid=(batch, n_pairs, nq),
        in_specs=[qblk,
                  pl.BlockSpec((heads, HEAD_DIM, seq), lambda b, h, i: (b * n_pairs + h, 0, 0)),
                  pl.BlockSpec((heads, fk_rows.shape[1], seq), lambda b, h, i: (b * n_pairs + h, 0, 0)),
                  pl.BlockSpec((seq, heads * HEAD_DIM), lambda b, h, i: (b, h)),
                  pl.BlockSpec((tq, HEAD_DIM), lambda b, h, i: (b * nq + i, 0)),
                  fedge, fedge,
                  pl.BlockSpec((1, HEAD_DIM), lambda b, h, i: (0, 0)),
                  qblk,
                  pl.BlockSpec((1, heads * HEAD_DIM), lambda b, h, i: (0, h))],
        out_specs=qblk,
        out_shape=jax.ShapeDtypeStruct((n_rows, d), BF16),
        scratch_shapes=[pltpu.VMEM((heads, 2 * HEAD_DIM, seq), BF16),
                        pltpu.VMEM((heads, tq, 2 * HEAD_DIM), BF16),
                        pltpu.VMEM((heads, tq, tq), F32),
                        pltpu.VMEM((heads, tq, tq), BF16),
                        pltpu.VMEM((heads, tq, HEAD_DIM), F32),
                        pltpu.VMEM((heads, tq, HEAD_DIM), F32),
                        pltpu.VMEM((heads, tq, HEAD_DIM), F32),
                        pltpu.VMEM((heads, tq, HEAD_DIM), F32)],
        compiler_params=pltpu.CompilerParams(
            dimension_semantics=("parallel", "parallel", "arbitrary"),
            vmem_limit_bytes=VMEM_LIMIT_BYTES),
        name="fox_attention",
    )(q, kt, fk_rows, v, fcol, fstart, fend, qkb_row, sg, out_norm_w.reshape(1, d))


def _fox_bias_rows(fk3, *, batch, seq, n_heads):
    terms = fk3.reshape(batch, seq, 3, HEAD_DIM)[:, :, :, :n_heads].transpose(0, 3, 2, 1)
    rows = jnp.concatenate([jnp.ones_like(terms), terms], axis=2)
    rows = jnp.pad(rows, ((0, 0), (0, 0), (0, BF16_SUBLANES - 6), (0, 0)))
    return rows.reshape(batch * n_heads, BF16_SUBLANES, seq)


def kernel(x, a_norm_w, a_w_in, a_lb_logits, a_out_norm_w, a_w_out, kv_norm_w, kv_w, kv_f_bias, k_norm_w,
           b_norm_w, b_w_in, b_q_norm_w, b_out_norm_w, b_w_out):
    batch, seq, d = x.shape
    n_heads = d // HEAD_DIM
    n_a = a_w_in.shape[0]
    n_b = b_w_in.shape[0]
    xr = x.reshape(batch * seq, d)

    a_w_in_b = a_w_in.astype(BF16)
    a_w_out_b = a_w_out.astype(BF16)
    b_w_in_b = b_w_in.astype(BF16)
    b_w_out_b = b_w_out.astype(BF16)
    kv_w_b = kv_w[:, :2 * d].astype(BF16)[None]
    wf_b = jnp.pad(kv_w[:, 2 * d:], ((0, 0), (0, HEAD_DIM - n_heads))).astype(BF16)
    f_bias = jnp.pad(kv_f_bias.astype(F32), (0, HEAD_DIM - n_heads)).reshape(1, HEAD_DIM)

    lb_all = jnp.cumsum(jax.nn.softmax(a_lb_logits.astype(F32), axis=0), axis=0)
    lb_all = lb_all - lb_all[0:1]
    ones_row = jnp.ones((1, d), F32)

    h = _norm(xr, a_norm_w[0])
    for layer in range(n_a):
        q, w2, k, v, sg = _proj(
            h,
            [(a_w_in_b, layer, 0, ones_row, "cast", (BF16,)),
             (a_w_in_b, layer, d, lb_all[layer].reshape(1, d), "hgate", (F32, BF16)),
             (a_w_in_b, layer, 2 * d, ones_row, "cast", (BF16,)),
             (a_w_in_b, layer, 3 * d, ones_row, "silu", (BF16,))],
            name="hgrn_in_proj", seq=seq, tm=1024)
        og = _hgrn_mix(q, w2, k, v, sg, a_out_norm_w[layer], batch=batch, seq=seq)
        if layer + 1 < n_a:
            xr, (h,) = _out_proj(og, a_w_out_b, layer, xr, [a_norm_w[layer + 1]])
        else:
            xr, (h_kv, h) = _out_proj(og, a_w_out_b, layer, xr, [kv_norm_w, b_norm_w[0]])

    k_norm_row = jnp.tile(k_norm_w.astype(F32), n_heads).reshape(1, d)
    kt, vv = _proj(
        h_kv,
        [(kv_w_b, 0, 0, k_norm_row, "headnorm_t", (BF16,)),
         (kv_w_b, 0, d, ones_row, "cast", (BF16,))],
        name="fox_kv_proj", seq=seq)
    fcol, fk3 = _fgate(h_kv, wf_b, f_bias, batch=batch, seq=seq)
    fk_rows = _fox_bias_rows(fk3, batch=batch, seq=seq, n_heads=n_heads)

    for j in range(n_b):
        q_norm_row = jnp.tile(b_q_norm_w[j].astype(F32), n_heads).reshape(1, d)
        q, sg = _proj(
            h,
            [(b_w_in_b, j, 0, q_norm_row, "headnorm", (BF16,)),
             (b_w_in_b, j, d, ones_row, "silu", (BF16,))],
            scale=HEAD_DIM ** -0.5 * LOG2E, name="fox_in_proj", seq=seq)
        qk_bound = (HEAD_DIM * jnp.max(jnp.abs(b_q_norm_w[j])) * jnp.max(jnp.abs(k_norm_w))
                    * (HEAD_DIM ** -0.5 * LOG2E * 1.02))
        og = _fox_mix(q, kt, fk_rows, vv, fcol, qk_bound, sg, b_out_norm_w[j], batch=batch, seq=seq)
        if j + 1 < n_b:
            xr, (h,) = _out_proj(og, b_w_out_b, j, xr, [b_norm_w[j + 1]])
        else:
            xr, _ = _out_proj(og, b_w_out_b, j, xr, [])

    return xr.reshape(batch, seq, d)
```

```python
import functools

import jax
import jax.numpy as jnp
from jax import lax
from jax.experimental import pallas as pl
from jax.experimental.pallas import tpu as pltpu

HEAD_DIM = 128
VREG_ROWS = 8
SUB_BLOCK = 8
BF16_SUBLANES = 16
EPS = 1e-6
VMEM_LIMIT_BYTES = 56 * 1024 * 1024
NEG_BIG = -1e30
LOG2E = 1.4426950408889634
FOX_TQ = 512
FOX_ROWS = 64
FOX_HEADS = 4
FOX_SKIP_BITS = 152.0

F32 = jnp.float32
BF16 = jnp.bfloat16


def _dot(a, b):
    return jnp.dot(a, b, preferred_element_type=F32)


def _dot_nt(a, b):
    return lax.dot_general(a, b, (((1,), (1,)), ((), ())), preferred_element_type=F32)


def _dot_tn(a, b):
    return lax.dot_general(a, b, (((0,), (0,)), ((), ())), preferred_element_type=F32)


def _split3(x):
    hi = x.astype(BF16)
    r1 = x - hi.astype(F32)
    mid = r1.astype(BF16)
    lo = (r1 - mid.astype(F32)).astype(BF16)
    return hi, mid, lo


def _scan_sub_block(x):
    n, c = x.shape
    x3 = x.reshape(n // VREG_ROWS, VREG_ROWS, c)
    pos = lax.broadcasted_iota(jnp.int32, x3.shape, 1)
    shift = 1
    while shift < VREG_ROWS:
        x3 = x3 + jnp.where(pos >= shift, pltpu.roll(x3, shift, axis=1), 0.0)
        shift *= 2
    if SUB_BLOCK == 2 * VREG_ROWS:
        tile = lax.broadcasted_iota(jnp.int32, x3.shape, 0)
        carry = jnp.roll(jnp.broadcast_to(x3[:, VREG_ROWS - 1:, :], x3.shape), 1, axis=0)
        x3 = x3 + jnp.where(tile % 2 == 1, carry, 0.0)
    return x3.reshape(n, c)


def _rms_rows(x, w):
    ms = jnp.mean(x * x, axis=-1, keepdims=True)
    return x * lax.rsqrt(ms + EPS) * w


def _proj_kernel(*refs, kinds, scale):
    n = len(kinds)
    h_ref = refs[0]
    w_refs = refs[1:1 + n]
    aux_refs = refs[1 + n:1 + 2 * n]
    out_refs = refs[1 + 2 * n:]

    h = h_ref[...]
    oi = 0
    for s, kind in enumerate(kinds):
        acc = _dot(h, w_refs[s][...])
        aux = aux_refs[s][...]
        if kind == "cast":
            out_refs[oi][...] = acc.astype(out_refs[oi].dtype)
            oi += 1
        elif kind == "silu":
            out_refs[oi][...] = (acc / (1.0 + jnp.exp(-acc))).astype(out_refs[oi].dtype)
            oi += 1
        elif kind in ("headnorm", "headnorm_t"):
            tn = acc.shape[1]
            for c in range(tn // HEAD_DIM):
                sl = slice(c * HEAD_DIM, (c + 1) * HEAD_DIM)
                a = acc[:, sl]
                ms = jnp.mean(a * a, axis=-1, keepdims=True)
                y = a * lax.rsqrt(ms + EPS) * aux[:, sl]
                if scale != 1.0:
                    y = y * scale
                if kind == "headnorm_t":
                    out_refs[oi][c] = y.T.astype(out_refs[oi].dtype)
                else:
                    out_refs[oi][:, sl] = y.astype(out_refs[oi].dtype)
            oi += 1
        elif kind == "hgate":
            lb = aux
            e = jnp.exp2(jnp.abs(acc) * (-LOG2E))
            one_e = 1.0 + e
            log2_sig = jnp.minimum(acc, 0.0) * LOG2E - jnp.log2(one_e)
            a = jnp.log2(lb)
            c = jnp.log2(1.0 - lb) + log2_sig
            g2 = jnp.maximum(a, c) + jnp.log2(1.0 + jnp.exp2(-jnp.abs(a - c)))
            r = 1.0 / one_e
            sig_neg = jnp.where(acc >= 0.0, e * r, r)
            out_refs[oi][...] = _scan_sub_block(g2).astype(out_refs[oi].dtype)
            out_refs[oi + 1][...] = ((1.0 - lb) * sig_neg).astype(out_refs[oi + 1].dtype)
            oi += 2
        else:
            raise ValueError(kind)


def _proj(h, streams, *, name, seq, scale=1.0, tm=2048, tn=256):
    n_rows, d = h.shape
    n_cols = streams[0][3].shape[1]
    tm = min(tm, seq)
    tn = min(tn, n_cols)
    assert seq % tm == 0 and n_rows % seq == 0 and n_cols % tn == 0 and tm % SUB_BLOCK == 0
    kinds = tuple(s[4] for s in streams)
    t_tiles = seq // tm
    heads_per_tile = tn // HEAD_DIM
    n_head_tiles = n_cols // tn

    in_specs = [pl.BlockSpec((tm, d), lambda i, j: (i, 0))]
    args = [h]
    for (w, layer, off, aux, kind, _) in streams:
        assert off % tn == 0
        in_specs.append(pl.BlockSpec((None, d, tn),
                                     functools.partial(lambda i, j, l, o: (l, 0, j + o), l=layer, o=off // tn)))
        args.append(w)
    for (w, layer, off, aux, kind, _) in streams:
        in_specs.append(pl.BlockSpec((1, tn), lambda i, j: (0, j)))
        args.append(aux)
    out_shapes, out_specs = [], []
    for (w, layer, off, aux, kind, dts) in streams:
        for dt in dts:
            if kind == "headnorm_t":
                out_shapes.append(jax.ShapeDtypeStruct((n_rows // seq * n_cols // HEAD_DIM, HEAD_DIM, seq), dt))
                out_specs.append(pl.BlockSpec(
                    (heads_per_tile, HEAD_DIM, tm),
                    lambda i, j: ((i // t_tiles) * n_head_tiles + j, 0, i % t_tiles)))
            else:
                out_shapes.append(jax.ShapeDtypeStruct((n_rows, n_cols), dt))
                out_specs.append(pl.BlockSpec((tm, tn), lambda i, j: (i, j)))

    return pl.pallas_call(
        functools.partial(_proj_kernel, kinds=kinds, scale=scale),
        grid=(n_rows // tm, n_cols // tn),
        in_specs=in_specs,
        out_specs=out_specs,
        out_shape=out_shapes,
        compiler_params=pltpu.CompilerParams(
            dimension_semantics=("parallel", "parallel"),
            vmem_limit_bytes=VMEM_LIMIT_BYTES),
        name=name,
    )(*args)


def _norm_kernel(x_ref, nw_ref, h_ref):
    h_ref[...] = _rms_rows(x_ref[...], nw_ref[...]).astype(h_ref.dtype)


def _norm(x, norm_w, *, tm=1024):
    n_rows, d = x.shape
    tm = min(tm, n_rows)
    assert n_rows % tm == 0
    return pl.pallas_call(
        _norm_kernel,
        grid=(n_rows // tm,),
        in_specs=[pl.BlockSpec((tm, d), lambda i: (i, 0)),
                  pl.BlockSpec((1, d), lambda i: (0, 0))],
        out_specs=pl.BlockSpec((tm, d), lambda i: (i, 0)),
        out_shape=jax.ShapeDtypeStruct((n_rows, d), BF16),
        compiler_params=pltpu.CompilerParams(
            dimension_semantics=("parallel",),
            vmem_limit_bytes=VMEM_LIMIT_BYTES),
        name="input_norm",
    )(x, norm_w.reshape(1, d))


def _out_proj_kernel(o_ref, w_ref, x_ref, *refs):
    n_next = (len(refs) - 1) // 2
    nw_refs, y_ref, h_refs = refs[:n_next], refs[n_next], refs[n_next + 1:]
    y = x_ref[...] + _dot(o_ref[...], w_ref[...])
    y_ref[...] = y
    if n_next:
        yn = y * lax.rsqrt(jnp.mean(y * y, axis=-1, keepdims=True) + EPS)
        for nw_ref, h_ref in zip(nw_refs, h_refs):
            h_ref[...] = (yn * nw_ref[...]).astype(h_ref.dtype)


def _out_proj(o, w3d, layer, x, next_norm_ws, *, tm=512):
    n_rows, d = x.shape
    tm = min(tm, n_rows)
    assert n_rows % tm == 0
    rows = pl.BlockSpec((tm, d), lambda i: (i, 0))
    gain = pl.BlockSpec((1, d), lambda i: (0, 0))
    n_next = len(next_norm_ws)
    outs = pl.pallas_call(
        _out_proj_kernel,
        grid=(n_rows // tm,),
        in_specs=[rows, pl.BlockSpec((None, d, d), functools.partial(lambda i, l: (l, 0, 0), l=layer)), rows]
                 + [gain] * n_next,
        out_specs=[rows] * (1 + n_next),
        out_shape=[jax.ShapeDtypeStruct((n_rows, d), F32)]
                  + [jax.ShapeDtypeStruct((n_rows, d), BF16)] * n_next,
        compiler_params=pltpu.CompilerParams(
            dimension_semantics=("parallel",),
            vmem_limit_bytes=VMEM_LIMIT_BYTES),
        name="out_proj",
    )(o, w3d, x, *[w.reshape(1, d) for w in next_norm_ws])
    return outs[0], list(outs[1:])


def _excl_prefix_rows(x):
    pos = lax.broadcasted_iota(jnp.int32, x.shape, 0)
    inc = x
    shift = 1
    while shift < x.shape[0]:
        inc = inc + jnp.where(pos >= shift, pltpu.roll(inc, shift, axis=0), 0.0)
        shift *= 2
    return inc - x


def _level_mid(bs2, lvl):
    n = bs2.shape[0]
    group = 2 << lvl
    if group >= VREG_ROWS:
        tiles = [jnp.broadcast_to(bs2[(t0 // group) * group + group // 2:][:1], (VREG_ROWS, HEAD_DIM))
                 for t0 in range(0, n, VREG_ROWS)]
        return jnp.concatenate(tiles, axis=0)
    x3 = bs2.reshape(n // VREG_ROWS, VREG_ROWS, HEAD_DIM)
    pos = lax.broadcasted_iota(jnp.int32, x3.shape, 1)
    out = None
    for g0 in range(0, VREG_ROWS, group):
        piece = jnp.broadcast_to(x3[:, g0 + group // 2:g0 + group // 2 + 1, :], x3.shape)
        out = piece if out is None else jnp.where(pos >= g0, piece, out)
    return out.reshape(n, HEAD_DIM)


def _rows_of(x, idx):
    return jnp.concatenate([x[i:i + 1] for i in idx], axis=0)


def _hgrn_chunk(q, k, w2, v, st, fac_ref, ck_ref, consts):
    lane8, sub_row, lvl_masks, same_blk, expand_r, jrow = consts
    c = q.shape[0]
    n_sub = c // SUB_BLOCK
    n_lvl = n_sub.bit_length() - 1

    gt2 = _rows_of(w2, [j * SUB_BLOCK + SUB_BLOCK - 1 for j in range(n_sub)])
    bs2 = _excl_prefix_rows(gt2)
    be2 = bs2 + gt2
    tot2 = be2[n_sub - 1:n_sub]
    fac_ref[0] = gt2
    fac_ref[1] = jnp.exp2(bs2)
    fac_ref[2] = jnp.exp2(tot2 - be2)
    for lvl in range(n_lvl):
        mid = _level_mid(bs2, lvl)
        upper = ((jrow >> lvl) & 1) == 1
        fac_ref[3 + 2 * lvl] = jnp.exp2(jnp.where(upper, bs2 - mid, -jnp.inf))
        fac_ref[4 + 2 * lvl] = jnp.exp2(jnp.where(upper, -jnp.inf, mid - be2))
    ck_ref[...] = jnp.log2(k) - w2

    qs_parts, kbar_parts, a_parts = [], [], []
    ql_parts = [[] for _ in range(n_lvl)]
    kl_parts = [[] for _ in range(n_lvl)]
    for j in range(n_sub):
        r0 = j * SUB_BLOCK
        sl = slice(r0, r0 + SUB_BLOCK)
        w_j, q_j = w2[sl], q[sl]
        qt = q_j * jnp.exp2(w_j)
        kh = k[sl] * jnp.exp2(fac_ref[0, j:j + 1, :] - w_j)
        qs_parts.append(qt * fac_ref[1, j:j + 1, :])
        kbar_parts.append(kh * fac_ref[2, j:j + 1, :])
        for lvl in range(n_lvl):
            ql_parts[lvl].append(qt * fac_ref[3 + 2 * lvl, j:j + 1, :])
            kl_parts[lvl].append(kh * fac_ref[4 + 2 * lvl, j:j + 1, :])

        for t0 in range(0, SUB_BLOCK, VREG_ROWS):
            w_t, q_t = w_j[t0:t0 + VREG_ROWS], q_j[t0:t0 + VREG_ROWS]
            a_t = jnp.zeros((VREG_ROWS, HEAD_DIM), F32)
            for s in range(t0 + VREG_ROWS):
                cs = ck_ref[r0 + s:r0 + s + 1, :]
                col = jnp.sum(q_t * jnp.exp2(w_t + cs), axis=-1, keepdims=True)
                a_t = jnp.where(lane8 == s, col, a_t)
            a_parts.append(jnp.where(lane8 <= sub_row + t0, a_t, 0.0))

    cat = lambda parts: jnp.concatenate(parts, axis=0)
    o = _dot_nt(cat(qs_parts).astype(BF16), st.astype(BF16))
    a = _dot_nt(cat(ql_parts[n_lvl - 1]).astype(BF16), cat(kl_parts[n_lvl - 1]).astype(BF16))
    for lvl in range(n_lvl - 2, -1, -1):
        a_l = _dot_nt(cat(ql_parts[lvl]).astype(BF16), cat(kl_parts[lvl]).astype(BF16))
        a = jnp.where(lvl_masks[lvl], a_l, a)
    a_diag = _dot(cat(a_parts).astype(BF16), expand_r)
    a = jnp.where(same_blk, a_diag, a)
    o = o + _dot(a.astype(BF16), v)
    st_new = st * jnp.exp2(tot2) + _dot_tn(v, cat(kbar_parts).astype(BF16))
    return o, st_new


def _hgrn_kernel(q_ref, w2_ref, k_ref, v_ref, sg_ref, onw_ref, o_ref, st_ref, fac_ref, ck_ref, *, chunk, heads):
    tb = q_ref.shape[0]
    n_chunks = tb // chunk
    n_sub = chunk // SUB_BLOCK
    n_lvl = n_sub.bit_length() - 1

    @pl.when(pl.program_id(2) == 0)
    def _():
        st_ref[...] = jnp.zeros_like(st_ref)

    r = lax.broadcasted_iota(jnp.int32, (chunk, chunk), 0) // SUB_BLOCK
    cidx = lax.broadcasted_iota(jnp.int32, (chunk, chunk), 1)
    lane_blk = cidx // SUB_BLOCK
    same_blk = r == lane_blk
    lvl_masks = [(r >> (lvl + 1)) == (lane_blk >> (lvl + 1)) for lvl in range(n_lvl)]
    lane8 = lax.broadcasted_iota(jnp.int32, (VREG_ROWS, HEAD_DIM), 1)
    sub_row = lax.broadcasted_iota(jnp.int32, (VREG_ROWS, HEAD_DIM), 0)
    er = lax.broadcasted_iota(jnp.int32, (HEAD_DIM, chunk), 0)
    ec = lax.broadcasted_iota(jnp.int32, (HEAD_DIM, chunk), 1)
    expand_r = (ec % SUB_BLOCK == er).astype(BF16)
    jrow = lax.broadcasted_iota(jnp.int32, (n_sub, HEAD_DIM), 0)
    consts = (lane8, sub_row, lvl_masks, same_blk, expand_r, jrow)

    def body(ci, carry):
        rows = pl.ds(pl.multiple_of(ci * chunk, chunk), chunk)
        for h in range(heads):
            lanes = slice(h * HEAD_DIM, (h + 1) * HEAD_DIM)
            q = q_ref[rows, lanes].astype(F32)
            k = k_ref[rows, lanes].astype(F32)
            o, st_new = _hgrn_chunk(q, k, w2_ref[rows, lanes], v_ref[rows, lanes], st_ref[h],
                                    fac_ref.at[h], ck_ref.at[h], consts)
            st_ref[h] = st_new
            ms = jnp.mean(o * o, axis=-1, keepdims=True)
            y = o * lax.rsqrt(ms + EPS) * onw_ref[:, lanes]
            o_ref[rows, lanes] = (y * sg_ref[rows, lanes].astype(F32)).astype(o_ref.dtype)
        return carry

    lax.fori_loop(0, n_chunks, body, 0)


def _hgrn_mix(q, w2, k, v, sg, out_norm_w, *, batch, seq, chunk=128, heads=4, tb=512):
    n_rows, d = q.shape
    n_heads = d // HEAD_DIM
    heads = min(heads, n_heads)
    tb = min(tb, seq)
    chunk = min(chunk, tb)
    n_sub = chunk // SUB_BLOCK
    assert seq % tb == 0 and tb % chunk == 0 and n_heads % heads == 0
    assert chunk % SUB_BLOCK == 0 and n_sub & (n_sub - 1) == 0
    nt = seq // tb
    n_fac = 3 + 2 * (n_sub.bit_length() - 1)
    blk = pl.BlockSpec((tb, heads * HEAD_DIM), lambda b, h, t: (b * nt + t, h))
    return pl.pallas_call(
        functools.partial(_hgrn_kernel, chunk=chunk, heads=heads),
        grid=(batch, n_heads // heads, nt),
        in_specs=[blk, blk, blk, blk, blk,
                  pl.BlockSpec((1, heads * HEAD_DIM), lambda b, h, t: (0, h))],
        out_specs=blk,
        out_shape=jax.ShapeDtypeStruct((n_rows, d), BF16),
        scratch_shapes=[pltpu.VMEM((heads, HEAD_DIM, HEAD_DIM), F32),
                        pltpu.VMEM((heads, n_fac, n_sub, HEAD_DIM), F32),
                        pltpu.VMEM((heads, chunk, HEAD_DIM), F32)],
        compiler_params=pltpu.CompilerParams(
            dimension_semantics=("parallel", "parallel", "arbitrary"),
            vmem_limit_bytes=VMEM_LIMIT_BYTES),
        name="hgrn_mix",
    )(q, w2, k, v, sg, out_norm_w.reshape(1, d))


def _fgate_kernel(h_ref, wf_ref, bias_ref, f_ref, fk_ref, carry_ref):
    @pl.when(pl.program_id(1) == 0)
    def _():
        carry_ref[...] = jnp.zeros_like(carry_ref)

    z = _dot(h_ref[...], wf_ref[...]) + bias_ref[...]
    ls = jnp.minimum(z, 0.0) - jnp.log1p(jnp.exp(-jnp.abs(z)))
    tt = z.shape[0]
    r = lax.broadcasted_iota(jnp.int32, (tt, tt), 0)
    c = lax.broadcasted_iota(jnp.int32, (tt, tt), 1)
    tri = (c <= r).astype(BF16)
    hi, mid, lo = _split3(ls)
    cum = _dot(tri, hi) + _dot(tri, mid) + _dot(tri, lo)
    f = cum + carry_ref[...]
    f_ref[...] = f
    carry_ref[...] = f[tt - 1:tt]
    for i, term in enumerate(_split3(f * (-LOG2E))):
        fk_ref[:, i * HEAD_DIM:(i + 1) * HEAD_DIM] = term


def _fgate(h, wf, bias, *, batch, seq, tt=512):
    n_rows, d = h.shape
    tt = min(tt, seq)
    nt = seq // tt
    return pl.pallas_call(
        _fgate_kernel,
        grid=(batch, nt),
        in_specs=[pl.BlockSpec((tt, d), lambda b, t: (b * nt + t, 0)),
                  pl.BlockSpec((d, HEAD_DIM), lambda b, t: (0, 0)),
                  pl.BlockSpec((1, HEAD_DIM), lambda b, t: (0, 0))],
        out_specs=[pl.BlockSpec((tt, HEAD_DIM), lambda b, t: (b * nt + t, 0)),
                   pl.BlockSpec((tt, 3 * HEAD_DIM), lambda b, t: (b * nt + t, 0))],
        out_shape=[jax.ShapeDtypeStruct((n_rows, HEAD_DIM), F32),
                   jax.ShapeDtypeStruct((n_rows, 3 * HEAD_DIM), BF16)],
        scratch_shapes=[pltpu.VMEM((1, HEAD_DIM), F32)],
        compiler_params=pltpu.CompilerParams(
            dimension_semantics=("parallel", "arbitrary"),
            vmem_limit_bytes=VMEM_LIMIT_BYTES),
        name="fox_forget_gate",
    )(h, wf, bias)


def _fox_scores(qa_ref, kt_ref, s_ref, slot, h, cols):
    s_ref[slot] = _dot(qa_ref[h], kt_ref[h, :, cols])


def _fox_softmax(s_ref, p_ref, m_ref, l_ref, alpha_ref, slot, h, diag):
    tq, tk = s_ref.shape[1], s_ref.shape[2]
    for r in range(tq // FOX_ROWS):
        row0 = r * FOX_ROWS
        rows = pl.ds(row0, FOX_ROWS)
        n_chunks = (row0 + FOX_ROWS - 1) // HEAD_DIM + 1 if diag else tk // HEAD_DIM
        chunks = [s_ref[slot, rows, c * HEAD_DIM:(c + 1) * HEAD_DIM] for c in range(n_chunks)]
        if diag:
            last = n_chunks - 1
            rr = lax.broadcasted_iota(jnp.int32, (FOX_ROWS, HEAD_DIM), 0) + row0
            cc = lax.broadcasted_iota(jnp.int32, (FOX_ROWS, HEAD_DIM), 1) + last * HEAD_DIM
            chunks[last] = jnp.where(rr >= cc, chunks[last], -jnp.inf)
        mx = chunks[0]
        for ch in chunks[1:]:
            mx = jnp.maximum(mx, ch)
        m_prev = m_ref[h, rows, :]
        m_new = jnp.maximum(m_prev, jnp.max(mx, axis=-1, keepdims=True))
        alpha = jnp.exp2(m_prev - m_new)
        psum = None
        for c, ch in enumerate(chunks):
            part = jnp.exp2(ch - m_new)
            psum = part if psum is None else psum + part
            p_ref[h, rows, c * HEAD_DIM:(c + 1) * HEAD_DIM] = part.astype(BF16)
        if n_chunks * HEAD_DIM < tk:
            p_ref[h, rows, n_chunks * HEAD_DIM:] = jnp.zeros((FOX_ROWS, tk - n_chunks * HEAD_DIM), BF16)
        l_ref[h, rows, :] = alpha * l_ref[h, rows, :] + psum
        alpha_ref[h, rows, :] = alpha
        m_ref[h, rows, :] = m_new


def _fox_values(p_ref, v_ref, alpha_ref, acc_ref, h, cols):
    lanes = slice(h * HEAD_DIM, (h + 1) * HEAD_DIM)
    acc_ref[h] = alpha_ref[h] * acc_ref[h] + _dot(p_ref[h], v_ref[cols, lanes])


def _fox_kernel(q_ref, ktr_ref, fk_ref, v_ref, fcol_ref, fstart_ref, fend_ref, qkb_ref, sg_ref, onw_ref, o_ref,
                kt_ref, qa_ref, s_ref, p_ref, m_ref, l_ref, alpha_ref, acc_ref, *, heads):
    hp = pl.program_id(1)
    qi = pl.program_id(2)
    tq = q_ref.shape[0]

    @pl.when(qi == 0)
    def _():
        n_bias = fk_ref.shape[1]
        kt_ref[:, :HEAD_DIM, :] = ktr_ref[...]
        kt_ref[:, HEAD_DIM:HEAD_DIM + n_bias, :] = fk_ref[...]
        kt_ref[:, HEAD_DIM + n_bias:, :] = jnp.zeros(
            (heads, HEAD_DIM - n_bias, kt_ref.shape[2]), BF16)
    lane = lax.broadcasted_iota(jnp.int32, (tq, HEAD_DIM), 1)
    fcol = fcol_ref[...]
    for h in range(heads):
        fq = jnp.sum(jnp.where(lane == hp * heads + h, fcol, 0.0), axis=-1, keepdims=True) * LOG2E
        hi, mid, lo = _split3(fq)
        faug = jnp.where(lane == 0, hi.astype(F32),
                         jnp.where(lane == 1, mid.astype(F32),
                                   jnp.where(lane == 2, lo.astype(F32),
                                             jnp.where(lane < 6, 1.0, 0.0))))
        qa_ref[h, :, :HEAD_DIM] = q_ref[:, h * HEAD_DIM:(h + 1) * HEAD_DIM]
        qa_ref[h, :, HEAD_DIM:] = faug.astype(BF16)
    m_ref[...] = jnp.full(m_ref.shape, NEG_BIG, F32)
    l_ref[...] = jnp.zeros(l_ref.shape, F32)
    acc_ref[...] = jnp.zeros(acc_ref.shape, F32)

    def block(kb, diag):
        cols = pl.ds(pl.multiple_of(kb * tq, tq), tq)
        for h in range(heads):
            _fox_scores(qa_ref, kt_ref, s_ref, h, h, cols)
        for h in range(heads):
            _fox_softmax(s_ref, p_ref, m_ref, l_ref, alpha_ref, h, h, diag)
        for h in range(heads):
            _fox_values(p_ref, v_ref, alpha_ref, acc_ref, h, cols)

    block(qi, True)
    lane8 = lax.broadcasted_iota(jnp.int32, fend_ref.shape, 1)
    kb8 = lax.broadcasted_iota(jnp.int32, fend_ref.shape, 0)
    gap = 2.0 * qkb_ref[...] + LOG2E * (fstart_ref[pl.ds(qi, 1), :] - fend_ref[...])
    mine = (lane8 >= hp * heads) & (lane8 < (hp + 1) * heads) & (kb8 < qi)
    live = jnp.where(mine & (gap >= -FOX_SKIP_BITS), 1.0, 0.0)
    n_live = jnp.sum(jnp.max(live, axis=1, keepdims=True)).astype(jnp.int32)

    def body(j, carry):
        block(qi - 1 - j, False)
        return carry

    lax.fori_loop(0, n_live, body, 0)

    for h in range(heads):
        lanes = slice(h * HEAD_DIM, (h + 1) * HEAD_DIM)
        l = jnp.sum(l_ref[h], axis=-1, keepdims=True)
        o = acc_ref[h] / l
        ms = jnp.mean(o * o, axis=-1, keepdims=True)
        y = o * lax.rsqrt(ms + EPS) * onw_ref[:, lanes]
        o_ref[:, lanes] = (y * sg_ref[:, lanes].astype(F32)).astype(o_ref.dtype)


def _fox_mix(q, kt, fk_rows, v, fcol, qk_bound, sg, out_norm_w, *, batch, seq):
    n_rows, d = q.shape
    n_heads = d // HEAD_DIM
    heads = min(FOX_HEADS, n_heads)
    tq = min(FOX_TQ, seq)
    assert seq % tq == 0 and tq % FOX_ROWS == 0 and tq % HEAD_DIM == 0 and n_heads % heads == 0
    nq = seq // tq
    n_pairs = n_heads // heads
    qblk = pl.BlockSpec((tq, heads * HEAD_DIM), lambda b, h, i: (b * nq + i, h))
    fblocks = fcol.reshape(batch, nq, tq, HEAD_DIM)
    fstart, fend = fblocks[:, :, 0, :], fblocks[:, :, tq - 1, :]
    fedge = pl.BlockSpec((None, nq, HEAD_DIM), lambda b, h, i: (b, 0, 0))
    qkb_row = jnp.full((1, HEAD_DIM), qk_bound, F32)
    return pl.pallas_call(
        functools.partial(_fox_kernel, heads=heads),
        grid=(batch, n_pairs, nq),
        in_specs=[qblk,
                  pl.BlockSpec((heads, HEAD_DIM, seq), lambda b, h, i: (b * n_pairs + h, 0, 0)),
                  pl.BlockSpec((heads, fk_rows.shape[1], seq), lambda b, h, i: (b * n_pairs + h, 0, 0)),
                  pl.BlockSpec((seq, heads * HEAD_DIM), lambda b, h, i: (b, h)),
                  pl.BlockSpec((tq, HEAD_DIM), lambda b, h, i: (b * nq + i, 0)),
                  fedge, fedge,
                  pl.BlockSpec((1, HEAD_DIM), lambda b, h, i: (0, 0)),
                  qblk,
                  pl.BlockSpec((1, heads * HEAD_DIM), lambda b, h, i: (0, h))],
        out_specs=qblk,
        out_shape=jax.ShapeDtypeStruct((n_rows, d), BF16),
        scratch_shapes=[pltpu.VMEM((heads, 2 * HEAD_DIM, seq), BF16),
                        pltpu.VMEM((heads, tq, 2 * HEAD_DIM), BF16),
                        pltpu.VMEM((heads, tq, tq), F32),
                        pltpu.VMEM((heads, tq, tq), BF16),
                        pltpu.VMEM((heads, tq, HEAD_DIM), F32),
                        pltpu.VMEM((heads, tq, HEAD_DIM), F32),
                        pltpu.VMEM((heads, tq, HEAD_DIM), F32),
                        pltpu.VMEM((heads, tq, HEAD_DIM), F32)],
        compiler_params=pltpu.CompilerParams(
            dimension_semantics=("parallel", "parallel", "arbitrary"),
            vmem_limit_bytes=VMEM_LIMIT_BYTES),
        name="fox_attention",
    )(q, kt, fk_rows, v, fcol, fstart, fend, qkb_row, sg, out_norm_w.reshape(1, d))


def _fox_bias_rows(fk3, *, batch, seq, n_heads):
    terms = fk3.reshape(batch, seq, 3, HEAD_DIM)[:, :, :, :n_heads].transpose(0, 3, 2, 1)
    rows = jnp.concatenate([jnp.ones_like(terms), terms], axis=2)
    rows = jnp.pad(rows, ((0, 0), (0, 0), (0, BF16_SUBLANES - 6), (0, 0)))
    return rows.reshape(batch * n_heads, BF16_SUBLANES, seq)


def kernel(x, a_norm_w, a_w_in, a_lb_logits, a_out_norm_w, a_w_out, kv_norm_w, kv_w, kv_f_bias, k_norm_w,
           b_norm_w, b_w_in, b_q_norm_w, b_out_norm_w, b_w_out):
    batch, seq, d = x.shape
    n_heads = d // HEAD_DIM
    n_a = a_w_in.shape[0]
    n_b = b_w_in.shape[0]
    xr = x.reshape(batch * seq, d)

    a_w_in_b = a_w_in.astype(BF16)
    a_w_out_b = a_w_out.astype(BF16)
    b_w_in_b = b_w_in.astype(BF16)
    b_w_out_b = b_w_out.astype(BF16)
    kv_w_b = kv_w[:, :2 * d].astype(BF16)[None]
    wf_b = jnp.pad(kv_w[:, 2 * d:], ((0, 0), (0, HEAD_DIM - n_heads))).astype(BF16)
    f_bias = jnp.pad(kv_f_bias.astype(F32), (0, HEAD_DIM - n_heads)).reshape(1, HEAD_DIM)

    lb_all = jnp.cumsum(jax.nn.softmax(a_lb_logits.astype(F32), axis=0), axis=0)
    lb_all = lb_all - lb_all[0:1]
    ones_row = jnp.ones((1, d), F32)

    h = _norm(xr, a_norm_w[0])
    for layer in range(n_a):
        q, w2, k, v, sg = _proj(
            h,
            [(a_w_in_b, layer, 0, ones_row, "cast", (BF16,)),
             (a_w_in_b, layer, d, lb_all[layer].reshape(1, d), "hgate", (F32, BF16)),
             (a_w_in_b, layer, 2 * d, ones_row, "cast", (BF16,)),
             (a_w_in_b, layer, 3 * d, ones_row, "silu", (BF16,))],
            name="hgrn_in_proj", seq=seq, tm=1024)
        og = _hgrn_mix(q, w2, k, v, sg, a_out_norm_w[layer], batch=batch, seq=seq)
        if layer + 1 < n_a:
            xr, (h,) = _out_proj(og, a_w_out_b, layer, xr, [a_norm_w[layer + 1]])
        else:
            xr, (h_kv, h) = _out_proj(og, a_w_out_b, layer, xr, [kv_norm_w, b_norm_w[0]])

    k_norm_row = jnp.tile(k_norm_w.astype(F32), n_heads).reshape(1, d)
    kt, vv = _proj(
        h_kv,
        [(kv_w_b, 0, 0, k_norm_row, "headnorm_t", (BF16,)),
         (kv_w_b, 0, d, ones_row, "cast", (BF16,))],
        name="fox_kv_proj", seq=seq)
    fcol, fk3 = _fgate(h_kv, wf_b, f_bias, batch=batch, seq=seq)
    fk_rows = _fox_bias_rows(fk3, batch=batch, seq=seq, n_heads=n_heads)

    for j in range(n_b):
        q_norm_row = jnp.tile(b_q_norm_w[j].astype(F32), n_heads).reshape(1, d)
        q, sg = _proj(
            h,
            [(b_w_in_b, j, 0, q_norm_row, "headnorm", (BF16,)),
             (b_w_in_b, j, d, ones_row, "silu", (BF16,))],
            scale=HEAD_DIM ** -0.5 * LOG2E, name="fox_in_proj", seq=seq)
        qk_bound = (HEAD_DIM * jnp.max(jnp.abs(b_q_norm_w[j])) * jnp.max(jnp.abs(k_norm_w))
                    * (HEAD_DIM ** -0.5 * LOG2E * 1.02))
        og = _fox_mix(q, kt, fk_rows, vv, fcol, qk_bound, sg, b_out_norm_w[j], batch=batch, seq=seq)
        if j + 1 < n_b:
            xr, (h,) = _out_proj(og, b_w_out_b, j, xr, [b_norm_w[j + 1]])
        else:
            xr, _ = _out_proj(og, b_w_out_b, j, xr, [])

    return xr.reshape(batch, seq, d)
```

```python
import functools

import jax
import jax.numpy as jnp
from jax import lax
from jax.experimental import pallas as pl
from jax.experimental.pallas import tpu as pltpu

HEAD_DIM = 128
VREG_ROWS = 8
SUB_BLOCK = 8
BF16_SUBLANES = 16
EPS = 1e-6
VMEM_LIMIT_BYTES = 56 * 1024 * 1024
NEG_BIG = -1e30
LOG2E = 1.4426950408889634
FOX_TQ = 512
FOX_ROWS = 64
FOX_HEADS = 4
FOX_BIAS_GROUP = 32
FOX_SKIP_BITS = 152.0

F32 = jnp.float32
BF16 = jnp.bfloat16


def _dot(a, b):
    return jnp.dot(a, b, preferred_element_type=F32)


def _dot_nt(a, b):
    return lax.dot_general(a, b, (((1,), (1,)), ((), ())), preferred_element_type=F32)


def _dot_tn(a, b):
    return lax.dot_general(a, b, (((0,), (0,)), ((), ())), preferred_element_type=F32)


def _split3(x):
    hi = x.astype(BF16)
    r1 = x - hi.astype(F32)
    mid = r1.astype(BF16)
    lo = (r1 - mid.astype(F32)).astype(BF16)
    return hi, mid, lo


def _scan_sub_block(x):
    n, c = x.shape
    x3 = x.reshape(n // VREG_ROWS, VREG_ROWS, c)
    pos = lax.broadcasted_iota(jnp.int32, x3.shape, 1)
    shift = 1
    while shift < VREG_ROWS:
        x3 = x3 + jnp.where(pos >= shift, pltpu.roll(x3, shift, axis=1), 0.0)
        shift *= 2
    if SUB_BLOCK == 2 * VREG_ROWS:
        tile = lax.broadcasted_iota(jnp.int32, x3.shape, 0)
        carry = jnp.roll(jnp.broadcast_to(x3[:, VREG_ROWS - 1:, :], x3.shape), 1, axis=0)
        x3 = x3 + jnp.where(tile % 2 == 1, carry, 0.0)
    return x3.reshape(n, c)


def _rms_rows(x, w):
    ms = jnp.mean(x * x, axis=-1, keepdims=True)
    return x * lax.rsqrt(ms + EPS) * w


def _proj_kernel(*refs, kinds, scale):
    n = len(kinds)
    h_ref = refs[0]
    w_refs = refs[1:1 + n]
    aux_refs = refs[1 + n:1 + 2 * n]
    out_refs = refs[1 + 2 * n:]

    h = h_ref[...]
    oi = 0
    for s, kind in enumerate(kinds):
        acc = _dot(h, w_refs[s][...])
        aux = aux_refs[s][...]
        if kind == "cast":
            out_refs[oi][...] = acc.astype(out_refs[oi].dtype)
            oi += 1
        elif kind == "silu":
            out_refs[oi][...] = (acc / (1.0 + jnp.exp(-acc))).astype(out_refs[oi].dtype)
            oi += 1
        elif kind in ("headnorm", "headnorm_t"):
            tn = acc.shape[1]
            for c in range(tn // HEAD_DIM):
                sl = slice(c * HEAD_DIM, (c + 1) * HEAD_DIM)
                a = acc[:, sl]
                ms = jnp.mean(a * a, axis=-1, keepdims=True)
                y = a * lax.rsqrt(ms + EPS) * aux[:, sl]
                if scale != 1.0:
                    y = y * scale
                if kind == "headnorm_t":
                    out_refs[oi][c] = y.T.astype(out_refs[oi].dtype)
                else:
                    out_refs[oi][:, sl] = y.astype(out_refs[oi].dtype)
            oi += 1
        elif kind == "hgate":
            lb = aux
            e = jnp.exp2(jnp.abs(acc) * (-LOG2E))
            one_e = 1.0 + e
            log2_sig = jnp.minimum(acc, 0.0) * LOG2E - jnp.log2(one_e)
            a = jnp.log2(lb)
            c = jnp.log2(1.0 - lb) + log2_sig
            g2 = jnp.maximum(a, c) + jnp.log2(1.0 + jnp.exp2(-jnp.abs(a - c)))
            r = 1.0 / one_e
            sig_neg = jnp.where(acc >= 0.0, e * r, r)
            out_refs[oi][...] = _scan_sub_block(g2).astype(out_refs[oi].dtype)
            out_refs[oi + 1][...] = ((1.0 - lb) * sig_neg).astype(out_refs[oi + 1].dtype)
            oi += 2
        else:
            raise ValueError(kind)


def _proj(h, streams, *, name, seq, scale=1.0, tm=1024, tn=512):
    n_rows, d = h.shape
    n_cols = streams[0][3].shape[1]
    tm = min(tm, seq)
    tn = min(tn, n_cols)
    assert seq % tm == 0 and n_rows % seq == 0 and n_cols % tn == 0 and tm % SUB_BLOCK == 0
    kinds = tuple(s[4] for s in streams)
    t_tiles = seq // tm
    heads_per_tile = tn // HEAD_DIM
    n_head_tiles = n_cols // tn

    in_specs = [pl.BlockSpec((tm, d), lambda i, j: (i, 0))]
    args = [h]
    for (w, layer, off, aux, kind, _) in streams:
        assert off % tn == 0
        in_specs.append(pl.BlockSpec((None, d, tn),
                                     functools.partial(lambda i, j, l, o: (l, 0, j + o), l=layer, o=off // tn)))
        args.append(w)
    for (w, layer, off, aux, kind, _) in streams:
        in_specs.append(pl.BlockSpec((1, tn), lambda i, j: (0, j)))
        args.append(aux)
    out_shapes, out_specs = [], []
    for (w, layer, off, aux, kind, dts) in streams:
        for dt in dts:
            if kind == "headnorm_t":
                out_shapes.append(jax.ShapeDtypeStruct((n_rows // seq * n_cols // HEAD_DIM, HEAD_DIM, seq), dt))
                out_specs.append(pl.BlockSpec(
                    (heads_per_tile, HEAD_DIM, tm),
                    lambda i, j: ((i // t_tiles) * n_head_tiles + j, 0, i % t_tiles)))
            else:
                out_shapes.append(jax.ShapeDtypeStruct((n_rows, n_cols), dt))
                out_specs.append(pl.BlockSpec((tm, tn), lambda i, j: (i, j)))

    return pl.pallas_call(
        functools.partial(_proj_kernel, kinds=kinds, scale=scale),
        grid=(n_rows // tm, n_cols // tn),
        in_specs=in_specs,
        out_specs=out_specs,
        out_shape=out_shapes,
        compiler_params=pltpu.CompilerParams(
            dimension_semantics=("parallel", "parallel"),
            vmem_limit_bytes=VMEM_LIMIT_BYTES),
        name=name,
    )(*args)


def _norm_kernel(x_ref, nw_ref, h_ref):
    h_ref[...] = _rms_rows(x_ref[...], nw_ref[...]).astype(h_ref.dtype)


def _norm(x, norm_w, *, tm=1024):
    n_rows, d = x.shape
    tm = min(tm, n_rows)
    assert n_rows % tm == 0
    return pl.pallas_call(
        _norm_kernel,
        grid=(n_rows // tm,),
        in_specs=[pl.BlockSpec((tm, d), lambda i: (i, 0)),
                  pl.BlockSpec((1, d), lambda i: (0, 0))],
        out_specs=pl.BlockSpec((tm, d), lambda i: (i, 0)),
        out_shape=jax.ShapeDtypeStruct((n_rows, d), BF16),
        compiler_params=pltpu.CompilerParams(
            dimension_semantics=("parallel",),
            vmem_limit_bytes=VMEM_LIMIT_BYTES),
        name="input_norm",
    )(x, norm_w.reshape(1, d))


def _out_proj_kernel(o_ref, w_ref, x_ref, *refs):
    n_next = (len(refs) - 1) // 2
    nw_refs, y_ref, h_refs = refs[:n_next], refs[n_next], refs[n_next + 1:]
    y = x_ref[...] + _dot(o_ref[...], w_ref[...])
    y_ref[...] = y
    if n_next:
        yn = y * lax.rsqrt(jnp.mean(y * y, axis=-1, keepdims=True) + EPS)
        for nw_ref, h_ref in zip(nw_refs, h_refs):
            h_ref[...] = (yn * nw_ref[...]).astype(h_ref.dtype)


def _out_proj(o, w3d, layer, x, next_norm_ws, *, tm=512):
    n_rows, d = x.shape
    tm = min(tm, n_rows)
    assert n_rows % tm == 0
    rows = pl.BlockSpec((tm, d), lambda i: (i, 0))
    gain = pl.BlockSpec((1, d), lambda i: (0, 0))
    n_next = len(next_norm_ws)
    outs = pl.pallas_call(
        _out_proj_kernel,
        grid=(n_rows // tm,),
        in_specs=[rows, pl.BlockSpec((None, d, d), functools.partial(lambda i, l: (l, 0, 0), l=layer)), rows]
                 + [gain] * n_next,
        out_specs=[rows] * (1 + n_next),
        out_shape=[jax.ShapeDtypeStruct((n_rows, d), F32)]
                  + [jax.ShapeDtypeStruct((n_rows, d), BF16)] * n_next,
        compiler_params=pltpu.CompilerParams(
            dimension_semantics=("parallel",),
            vmem_limit_bytes=VMEM_LIMIT_BYTES),
        name="out_proj",
    )(o, w3d, x, *[w.reshape(1, d) for w in next_norm_ws])
    return outs[0], list(outs[1:])


def _excl_prefix_rows(x):
    pos = lax.broadcasted_iota(jnp.int32, x.shape, 0)
    inc = x
    shift = 1
    while shift < x.shape[0]:
        inc = inc + jnp.where(pos >= shift, pltpu.roll(inc, shift, axis=0), 0.0)
        shift *= 2
    return inc - x


def _level_mid(bs2, lvl):
    n = bs2.shape[0]
    group = 2 << lvl
    if group >= VREG_ROWS:
        tiles = [jnp.broadcast_to(bs2[(t0 // group) * group + group // 2:][:1], (VREG_ROWS, HEAD_DIM))
                 for t0 in range(0, n, VREG_ROWS)]
        return jnp.concatenate(tiles, axis=0)
    x3 = bs2.reshape(n // VREG_ROWS, VREG_ROWS, HEAD_DIM)
    pos = lax.broadcasted_iota(jnp.int32, x3.shape, 1)
    out = None
    for g0 in range(0, VREG_ROWS, group):
        piece = jnp.broadcast_to(x3[:, g0 + group // 2:g0 + group // 2 + 1, :], x3.shape)
        out = piece if out is None else jnp.where(pos >= g0, piece, out)
    return out.reshape(n, HEAD_DIM)


def _rows_of(x, idx):
    return jnp.concatenate([x[i:i + 1] for i in idx], axis=0)


def _hgrn_chunk(q, k, w2, v, st, fac_ref, ck_ref, consts):
    lane8, sub_row, lvl_masks, same_blk, expand_r, jrow = consts
    c = q.shape[0]
    n_sub = c // SUB_BLOCK
    n_lvl = n_sub.bit_length() - 1

    gt2 = _rows_of(w2, [j * SUB_BLOCK + SUB_BLOCK - 1 for j in range(n_sub)])
    bs2 = _excl_prefix_rows(gt2)
    be2 = bs2 + gt2
    tot2 = be2[n_sub - 1:n_sub]
    fac_ref[0] = gt2
    fac_ref[1] = jnp.exp2(bs2)
    fac_ref[2] = jnp.exp2(tot2 - be2)
    for lvl in range(n_lvl):
        mid = _level_mid(bs2, lvl)
        upper = ((jrow >> lvl) & 1) == 1
        fac_ref[3 + 2 * lvl] = jnp.exp2(jnp.where(upper, bs2 - mid, -jnp.inf))
        fac_ref[4 + 2 * lvl] = jnp.exp2(jnp.where(upper, -jnp.inf, mid - be2))
    ck_ref[...] = jnp.log2(k) - w2

    qs_parts, kbar_parts, a_parts = [], [], []
    ql_parts = [[] for _ in range(n_lvl)]
    kl_parts = [[] for _ in range(n_lvl)]
    for j in range(n_sub):
        r0 = j * SUB_BLOCK
        sl = slice(r0, r0 + SUB_BLOCK)
        w_j, q_j = w2[sl], q[sl]
        qt = q_j * jnp.exp2(w_j)
        kh = k[sl] * jnp.exp2(fac_ref[0, j:j + 1, :] - w_j)
        qs_parts.append(qt * fac_ref[1, j:j + 1, :])
        kbar_parts.append(kh * fac_ref[2, j:j + 1, :])
        for lvl in range(n_lvl):
            ql_parts[lvl].append(qt * fac_ref[3 + 2 * lvl, j:j + 1, :])
            kl_parts[lvl].append(kh * fac_ref[4 + 2 * lvl, j:j + 1, :])

        for t0 in range(0, SUB_BLOCK, VREG_ROWS):
            w_t, q_t = w_j[t0:t0 + VREG_ROWS], q_j[t0:t0 + VREG_ROWS]
            a_t = jnp.zeros((VREG_ROWS, HEAD_DIM), F32)
            for s in range(t0 + VREG_ROWS):
                cs = ck_ref[r0 + s:r0 + s + 1, :]
                col = jnp.sum(q_t * jnp.exp2(w_t + cs), axis=-1, keepdims=True)
                a_t = jnp.where(lane8 == s, col, a_t)
            a_parts.append(jnp.where(lane8 <= sub_row + t0, a_t, 0.0))

    cat = lambda parts: jnp.concatenate(parts, axis=0)
    o = _dot_nt(cat(qs_parts).astype(BF16), st.astype(BF16))
    a = _dot_nt(cat(ql_parts[n_lvl - 1]).astype(BF16), cat(kl_parts[n_lvl - 1]).astype(BF16))
    for lvl in range(n_lvl - 2, -1, -1):
        a_l = _dot_nt(cat(ql_parts[lvl]).astype(BF16), cat(kl_parts[lvl]).astype(BF16))
        a = jnp.where(lvl_masks[lvl], a_l, a)
    a_diag = _dot(cat(a_parts).astype(BF16), expand_r)
    a = jnp.where(same_blk, a_diag, a)
    o = o + _dot(a.astype(BF16), v)
    st_new = st * jnp.exp2(tot2) + _dot_tn(v, cat(kbar_parts).astype(BF16))
    return o, st_new


def _hgrn_kernel(q_ref, w2_ref, k_ref, v_ref, sg_ref, onw_ref, o_ref, st_ref, fac_ref, ck_ref, *, chunk, heads):
    tb = q_ref.shape[0]
    n_chunks = tb // chunk
    n_sub = chunk // SUB_BLOCK
    n_lvl = n_sub.bit_length() - 1

    @pl.when(pl.program_id(2) == 0)
    def _():
        st_ref[...] = jnp.zeros_like(st_ref)

    r = lax.broadcasted_iota(jnp.int32, (chunk, chunk), 0) // SUB_BLOCK
    cidx = lax.broadcasted_iota(jnp.int32, (chunk, chunk), 1)
    lane_blk = cidx // SUB_BLOCK
    same_blk = r == lane_blk
    lvl_masks = [(r >> (lvl + 1)) == (lane_blk >> (lvl + 1)) for lvl in range(n_lvl)]
    lane8 = lax.broadcasted_iota(jnp.int32, (VREG_ROWS, HEAD_DIM), 1)
    sub_row = lax.broadcasted_iota(jnp.int32, (VREG_ROWS, HEAD_DIM), 0)
    er = lax.broadcasted_iota(jnp.int32, (HEAD_DIM, chunk), 0)
    ec = lax.broadcasted_iota(jnp.int32, (HEAD_DIM, chunk), 1)
    expand_r = (ec % SUB_BLOCK == er).astype(BF16)
    jrow = lax.broadcasted_iota(jnp.int32, (n_sub, HEAD_DIM), 0)
    consts = (lane8, sub_row, lvl_masks, same_blk, expand_r, jrow)

    def body(ci, carry):
        rows = pl.ds(pl.multiple_of(ci * chunk, chunk), chunk)
        for h in range(heads):
            lanes = slice(h * HEAD_DIM, (h + 1) * HEAD_DIM)
            q = q_ref[rows, lanes].astype(F32)
            k = k_ref[rows, lanes].astype(F32)
            o, st_new = _hgrn_chunk(q, k, w2_ref[rows, lanes], v_ref[rows, lanes], st_ref[h],
                                    fac_ref.at[h], ck_ref.at[h], consts)
            st_ref[h] = st_new
            ms = jnp.mean(o * o, axis=-1, keepdims=True)
            y = o * lax.rsqrt(ms + EPS) * onw_ref[:, lanes]
            o_ref[rows, lanes] = (y * sg_ref[rows, lanes].astype(F32)).astype(o_ref.dtype)
        return carry

    lax.fori_loop(0, n_chunks, body, 0)


def _hgrn_mix(q, w2, k, v, sg, out_norm_w, *, batch, seq, chunk=128, heads=4, tb=512):
    n_rows, d = q.shape
    n_heads = d // HEAD_DIM
    heads = min(heads, n_heads)
    tb = min(tb, seq)
    chunk = min(chunk, tb)
    n_sub = chunk // SUB_BLOCK
    assert seq % tb == 0 and tb % chunk == 0 and n_heads % heads == 0
    assert chunk % SUB_BLOCK == 0 and n_sub & (n_sub - 1) == 0
    nt = seq // tb
    n_fac = 3 + 2 * (n_sub.bit_length() - 1)
    blk = pl.BlockSpec((tb, heads * HEAD_DIM), lambda b, h, t: (b * nt + t, h))
    return pl.pallas_call(
        functools.partial(_hgrn_kernel, chunk=chunk, heads=heads),
        grid=(batch, n_heads // heads, nt),
        in_specs=[blk, blk, blk, blk, blk,
                  pl.BlockSpec((1, heads * HEAD_DIM), lambda b, h, t: (0, h))],
        out_specs=blk,
        out_shape=jax.ShapeDtypeStruct((n_rows, d), BF16),
        scratch_shapes=[pltpu.VMEM((heads, HEAD_DIM, HEAD_DIM), F32),
                        pltpu.VMEM((heads, n_fac, n_sub, HEAD_DIM), F32),
                        pltpu.VMEM((heads, chunk, HEAD_DIM), F32)],
        compiler_params=pltpu.CompilerParams(
            dimension_semantics=("parallel", "parallel", "arbitrary"),
            vmem_limit_bytes=VMEM_LIMIT_BYTES),
        name="hgrn_mix",
    )(q, w2, k, v, sg, out_norm_w.reshape(1, d))


def _fgate_kernel(h_ref, wf_ref, bias_ref, f_ref, fk_ref, fq_ref, carry_ref):
    @pl.when(pl.program_id(1) == 0)
    def _():
        carry_ref[...] = jnp.zeros_like(carry_ref)

    z = _dot(h_ref[...], wf_ref[...]) + bias_ref[...]
    ls = jnp.minimum(z, 0.0) - jnp.log1p(jnp.exp(-jnp.abs(z)))
    tt = z.shape[0]
    r = lax.broadcasted_iota(jnp.int32, (tt, tt), 0)
    c = lax.broadcasted_iota(jnp.int32, (tt, tt), 1)
    tri = (c <= r).astype(BF16)
    hi, mid, lo = _split3(ls)
    cum = _dot(tri, hi) + _dot(tri, mid) + _dot(tri, lo)
    f = cum + carry_ref[...]
    f_ref[...] = f
    carry_ref[...] = f[tt - 1:tt]
    terms = _split3(f * (-LOG2E))
    for i, term in enumerate(terms):
        fk_ref[:, i * HEAD_DIM:(i + 1) * HEAD_DIM] = term
    lane = lax.broadcasted_iota(jnp.int32, f.shape, 1)
    g = FOX_BIAS_GROUP
    t_hi, t_mid, t_lo = [-t.astype(F32) for t in terms]
    fq = jnp.where(lane < g, t_hi,
                   jnp.where(lane < 2 * g, pltpu.roll(t_mid, g, axis=1),
                             jnp.where(lane < 3 * g, pltpu.roll(t_lo, 2 * g, axis=1),
                                       jnp.where(lane < 3 * g + 3, 1.0, 0.0))))
    fq_ref[...] = fq.astype(BF16)


def _fgate(h, wf, bias, *, batch, seq, tt=512):
    n_rows, d = h.shape
    tt = min(tt, seq)
    nt = seq // tt
    return pl.pallas_call(
        _fgate_kernel,
        grid=(batch, nt),
        in_specs=[pl.BlockSpec((tt, d), lambda b, t: (b * nt + t, 0)),
                  pl.BlockSpec((d, HEAD_DIM), lambda b, t: (0, 0)),
                  pl.BlockSpec((1, HEAD_DIM), lambda b, t: (0, 0))],
        out_specs=[pl.BlockSpec((tt, HEAD_DIM), lambda b, t: (b * nt + t, 0)),
                   pl.BlockSpec((tt, 3 * HEAD_DIM), lambda b, t: (b * nt + t, 0)),
                   pl.BlockSpec((tt, HEAD_DIM), lambda b, t: (b * nt + t, 0))],
        out_shape=[jax.ShapeDtypeStruct((n_rows, HEAD_DIM), F32),
                   jax.ShapeDtypeStruct((n_rows, 3 * HEAD_DIM), BF16),
                   jax.ShapeDtypeStruct((n_rows, HEAD_DIM), BF16)],
        scratch_shapes=[pltpu.VMEM((1, HEAD_DIM), F32)],
        compiler_params=pltpu.CompilerParams(
            dimension_semantics=("parallel", "arbitrary"),
            vmem_limit_bytes=VMEM_LIMIT_BYTES),
        name="fox_forget_gate",
    )(h, wf, bias)


def _fox_scores(qa_ref, kt_ref, s_ref, slot, h, cols):
    s_ref[slot] = _dot(qa_ref[h], kt_ref[h, :, cols])


def _fox_softmax(s_ref, p_ref, m_ref, l_ref, alpha_ref, slot, h, diag, first):
    tq, tk = s_ref.shape[1], s_ref.shape[2]
    for r in range(tq // FOX_ROWS):
        row0 = r * FOX_ROWS
        rows = pl.ds(row0, FOX_ROWS)
        n_chunks = (row0 + FOX_ROWS - 1) // HEAD_DIM + 1 if diag else tk // HEAD_DIM
        chunks = [s_ref[slot, rows, c * HEAD_DIM:(c + 1) * HEAD_DIM] for c in range(n_chunks)]
        if diag:
            last = n_chunks - 1
            rr = lax.broadcasted_iota(jnp.int32, (FOX_ROWS, HEAD_DIM), 0) + row0
            cc = lax.broadcasted_iota(jnp.int32, (FOX_ROWS, HEAD_DIM), 1) + last * HEAD_DIM
            chunks[last] = jnp.where(rr >= cc, chunks[last], -jnp.inf)
        mx = chunks[0]
        for ch in chunks[1:]:
            mx = jnp.maximum(mx, ch)
        m_new = jnp.max(mx, axis=-1, keepdims=True)
        if first:
            m_new = jnp.broadcast_to(m_new, (FOX_ROWS, HEAD_DIM))
        else:
            m_prev = m_ref[h, rows, :]
            m_new = jnp.maximum(m_prev, m_new)
            alpha = jnp.exp2(m_prev - m_new)
        psum = None
        for c, ch in enumerate(chunks):
            part = jnp.exp2(ch - m_new)
            psum = part if psum is None else psum + part
            p_ref[h, rows, c * HEAD_DIM:(c + 1) * HEAD_DIM] = part.astype(BF16)
        if n_chunks * HEAD_DIM < tk:
            p_ref[h, rows, n_chunks * HEAD_DIM:] = jnp.zeros((FOX_ROWS, tk - n_chunks * HEAD_DIM), BF16)
        row_sum = jnp.broadcast_to(jnp.sum(psum, axis=-1, keepdims=True), (FOX_ROWS, HEAD_DIM))
        if first:
            l_ref[h, rows, :] = row_sum
        else:
            l_ref[h, rows, :] = alpha * l_ref[h, rows, :] + row_sum
            alpha_ref[h, rows, :] = alpha
        m_ref[h, rows, :] = m_new


def _fox_values(p_ref, v_ref, alpha_ref, acc_ref, h, cols, first):
    lanes = slice(h * HEAD_DIM, (h + 1) * HEAD_DIM)
    pv = _dot(p_ref[h], v_ref[cols, lanes])
    acc_ref[h] = pv if first else alpha_ref[h] * acc_ref[h] + pv


def _fox_kernel(q_ref, ktr_ref, fk_ref, v_ref, fq_ref, fstart_ref, fend_ref, qkb_ref, sg_ref, onw_ref, o_ref,
                kt_ref, qa_ref, s_ref, p_ref, m_ref, l_ref, alpha_ref, acc_ref, *, heads):
    hp = pl.program_id(1)
    qi = pl.program_id(2)
    tq = q_ref.shape[0]

    @pl.when(qi == 0)
    def _():
        n_sel = 3 * FOX_BIAS_GROUP
        n_bias = fk_ref.shape[1]
        seq = kt_ref.shape[2]
        kt_ref[:, :HEAD_DIM, :] = ktr_ref[...]
        sel_row = lax.broadcasted_iota(jnp.int32, (n_sel, seq), 0) % FOX_BIAS_GROUP
        for h in range(heads):
            kt_ref[h, HEAD_DIM:HEAD_DIM + n_sel, :] = jnp.where(sel_row == hp * heads + h, 1.0, 0.0).astype(BF16)
        kt_ref[:, HEAD_DIM + n_sel:HEAD_DIM + n_sel + n_bias, :] = fk_ref[...]
        kt_ref[:, HEAD_DIM + n_sel + n_bias:, :] = jnp.zeros((heads, HEAD_DIM - n_sel - n_bias, seq), BF16)
    for h in range(heads):
        qa_ref[h, :, :HEAD_DIM] = q_ref[:, h * HEAD_DIM:(h + 1) * HEAD_DIM]
        qa_ref[h, :, HEAD_DIM:] = fq_ref[...]

    def block(kb, diag):
        cols = pl.ds(pl.multiple_of(kb * tq, tq), tq)
        for h in range(heads):
            _fox_scores(qa_ref, kt_ref, s_ref, h, h, cols)
        for h in range(heads):
            _fox_softmax(s_ref, p_ref, m_ref, l_ref, alpha_ref, h, h, diag, diag)
        for h in range(heads):
            _fox_values(p_ref, v_ref, alpha_ref, acc_ref, h, cols, diag)

    block(qi, True)
    lane8 = lax.broadcasted_iota(jnp.int32, fend_ref.shape, 1)
    kb8 = lax.broadcasted_iota(jnp.int32, fend_ref.shape, 0)
    gap = 2.0 * qkb_ref[...] + LOG2E * (fstart_ref[pl.ds(qi, 1), :] - fend_ref[...])
    mine = (lane8 >= hp * heads) & (lane8 < (hp + 1) * heads) & (kb8 < qi)
    live = jnp.where(mine & (gap >= -FOX_SKIP_BITS), 1.0, 0.0)
    n_live = jnp.sum(jnp.max(live, axis=1, keepdims=True)).astype(jnp.int32)

    def body(j, carry):
        block(qi - 1 - j, False)
        return carry

    lax.fori_loop(0, n_live, body, 0)

    for h in range(heads):
        lanes = slice(h * HEAD_DIM, (h + 1) * HEAD_DIM)
        o = acc_ref[h] / l_ref[h]
        ms = jnp.mean(o * o, axis=-1, keepdims=True)
        y = o * lax.rsqrt(ms + EPS) * onw_ref[:, lanes]
        o_ref[:, lanes] = (y * sg_ref[:, lanes].astype(F32)).astype(o_ref.dtype)


def _fox_mix(q, kt, fk_rows, v, fcol, fq_cols, qk_bound, sg, out_norm_w, *, batch, seq):
    n_rows, d = q.shape
    n_heads = d // HEAD_DIM
    heads = min(FOX_HEADS, n_heads)
    tq = min(FOX_TQ, seq)
    assert seq % tq == 0 and tq % FOX_ROWS == 0 and tq % HEAD_DIM == 0 and n_heads % heads == 0
    assert n_heads <= FOX_BIAS_GROUP
    nq = seq // tq
    n_pairs = n_heads // heads
    qblk = pl.BlockSpec((tq, heads * HEAD_DIM), lambda b, h, i: (b * nq + i, h))
    fblocks = fcol.reshape(batch, nq, tq, HEAD_DIM)
    fstart, fend = fblocks[:, :, 0, :], fblocks[:, :, tq - 1, :]
    fedge = pl.BlockSpec((None, nq, HEAD_DIM), lambda b, h, i: (b, 0, 0))
    qkb_row = jnp.full((1, HEAD_DIM), qk_bound, F32)
    return pl.pallas_call(
        functools.partial(_fox_kernel, heads=heads),
        grid=(batch, n_pairs, nq),
        in_specs=[qblk,
                  pl.BlockSpec((heads, HEAD_DIM, seq), lambda b, h, i: (b * n_pairs + h, 0, 0)),
                  pl.BlockSpec((heads, fk_rows.shape[1], seq), lambda b, h, i: (b * n_pairs + h, 0, 0)),
                  pl.BlockSpec((seq, heads * HEAD_DIM), lambda b, h, i: (b, h)),
                  pl.BlockSpec((tq, HEAD_DIM), lambda b, h, i: (b * nq + i, 0)),
                  fedge, fedge,
                  pl.BlockSpec((1, HEAD_DIM), lambda b, h, i: (0, 0)),
                  qblk,
                  pl.BlockSpec((1, heads * HEAD_DIM), lambda b, h, i: (0, h))],
        out_specs=qblk,
        out_shape=jax.ShapeDtypeStruct((n_rows, d), BF16),
        scratch_shapes=[pltpu.VMEM((heads, 2 * HEAD_DIM, seq), BF16),
                        pltpu.VMEM((heads, tq, 2 * HEAD_DIM), BF16),
                        pltpu.VMEM((heads, tq, tq), F32),
                        pltpu.VMEM((heads, tq, tq), BF16),
                        pltpu.VMEM((heads, tq, HEAD_DIM), F32),
                        pltpu.VMEM((heads, tq, HEAD_DIM), F32),
                        pltpu.VMEM((heads, tq, HEAD_DIM), F32),
                        pltpu.VMEM((heads, tq, HEAD_DIM), F32)],
        compiler_params=pltpu.CompilerParams(
            dimension_semantics=("parallel", "parallel", "arbitrary"),
            vmem_limit_bytes=VMEM_LIMIT_BYTES),
        name="fox_attention",
    )(q, kt, fk_rows, v, fq_cols, fstart, fend, qkb_row, sg, out_norm_w.reshape(1, d))


def _fox_bias_rows(fk3, *, batch, seq, n_heads):
    rows = fk3.reshape(batch, seq, 3, HEAD_DIM)[:, :, :, :n_heads].transpose(0, 3, 2, 1)
    rows = jnp.pad(rows, ((0, 0), (0, 0), (0, BF16_SUBLANES - 3), (0, 0)))
    return rows.reshape(batch * n_heads, BF16_SUBLANES, seq)


def kernel(x, a_norm_w, a_w_in, a_lb_logits, a_out_norm_w, a_w_out, kv_norm_w, kv_w, kv_f_bias, k_norm_w,
           b_norm_w, b_w_in, b_q_norm_w, b_out_norm_w, b_w_out):
    batch, seq, d = x.shape
    n_heads = d // HEAD_DIM
    n_a = a_w_in.shape[0]
    n_b = b_w_in.shape[0]
    xr = x.reshape(batch * seq, d)

    a_w_in_b = a_w_in.astype(BF16)
    a_w_out_b = a_w_out.astype(BF16)
    b_w_in_b = b_w_in.astype(BF16)
    b_w_out_b = b_w_out.astype(BF16)
    kv_w_b = kv_w[:, :2 * d].astype(BF16)[None]
    wf_b = jnp.pad(kv_w[:, 2 * d:], ((0, 0), (0, HEAD_DIM - n_heads))).astype(BF16)
    f_bias = jnp.pad(kv_f_bias.astype(F32), (0, HEAD_DIM - n_heads)).reshape(1, HEAD_DIM)

    lb_all = jnp.cumsum(jax.nn.softmax(a_lb_logits.astype(F32), axis=0), axis=0)
    lb_all = lb_all - lb_all[0:1]
    ones_row = jnp.ones((1, d), F32)

    h = _norm(xr, a_norm_w[0])
    for layer in range(n_a):
        q, w2, k, v, sg = _proj(
            h,
            [(a_w_in_b, layer, 0, ones_row, "cast", (BF16,)),
             (a_w_in_b, layer, d, lb_all[layer].reshape(1, d), "hgate", (F32, BF16)),
             (a_w_in_b, layer, 2 * d, ones_row, "cast", (BF16,)),
             (a_w_in_b, layer, 3 * d, ones_row, "silu", (BF16,))],
            name="hgrn_in_proj", seq=seq, tm=1024, tn=256)
        og = _hgrn_mix(q, w2, k, v, sg, a_out_norm_w[layer], batch=batch, seq=seq)
        if layer + 1 < n_a:
            xr, (h,) = _out_proj(og, a_w_out_b, layer, xr, [a_norm_w[layer + 1]])
        else:
            xr, (h_kv, h) = _out_proj(og, a_w_out_b, layer, xr, [kv_norm_w, b_norm_w[0]])

    k_norm_row = jnp.tile(k_norm_w.astype(F32), n_heads).reshape(1, d)
    kt, vv = _proj(
        h_kv,
        [(kv_w_b, 0, 0, k_norm_row, "headnorm_t", (BF16,)),
         (kv_w_b, 0, d, ones_row, "cast", (BF16,))],
        name="fox_kv_proj", seq=seq)
    fcol, fk3, fq_cols = _fgate(h_kv, wf_b, f_bias, batch=batch, seq=seq)
    fk_rows = _fox_bias_rows(fk3, batch=batch, seq=seq, n_heads=n_heads)

    for j in range(n_b):
        q_norm_row = jnp.tile(b_q_norm_w[j].astype(F32), n_heads).reshape(1, d)
        q, sg = _proj(
            h,
            [(b_w_in_b, j, 0, q_norm_row, "headnorm", (BF16,)),
             (b_w_in_b, j, d, ones_row, "silu", (BF16,))],
            scale=HEAD_DIM ** -0.5 * LOG2E, name="fox_in_proj", seq=seq)
        qk_bound = (HEAD_DIM * jnp.max(jnp.abs(b_q_norm_w[j])) * jnp.max(jnp.abs(k_norm_w))
                    * (HEAD_DIM ** -0.5 * LOG2E * 1.02))
        og = _fox_mix(q, kt, fk_rows, vv, fcol, fq_cols, qk_bound, sg, b_out_norm_w[j], batch=batch, seq=seq)
        if j + 1 < n_b:
            xr, (h,) = _out_proj(og, b_w_out_b, j, xr, [b_norm_w[j + 1]])
        else:
            xr, _ = _out_proj(og, b_w_out_b, j, xr, [])

    return xr.reshape(batch, seq, d)
```

```python
import functools

import jax
import jax.numpy as jnp
from jax import lax
from jax.experimental import pallas as pl
from jax.experimental.pallas import tpu as pltpu

HEAD_DIM = 128
VREG_ROWS = 8
SUB_BLOCK = 8
BF16_SUBLANES = 16
EPS = 1e-6
VMEM_LIMIT_BYTES = 56 * 1024 * 1024
NEG_BIG = -1e30
LOG2E = 1.4426950408889634
FOX_TQ = 512
FOX_ROWS = 64
FOX_HEADS = 4
FOX_BIAS_GROUP = 32
FOX_SKIP_BITS = 152.0

F32 = jnp.float32
BF16 = jnp.bfloat16


def _dot(a, b):
    return jnp.dot(a, b, preferred_element_type=F32)


def _dot_nt(a, b):
    return lax.dot_general(a, b, (((1,), (1,)), ((), ())), preferred_element_type=F32)


def _dot_tn(a, b):
    return lax.dot_general(a, b, (((0,), (0,)), ((), ())), preferred_element_type=F32)


def _split3(x):
    hi = x.astype(BF16)
    r1 = x - hi.astype(F32)
    mid = r1.astype(BF16)
    lo = (r1 - mid.astype(F32)).astype(BF16)
    return hi, mid, lo


def _scan_sub_block(x):
    n, c = x.shape
    x3 = x.reshape(n // VREG_ROWS, VREG_ROWS, c)
    pos = lax.broadcasted_iota(jnp.int32, x3.shape, 1)
    shift = 1
    while shift < VREG_ROWS:
        x3 = x3 + jnp.where(pos >= shift, pltpu.roll(x3, shift, axis=1), 0.0)
        shift *= 2
    if SUB_BLOCK == 2 * VREG_ROWS:
        tile = lax.broadcasted_iota(jnp.int32, x3.shape, 0)
        carry = jnp.roll(jnp.broadcast_to(x3[:, VREG_ROWS - 1:, :], x3.shape), 1, axis=0)
        x3 = x3 + jnp.where(tile % 2 == 1, carry, 0.0)
    return x3.reshape(n, c)


def _rms_rows(x, w):
    ms = jnp.mean(x * x, axis=-1, keepdims=True)
    return x * lax.rsqrt(ms + EPS) * w


def _proj_kernel(*refs, kinds, scale):
    n = len(kinds)
    h_ref = refs[0]
    w_refs = refs[1:1 + n]
    aux_refs = refs[1 + n:1 + 2 * n]
    out_refs = refs[1 + 2 * n:]

    h = h_ref[...]
    oi = 0
    for s, kind in enumerate(kinds):
        acc = _dot(h, w_refs[s][...])
        aux = aux_refs[s][...]
        if kind == "cast":
            out_refs[oi][...] = acc.astype(out_refs[oi].dtype)
            oi += 1
        elif kind == "silu":
            out_refs[oi][...] = (acc / (1.0 + jnp.exp(-acc))).astype(out_refs[oi].dtype)
            oi += 1
        elif kind in ("headnorm", "headnorm_t"):
            tn = acc.shape[1]
            for c in range(tn // HEAD_DIM):
                sl = slice(c * HEAD_DIM, (c + 1) * HEAD_DIM)
                a = acc[:, sl]
                ms = jnp.mean(a * a, axis=-1, keepdims=True)
                y = a * lax.rsqrt(ms + EPS) * aux[:, sl]
                if scale != 1.0:
                    y = y * scale
                if kind == "headnorm_t":
                    out_refs[oi][c] = y.T.astype(out_refs[oi].dtype)
                else:
                    out_refs[oi][:, sl] = y.astype(out_refs[oi].dtype)
            oi += 1
        elif kind == "hgate":
            lb = aux
            e = jnp.exp2(jnp.abs(acc) * (-LOG2E))
            one_e = 1.0 + e
            log2_sig = jnp.minimum(acc, 0.0) * LOG2E - jnp.log2(one_e)
            a = jnp.log2(lb)
            c = jnp.log2(1.0 - lb) + log2_sig
            g2 = jnp.maximum(a, c) + jnp.log2(1.0 + jnp.exp2(-jnp.abs(a - c)))
            r = 1.0 / one_e
            sig_neg = jnp.where(acc >= 0.0, e * r, r)
            out_refs[oi][...] = _scan_sub_block(g2).astype(out_refs[oi].dtype)
            out_refs[oi + 1][...] = ((1.0 - lb) * sig_neg).astype(out_refs[oi + 1].dtype)
            oi += 2
        else:
            raise ValueError(kind)


def _proj(h, streams, *, name, seq, scale=1.0, tm=1024, tn=512):
    n_rows, d = h.shape
    n_cols = streams[0][3].shape[1]
    tm = min(tm, seq)
    tn = min(tn, n_cols)
    assert seq % tm == 0 and n_rows % seq == 0 and n_cols % tn == 0 and tm % SUB_BLOCK == 0
    kinds = tuple(s[4] for s in streams)
    t_tiles = seq // tm
    heads_per_tile = tn // HEAD_DIM
    n_head_tiles = n_cols // tn

    in_specs = [pl.BlockSpec((tm, d), lambda i, j: (i, 0))]
    args = [h]
    for (w, layer, off, aux, kind, _) in streams:
        assert off % tn == 0
        in_specs.append(pl.BlockSpec((None, d, tn),
                                     functools.partial(lambda i, j, l, o: (l, 0, j + o), l=layer, o=off // tn)))
        args.append(w)
    for (w, layer, off, aux, kind, _) in streams:
        in_specs.append(pl.BlockSpec((1, tn), lambda i, j: (0, j)))
        args.append(aux)
    out_shapes, out_specs = [], []
    for (w, layer, off, aux, kind, dts) in streams:
        for dt in dts:
            if kind == "headnorm_t":
                out_shapes.append(jax.ShapeDtypeStruct((n_rows // seq * n_cols // HEAD_DIM, HEAD_DIM, seq), dt))
                out_specs.append(pl.BlockSpec(
                    (heads_per_tile, HEAD_DIM, tm),
                    lambda i, j: ((i // t_tiles) * n_head_tiles + j, 0, i % t_tiles)))
            else:
                out_shapes.append(jax.ShapeDtypeStruct((n_rows, n_cols), dt))
                out_specs.append(pl.BlockSpec((tm, tn), lambda i, j: (i, j)))

    return pl.pallas_call(
        functools.partial(_proj_kernel, kinds=kinds, scale=scale),
        grid=(n_rows // tm, n_cols // tn),
        in_specs=in_specs,
        out_specs=out_specs,
        out_shape=out_shapes,
        compiler_params=pltpu.CompilerParams(
            dimension_semantics=("parallel", "parallel"),
            vmem_limit_bytes=VMEM_LIMIT_BYTES),
        name=name,
    )(*args)


def _norm_kernel(x_ref, nw_ref, h_ref):
    h_ref[...] = _rms_rows(x_ref[...], nw_ref[...]).astype(h_ref.dtype)


def _norm(x, norm_w, *, tm=1024):
    n_rows, d = x.shape
    tm = min(tm, n_rows)
    assert n_rows % tm == 0
    return pl.pallas_call(
        _norm_kernel,
        grid=(n_rows // tm,),
        in_specs=[pl.BlockSpec((tm, d), lambda i: (i, 0)),
                  pl.BlockSpec((1, d), lambda i: (0, 0))],
        out_specs=pl.BlockSpec((tm, d), lambda i: (i, 0)),
        out_shape=jax.ShapeDtypeStruct((n_rows, d), BF16),
        compiler_params=pltpu.CompilerParams(
            dimension_semantics=("parallel",),
            vmem_limit_bytes=VMEM_LIMIT_BYTES),
        name="input_norm",
    )(x, norm_w.reshape(1, d))


def _out_proj_kernel(o_ref, w_ref, x_ref, *refs):
    n_next = (len(refs) - 1) // 2
    nw_refs, y_ref, h_refs = refs[:n_next], refs[n_next], refs[n_next + 1:]
    y = x_ref[...] + _dot(o_ref[...], w_ref[...])
    y_ref[...] = y
    if n_next:
        yn = y * lax.rsqrt(jnp.mean(y * y, axis=-1, keepdims=True) + EPS)
        for nw_ref, h_ref in zip(nw_refs, h_refs):
            h_ref[...] = (yn * nw_ref[...]).astype(h_ref.dtype)


def _out_proj(o, w3d, layer, x, next_norm_ws, *, tm=512):
    n_rows, d = x.shape
    tm = min(tm, n_rows)
    assert n_rows % tm == 0
    rows = pl.BlockSpec((tm, d), lambda i: (i, 0))
    gain = pl.BlockSpec((1, d), lambda i: (0, 0))
    n_next = len(next_norm_ws)
    outs = pl.pallas_call(
        _out_proj_kernel,
        grid=(n_rows // tm,),
        in_specs=[rows, pl.BlockSpec((None, d, d), functools.partial(lambda i, l: (l, 0, 0), l=layer)), rows]
                 + [gain] * n_next,
        out_specs=[rows] * (1 + n_next),
        out_shape=[jax.ShapeDtypeStruct((n_rows, d), F32)]
                  + [jax.ShapeDtypeStruct((n_rows, d), BF16)] * n_next,
        compiler_params=pltpu.CompilerParams(
            dimension_semantics=("parallel",),
            vmem_limit_bytes=VMEM_LIMIT_BYTES),
        name="out_proj",
    )(o, w3d, x, *[w.reshape(1, d) for w in next_norm_ws])
    return outs[0], list(outs[1:])


def _excl_prefix_rows(x):
    pos = lax.broadcasted_iota(jnp.int32, x.shape, 0)
    inc = x
    shift = 1
    while shift < x.shape[0]:
        inc = inc + jnp.where(pos >= shift, pltpu.roll(inc, shift, axis=0), 0.0)
        shift *= 2
    return inc - x


def _level_mid(bs2, lvl):
    n = bs2.shape[0]
    group = 2 << lvl
    if group >= VREG_ROWS:
        tiles = [jnp.broadcast_to(bs2[(t0 // group) * group + group // 2:][:1], (VREG_ROWS, HEAD_DIM))
                 for t0 in range(0, n, VREG_ROWS)]
        return jnp.concatenate(tiles, axis=0)
    x3 = bs2.reshape(n // VREG_ROWS, VREG_ROWS, HEAD_DIM)
    pos = lax.broadcasted_iota(jnp.int32, x3.shape, 1)
    out = None
    for g0 in range(0, VREG_ROWS, group):
        piece = jnp.broadcast_to(x3[:, g0 + group // 2:g0 + group // 2 + 1, :], x3.shape)
        out = piece if out is None else jnp.where(pos >= g0, piece, out)
    return out.reshape(n, HEAD_DIM)


def _rows_of(x, idx):
    return jnp.concatenate([x[i:i + 1] for i in idx], axis=0)


def _hgrn_chunk(q, k, w2, v, st, fac_ref, ck_ref, consts):
    lane8, sub_row, lvl_masks, same_blk, expand_r, jrow = consts
    c = q.shape[0]
    n_sub = c // SUB_BLOCK
    n_lvl = n_sub.bit_length() - 1

    gt2 = _rows_of(w2, [j * SUB_BLOCK + SUB_BLOCK - 1 for j in range(n_sub)])
    bs2 = _excl_prefix_rows(gt2)
    be2 = bs2 + gt2
    tot2 = be2[n_sub - 1:n_sub]
    fac_ref[0] = gt2
    fac_ref[1] = jnp.exp2(bs2)
    fac_ref[2] = jnp.exp2(tot2 - be2)
    for lvl in range(n_lvl):
        mid = _level_mid(bs2, lvl)
        upper = ((jrow >> lvl) & 1) == 1
        fac_ref[3 + 2 * lvl] = jnp.exp2(jnp.where(upper, bs2 - mid, -jnp.inf))
        fac_ref[4 + 2 * lvl] = jnp.exp2(jnp.where(upper, -jnp.inf, mid - be2))
    ck_ref[...] = jnp.log2(k) - w2

    qs_parts, kbar_parts, a_parts = [], [], []
    ql_parts = [[] for _ in range(n_lvl)]
    kl_parts = [[] for _ in range(n_lvl)]
    for j in range(n_sub):
        r0 = j * SUB_BLOCK
        sl = slice(r0, r0 + SUB_BLOCK)
        w_j, q_j = w2[sl], q[sl]
        qt = q_j * jnp.exp2(w_j)
        kh = k[sl] * jnp.exp2(fac_ref[0, j:j + 1, :] - w_j)
        qs_parts.append(qt * fac_ref[1, j:j + 1, :])
        kbar_parts.append(kh * fac_ref[2, j:j + 1, :])
        for lvl in range(n_lvl):
            ql_parts[lvl].append(qt * fac_ref[3 + 2 * lvl, j:j + 1, :])
            kl_parts[lvl].append(kh * fac_ref[4 + 2 * lvl, j:j + 1, :])

        for t0 in range(0, SUB_BLOCK, VREG_ROWS):
            w_t, q_t = w_j[t0:t0 + VREG_ROWS], q_j[t0:t0 + VREG_ROWS]
            a_t = jnp.zeros((VREG_ROWS, HEAD_DIM), F32)
            for s in range(t0 + VREG_ROWS):
                cs = ck_ref[r0 + s:r0 + s + 1, :]
                col = jnp.sum(q_t * jnp.exp2(w_t + cs), axis=-1, keepdims=True)
                a_t = jnp.where(lane8 == s, col, a_t)
            a_parts.append(jnp.where(lane8 <= sub_row + t0, a_t, 0.0))

    cat = lambda parts: jnp.concatenate(parts, axis=0)
    o = _dot_nt(cat(qs_parts).astype(BF16), st.astype(BF16))
    a = _dot_nt(cat(ql_parts[n_lvl - 1]).astype(BF16), cat(kl_parts[n_lvl - 1]).astype(BF16))
    for lvl in range(n_lvl - 2, -1, -1):
        a_l = _dot_nt(cat(ql_parts[lvl]).astype(BF16), cat(kl_parts[lvl]).astype(BF16))
        a = jnp.where(lvl_masks[lvl], a_l, a)
    a_diag = _dot(cat(a_parts).astype(BF16), expand_r)
    a = jnp.where(same_blk, a_diag, a)
    o = o + _dot(a.astype(BF16), v)
    st_new = st * jnp.exp2(tot2) + _dot_tn(v, cat(kbar_parts).astype(BF16))
    return o, st_new


def _hgrn_kernel(q_ref, w2_ref, k_ref, v_ref, sg_ref, onw_ref, o_ref, st_ref, fac_ref, ck_ref, *, chunk, heads):
    tb = q_ref.shape[0]
    n_chunks = tb // chunk
    n_sub = chunk // SUB_BLOCK
    n_lvl = n_sub.bit_length() - 1

    @pl.when(pl.program_id(2) == 0)
    def _():
        st_ref[...] = jnp.zeros_like(st_ref)

    r = lax.broadcasted_iota(jnp.int32, (chunk, chunk), 0) // SUB_BLOCK
    cidx = lax.broadcasted_iota(jnp.int32, (chunk, chunk), 1)
    lane_blk = cidx // SUB_BLOCK
    same_blk = r == lane_blk
    lvl_masks = [(r >> (lvl + 1)) == (lane_blk >> (lvl + 1)) for lvl in range(n_lvl)]
    lane8 = lax.broadcasted_iota(jnp.int32, (VREG_ROWS, HEAD_DIM), 1)
    sub_row = lax.broadcasted_iota(jnp.int32, (VREG_ROWS, HEAD_DIM), 0)
    er = lax.broadcasted_iota(jnp.int32, (HEAD_DIM, chunk), 0)
    ec = lax.broadcasted_iota(jnp.int32, (HEAD_DIM, chunk), 1)
    expand_r = (ec % SUB_BLOCK == er).astype(BF16)
    jrow = lax.broadcasted_iota(jnp.int32, (n_sub, HEAD_DIM), 0)
    consts = (lane8, sub_row, lvl_masks, same_blk, expand_r, jrow)

    for ci in range(n_chunks):
        rows = slice(ci * chunk, (ci + 1) * chunk)
        for h in range(heads):
            lanes = slice(h * HEAD_DIM, (h + 1) * HEAD_DIM)
            q = q_ref[rows, lanes].astype(F32)
            k = k_ref[rows, lanes].astype(F32)
            o, st_new = _hgrn_chunk(q, k, w2_ref[rows, lanes], v_ref[rows, lanes], st_ref[h],
                                    fac_ref.at[ci * heads + h], ck_ref.at[ci * heads + h], consts)
            st_ref[h] = st_new
            ms = jnp.mean(o * o, axis=-1, keepdims=True)
            y = o * lax.rsqrt(ms + EPS) * onw_ref[:, lanes]
            o_ref[rows, lanes] = (y * sg_ref[rows, lanes].astype(F32)).astype(o_ref.dtype)


def _hgrn_mix(q, w2, k, v, sg, out_norm_w, *, batch, seq, chunk=128, heads=4, tb=512):
    n_rows, d = q.shape
    n_heads = d // HEAD_DIM
    heads = min(heads, n_heads)
    tb = min(tb, seq)
    chunk = min(chunk, tb)
    n_sub = chunk // SUB_BLOCK
    assert seq % tb == 0 and tb % chunk == 0 and n_heads % heads == 0
    assert chunk % SUB_BLOCK == 0 and n_sub & (n_sub - 1) == 0
    nt = seq // tb
    n_fac = 3 + 2 * (n_sub.bit_length() - 1)
    blk = pl.BlockSpec((tb, heads * HEAD_DIM), lambda b, h, t: (b * nt + t, h))
    return pl.pallas_call(
        functools.partial(_hgrn_kernel, chunk=chunk, heads=heads),
        grid=(batch, n_heads // heads, nt),
        in_specs=[blk, blk, blk, blk, blk,
                  pl.BlockSpec((1, heads * HEAD_DIM), lambda b, h, t: (0, h))],
        out_specs=blk,
        out_shape=jax.ShapeDtypeStruct((n_rows, d), BF16),
        scratch_shapes=[pltpu.VMEM((heads, HEAD_DIM, HEAD_DIM), F32),
                        pltpu.VMEM((tb // chunk * heads, n_fac, n_sub, HEAD_DIM), F32),
                        pltpu.VMEM((tb // chunk * heads, chunk, HEAD_DIM), F32)],
        compiler_params=pltpu.CompilerParams(
            dimension_semantics=("parallel", "parallel", "arbitrary"),
            vmem_limit_bytes=VMEM_LIMIT_BYTES),
        name="hgrn_mix",
    )(q, w2, k, v, sg, out_norm_w.reshape(1, d))


def _fgate_kernel(h_ref, wf_ref, bias_ref, f_ref, fk_ref, fq_ref, carry_ref):
    @pl.when(pl.program_id(1) == 0)
    def _():
        carry_ref[...] = jnp.zeros_like(carry_ref)

    z = _dot(h_ref[...], wf_ref[...]) + bias_ref[...]
    ls = jnp.minimum(z, 0.0) - jnp.log1p(jnp.exp(-jnp.abs(z)))
    tt = z.shape[0]
    r = lax.broadcasted_iota(jnp.int32, (tt, tt), 0)
    c = lax.broadcasted_iota(jnp.int32, (tt, tt), 1)
    tri = (c <= r).astype(BF16)
    hi, mid, lo = _split3(ls)
    cum = _dot(tri, hi) + _dot(tri, mid) + _dot(tri, lo)
    f = cum + carry_ref[...]
    f_ref[...] = f
    carry_ref[...] = f[tt - 1:tt]
    terms = _split3(f * (-LOG2E))
    for i, term in enumerate(terms):
        fk_ref[:, i * HEAD_DIM:(i + 1) * HEAD_DIM] = term
    lane = lax.broadcasted_iota(jnp.int32, f.shape, 1)
    g = FOX_BIAS_GROUP
    t_hi, t_mid, t_lo = [-t.astype(F32) for t in terms]
    fq = jnp.where(lane < g, t_hi,
                   jnp.where(lane < 2 * g, pltpu.roll(t_mid, g, axis=1),
                             jnp.where(lane < 3 * g, pltpu.roll(t_lo, 2 * g, axis=1),
                                       jnp.where(lane < 3 * g + 3, 1.0, 0.0))))
    fq_ref[...] = fq.astype(BF16)


def _fgate(h, wf, bias, *, batch, seq, tt=512):
    n_rows, d = h.shape
    tt = min(tt, seq)
    nt = seq // tt
    return pl.pallas_call(
        _fgate_kernel,
        grid=(batch, nt),
        in_specs=[pl.BlockSpec((tt, d), lambda b, t: (b * nt + t, 0)),
                  pl.BlockSpec((d, HEAD_DIM), lambda b, t: (0, 0)),
                  pl.BlockSpec((1, HEAD_DIM), lambda b, t: (0, 0))],
        out_specs=[pl.BlockSpec((tt, HEAD_DIM), lambda b, t: (b * nt + t, 0)),
                   pl.BlockSpec((tt, 3 * HEAD_DIM), lambda b, t: (b * nt + t, 0)),
                   pl.BlockSpec((tt, HEAD_DIM), lambda b, t: (b * nt + t, 0))],
        out_shape=[jax.ShapeDtypeStruct((n_rows, HEAD_DIM), F32),
                   jax.ShapeDtypeStruct((n_rows, 3 * HEAD_DIM), BF16),
                   jax.ShapeDtypeStruct((n_rows, HEAD_DIM), BF16)],
        scratch_shapes=[pltpu.VMEM((1, HEAD_DIM), F32)],
        compiler_params=pltpu.CompilerParams(
            dimension_semantics=("parallel", "arbitrary"),
            vmem_limit_bytes=VMEM_LIMIT_BYTES),
        name="fox_forget_gate",
    )(h, wf, bias)


def _fox_scores(qa_ref, kt_ref, s_ref, slot, h, cols):
    s_ref[slot] = _dot(qa_ref[h], kt_ref[h, :, cols])


def _fox_softmax(s_ref, p_ref, m_ref, l_ref, alpha_ref, slot, h, diag, first):
    tq, tk = s_ref.shape[1], s_ref.shape[2]
    for r in range(tq // FOX_ROWS):
        row0 = r * FOX_ROWS
        rows = pl.ds(row0, FOX_ROWS)
        n_chunks = (row0 + FOX_ROWS - 1) // HEAD_DIM + 1 if diag else tk // HEAD_DIM
        chunks = [s_ref[slot, rows, c * HEAD_DIM:(c + 1) * HEAD_DIM] for c in range(n_chunks)]
        if diag:
            last = n_chunks - 1
            rr = lax.broadcasted_iota(jnp.int32, (FOX_ROWS, HEAD_DIM), 0) + row0
            cc = lax.broadcasted_iota(jnp.int32, (FOX_ROWS, HEAD_DIM), 1) + last * HEAD_DIM
            chunks[last] = jnp.where(rr >= cc, chunks[last], -jnp.inf)
        mx = chunks[0]
        for ch in chunks[1:]:
            mx = jnp.maximum(mx, ch)
        m_new = jnp.max(mx, axis=-1, keepdims=True)
        if first:
            m_new = jnp.broadcast_to(m_new, (FOX_ROWS, HEAD_DIM))
        else:
            m_prev = m_ref[h, rows, :]
            m_new = jnp.maximum(m_prev, m_new)
            alpha = jnp.exp2(m_prev - m_new)
        psum = None
        for c, ch in enumerate(chunks):
            part = jnp.exp2(ch - m_new)
            psum = part if psum is None else psum + part
            p_ref[h, rows, c * HEAD_DIM:(c + 1) * HEAD_DIM] = part.astype(BF16)
        if n_chunks * HEAD_DIM < tk:
            p_ref[h, rows, n_chunks * HEAD_DIM:] = jnp.zeros((FOX_ROWS, tk - n_chunks * HEAD_DIM), BF16)
        row_sum = jnp.broadcast_to(jnp.sum(psum, axis=-1, keepdims=True), (FOX_ROWS, HEAD_DIM))
        if first:
            l_ref[h, rows, :] = row_sum
        else:
            l_ref[h, rows, :] = alpha * l_ref[h, rows, :] + row_sum
            alpha_ref[h, rows, :] = alpha
        m_ref[h, rows, :] = m_new


def _fox_values(p_ref, v_ref, alpha_ref, acc_ref, h, cols, first):
    lanes = slice(h * HEAD_DIM, (h + 1) * HEAD_DIM)
    pv = _dot(p_ref[h], v_ref[cols, lanes])
    acc_ref[h] = pv if first else alpha_ref[h] * acc_ref[h] + pv


def _fox_kernel(q_ref, ktr_ref, fk_ref, v_ref, fq_ref, fstart_ref, fend_ref, qkb_ref, sg_ref, onw_ref, o_ref,
                kt_ref, qa_ref, s_ref, p_ref, m_ref, l_ref, alpha_ref, acc_ref, *, heads):
    hp = pl.program_id(1)
    qi = pl.program_id(2)
    tq = q_ref.shape[0]

    @pl.when(qi == 0)
    def _():
        n_sel = 3 * FOX_BIAS_GROUP
        n_bias = fk_ref.shape[1]
        seq = kt_ref.shape[2]
        kt_ref[:, :HEAD_DIM, :] = ktr_ref[...]
        sel_row = lax.broadcasted_iota(jnp.int32, (n_sel, seq), 0) % FOX_BIAS_GROUP
        for h in range(heads):
            kt_ref[h, HEAD_DIM:HEAD_DIM + n_sel, :] = jnp.where(sel_row == hp * heads + h, 1.0, 0.0).astype(BF16)
        kt_ref[:, HEAD_DIM + n_sel:HEAD_DIM + n_sel + n_bias, :] = fk_ref[...]
        kt_ref[:, HEAD_DIM + n_sel + n_bias:, :] = jnp.zeros((heads, HEAD_DIM - n_sel - n_bias, seq), BF16)
    for h in range(heads):
        qa_ref[h, :, :HEAD_DIM] = q_ref[:, h * HEAD_DIM:(h + 1) * HEAD_DIM]
        qa_ref[h, :, HEAD_DIM:] = fq_ref[...]

    def block(kb, diag):
        cols = pl.ds(pl.multiple_of(kb * tq, tq), tq)
        for h in range(heads):
            _fox_scores(qa_ref, kt_ref, s_ref, h, h, cols)
        for h in range(heads):
            _fox_softmax(s_ref, p_ref, m_ref, l_ref, alpha_ref, h, h, diag, diag)
        for h in range(heads):
            _fox_values(p_ref, v_ref, alpha_ref, acc_ref, h, cols, diag)

    block(qi, True)
    lane8 = lax.broadcasted_iota(jnp.int32, fend_ref.shape, 1)
    kb8 = lax.broadcasted_iota(jnp.int32, fend_ref.shape, 0)
    gap = 2.0 * qkb_ref[...] + LOG2E * (fstart_ref[pl.ds(qi, 1), :] - fend_ref[...])
    mine = (lane8 >= hp * heads) & (lane8 < (hp + 1) * heads) & (kb8 < qi)
    live = jnp.where(mine & (gap >= -FOX_SKIP_BITS), 1.0, 0.0)
    n_live = jnp.sum(jnp.max(live, axis=1, keepdims=True)).astype(jnp.int32)

    def body(j, carry):
        block(qi - 1 - j, False)
        return carry

    lax.fori_loop(0, n_live, body, 0)

    for h in range(heads):
        lanes = slice(h * HEAD_DIM, (h + 1) * HEAD_DIM)
        o = acc_ref[h] / l_ref[h]
        ms = jnp.mean(o * o, axis=-1, keepdims=True)
        y = o * lax.rsqrt(ms + EPS) * onw_ref[:, lanes]
        o_ref[:, lanes] = (y * sg_ref[:, lanes].astype(F32)).astype(o_ref.dtype)


def _fox_mix(q, kt, fk_rows, v, fcol, fq_cols, qk_bound, sg, out_norm_w, *, batch, seq):
    n_rows, d = q.shape
    n_heads = d // HEAD_DIM
    heads = min(FOX_HEADS, n_heads)
    tq = min(FOX_TQ, seq)
    assert seq % tq == 0 and tq % FOX_ROWS == 0 and tq % HEAD_DIM == 0 and n_heads % heads == 0
    assert n_heads <= FOX_BIAS_GROUP
    nq = seq // tq
    n_pairs = n_heads // heads
    qblk = pl.BlockSpec((tq, heads * HEAD_DIM), lambda b, h, i: (b * nq + i, h))
    fblocks = fcol.reshape(batch, nq, tq, HEAD_DIM)
    fstart, fend = fblocks[:, :, 0, :], fblocks[:, :, tq - 1, :]
    fedge = pl.BlockSpec((None, nq, HEAD_DIM), lambda b, h, i: (b, 0, 0))
    qkb_row = jnp.full((1, HEAD_DIM), qk_bound, F32)
    return pl.pallas_call(
        functools.partial(_fox_kernel, heads=heads),
        grid=(batch, n_pairs, nq),
        in_specs=[qblk,
                  pl.BlockSpec((heads, HEAD_DIM, seq), lambda b, h, i: (b * n_pairs + h, 0, 0)),
                  pl.BlockSpec((heads, fk_rows.shape[1], seq), lambda b, h, i: (b * n_pairs + h, 0, 0)),
                  pl.BlockSpec((seq, heads * HEAD_DIM), lambda b, h, i: (b, h)),
                  pl.BlockSpec((tq, HEAD_DIM), lambda b, h, i: (b * nq + i, 0)),
                  fedge, fedge,
                  pl.BlockSpec((1, HEAD_DIM), lambda b, h, i: (0, 0)),
                  qblk,
                  pl.BlockSpec((1, heads * HEAD_DIM), lambda b, h, i: (0, h))],
        out_specs=qblk,
        out_shape=jax.ShapeDtypeStruct((n_rows, d), BF16),
        scratch_shapes=[pltpu.VMEM((heads, 2 * HEAD_DIM, seq), BF16),
                        pltpu.VMEM((heads, tq, 2 * HEAD_DIM), BF16),
                        pltpu.VMEM((heads, tq, tq), F32),
                        pltpu.VMEM((heads, tq, tq), BF16),
                        pltpu.VMEM((heads, tq, HEAD_DIM), F32),
                        pltpu.VMEM((heads, tq, HEAD_DIM), F32),
                        pltpu.VMEM((heads, tq, HEAD_DIM), F32),
                        pltpu.VMEM((heads, tq, HEAD_DIM), F32)],
        compiler_params=pltpu.CompilerParams(
            dimension_semantics=("parallel", "parallel", "arbitrary"),
            vmem_limit_bytes=VMEM_LIMIT_BYTES),
        name="fox_attention",
    )(q, kt, fk_rows, v, fq_cols, fstart, fend, qkb_row, sg, out_norm_w.reshape(1, d))


def _fox_bias_rows(fk3, *, batch, seq, n_heads):
    rows = fk3.reshape(batch, seq, 3, HEAD_DIM)[:, :, :, :n_heads].transpose(0, 3, 2, 1)
    rows = jnp.pad(rows, ((0, 0), (0, 0), (0, BF16_SUBLANES - 3), (0, 0)))
    return rows.reshape(batch * n_heads, BF16_SUBLANES, seq)


def kernel(x, a_norm_w, a_w_in, a_lb_logits, a_out_norm_w, a_w_out, kv_norm_w, kv_w, kv_f_bias, k_norm_w,
           b_norm_w, b_w_in, b_q_norm_w, b_out_norm_w, b_w_out):
    batch, seq, d = x.shape
    n_heads = d // HEAD_DIM
    n_a = a_w_in.shape[0]
    n_b = b_w_in.shape[0]
    xr = x.reshape(batch * seq, d)

    a_w_in_b = a_w_in.astype(BF16)
    a_w_out_b = a_w_out.astype(BF16)
    b_w_in_b = b_w_in.astype(BF16)
    b_w_out_b = b_w_out.astype(BF16)
    kv_w_b = kv_w[:, :2 * d].astype(BF16)[None]
    wf_b = jnp.pad(kv_w[:, 2 * d:], ((0, 0), (0, HEAD_DIM - n_heads))).astype(BF16)
    f_bias = jnp.pad(kv_f_bias.astype(F32), (0, HEAD_DIM - n_heads)).reshape(1, HEAD_DIM)

    lb_all = jnp.cumsum(jax.nn.softmax(a_lb_logits.astype(F32), axis=0), axis=0)
    lb_all = lb_all - lb_all[0:1]
    ones_row = jnp.ones((1, d), F32)

    h = _norm(xr, a_norm_w[0])
    for layer in range(n_a):
        q, w2, k, v, sg = _proj(
            h,
            [(a_w_in_b, layer, 0, ones_row, "cast", (BF16,)),
             (a_w_in_b, layer, d, lb_all[layer].reshape(1, d), "hgate", (F32, BF16)),
             (a_w_in_b, layer, 2 * d, ones_row, "cast", (BF16,)),
             (a_w_in_b, layer, 3 * d, ones_row, "silu", (BF16,))],
            name="hgrn_in_proj", seq=seq, tm=1024, tn=256)
        og = _hgrn_mix(q, w2, k, v, sg, a_out_norm_w[layer], batch=batch, seq=seq)
        if layer + 1 < n_a:
            xr, (h,) = _out_proj(og, a_w_out_b, layer, xr, [a_norm_w[layer + 1]])
        else:
            xr, (h_kv, h) = _out_proj(og, a_w_out_b, layer, xr, [kv_norm_w, b_norm_w[0]])

    k_norm_row = jnp.tile(k_norm_w.astype(F32), n_heads).reshape(1, d)
    kt, vv = _proj(
        h_kv,
        [(kv_w_b, 0, 0, k_norm_row, "headnorm_t", (BF16,)),
         (kv_w_b, 0, d, ones_row, "cast", (BF16,))],
        name="fox_kv_proj", seq=seq)
    fcol, fk3, fq_cols = _fgate(h_kv, wf_b, f_bias, batch=batch, seq=seq)
    fk_rows = _fox_bias_rows(fk3, batch=batch, seq=seq, n_heads=n_heads)

    for j in range(n_b):
        q_norm_row = jnp.tile(b_q_norm_w[j].astype(F32), n_heads).reshape(1, d)
        q, sg = _proj(
            h,
            [(b_w_in_b, j, 0, q_norm_row, "headnorm", (BF16,)),
             (b_w_in_b, j, d, ones_row, "silu", (BF16,))],
            scale=HEAD_DIM ** -0.5 * LOG2E, name="fox_in_proj", seq=seq)
        qk_bound = (HEAD_DIM * jnp.max(jnp.abs(b_q_norm_w[j])) * jnp.max(jnp.abs(k_norm_w))
                    * (HEAD_DIM ** -0.5 * LOG2E * 1.02))
        og = _fox_mix(q, kt, fk_rows, vv, fcol, fq_cols, qk_bound, sg, b_out_norm_w[j], batch=batch, seq=seq)
        if j + 1 < n_b:
            xr, (h,) = _out_proj(og, b_w_out_b, j, xr, [b_norm_w[j + 1]])
        else:
            xr, _ = _out_proj(og, b_w_out_b, j, xr, [])

    return xr.reshape(batch, seq, d)
```

```python
import functools

import jax
import jax.numpy as jnp
from jax import lax
from jax.experimental import pallas as pl
from jax.experimental.pallas import tpu as pltpu

HEAD_DIM = 128
VREG_ROWS = 8
SUB_BLOCK = 8
BF16_SUBLANES = 16
EPS = 1e-6
VMEM_LIMIT_BYTES = 56 * 1024 * 1024
LOG2E = 1.4426950408889634
FOX_TQ = 512
FOX_ROWS = 64
FOX_HEADS = 4
FOX_BIAS_GROUP = 32
FOX_SKIP_BITS = 152.0

F32 = jnp.float32
BF16 = jnp.bfloat16


def _dot(a, b):
    return jnp.dot(a, b, preferred_element_type=F32)


def _dot_nt(a, b):
    return lax.dot_general(a, b, (((1,), (1,)), ((), ())), preferred_element_type=F32)


def _dot_tn(a, b):
    return lax.dot_general(a, b, (((0,), (0,)), ((), ())), preferred_element_type=F32)


def _split3(x):
    hi = x.astype(BF16)
    r1 = x - hi.astype(F32)
    mid = r1.astype(BF16)
    lo = (r1 - mid.astype(F32)).astype(BF16)
    return hi, mid, lo


def _scan_sub_block(x):
    n, c = x.shape
    x3 = x.reshape(n // VREG_ROWS, VREG_ROWS, c)
    pos = lax.broadcasted_iota(jnp.int32, x3.shape, 1)
    shift = 1
    while shift < VREG_ROWS:
        x3 = x3 + jnp.where(pos >= shift, pltpu.roll(x3, shift, axis=1), 0.0)
        shift *= 2
    if SUB_BLOCK == 2 * VREG_ROWS:
        tile = lax.broadcasted_iota(jnp.int32, x3.shape, 0)
        carry = jnp.roll(jnp.broadcast_to(x3[:, VREG_ROWS - 1:, :], x3.shape), 1, axis=0)
        x3 = x3 + jnp.where(tile % 2 == 1, carry, 0.0)
    return x3.reshape(n, c)


def _rms_rows(x, w):
    ms = jnp.mean(x * x, axis=-1, keepdims=True)
    return x * lax.rsqrt(ms + EPS) * w


def _proj_kernel(*refs, kinds, scale):
    n = len(kinds)
    h_ref = refs[0]
    w_refs = refs[1:1 + n]
    aux_refs = refs[1 + n:1 + 2 * n]
    out_refs = refs[1 + 2 * n:]

    h = h_ref[...]
    oi = 0
    for s, kind in enumerate(kinds):
        acc = _dot(h, w_refs[s][...])
        aux = aux_refs[s][...]
        if kind == "cast":
            out_refs[oi][...] = acc.astype(out_refs[oi].dtype)
            oi += 1
        elif kind == "silu":
            out_refs[oi][...] = (acc / (1.0 + jnp.exp(-acc))).astype(out_refs[oi].dtype)
            oi += 1
        elif kind in ("headnorm", "headnorm_t"):
            tn = acc.shape[1]
            for c in range(tn // HEAD_DIM):
                sl = slice(c * HEAD_DIM, (c + 1) * HEAD_DIM)
                a = acc[:, sl]
                ms = jnp.mean(a * a, axis=-1, keepdims=True)
                y = a * lax.rsqrt(ms + EPS) * aux[:, sl]
                if scale != 1.0:
                    y = y * scale
                if kind == "headnorm_t":
                    out_refs[oi][c] = y.T.astype(out_refs[oi].dtype)
                else:
                    out_refs[oi][:, sl] = y.astype(out_refs[oi].dtype)
            oi += 1
        elif kind == "hgate":
            lb = aux
            e = jnp.exp2(jnp.abs(acc) * (-LOG2E))
            one_e = 1.0 + e
            log2_sig = jnp.minimum(acc, 0.0) * LOG2E - jnp.log2(one_e)
            a = jnp.log2(lb)
            c = jnp.log2(1.0 - lb) + log2_sig
            g2 = jnp.maximum(a, c) + jnp.log2(1.0 + jnp.exp2(-jnp.abs(a - c)))
            r = 1.0 / one_e
            sig_neg = jnp.where(acc >= 0.0, e * r, r)
            out_refs[oi][...] = _scan_sub_block(g2).astype(out_refs[oi].dtype)
            out_refs[oi + 1][...] = ((1.0 - lb) * sig_neg).astype(out_refs[oi + 1].dtype)
            oi += 2
        else:
            raise ValueError(kind)


def _proj(h, streams, *, name, seq, scale=1.0, tm=1024, tn=512):
    n_rows, d = h.shape
    n_cols = streams[0][3].shape[1]
    tm = min(tm, seq)
    tn = min(tn, n_cols)
    assert seq % tm == 0 and n_rows % seq == 0 and n_cols % tn == 0 and tm % SUB_BLOCK == 0
    kinds = tuple(s[4] for s in streams)
    t_tiles = seq // tm
    heads_per_tile = tn // HEAD_DIM
    n_head_tiles = n_cols // tn

    in_specs = [pl.BlockSpec((tm, d), lambda i, j: (i, 0))]
    args = [h]
    for (w, layer, off, aux, kind, _) in streams:
        assert off % tn == 0
        in_specs.append(pl.BlockSpec((None, d, tn),
                                     functools.partial(lambda i, j, l, o: (l, 0, j + o), l=layer, o=off // tn)))
        args.append(w)
    for (w, layer, off, aux, kind, _) in streams:
        in_specs.append(pl.BlockSpec((1, tn), lambda i, j: (0, j)))
        args.append(aux)
    out_shapes, out_specs = [], []
    for (w, layer, off, aux, kind, dts) in streams:
        for dt in dts:
            if kind == "headnorm_t":
                out_shapes.append(jax.ShapeDtypeStruct((n_rows // seq * n_cols // HEAD_DIM, HEAD_DIM, seq), dt))
                out_specs.append(pl.BlockSpec(
                    (heads_per_tile, HEAD_DIM, tm),
                    lambda i, j: ((i // t_tiles) * n_head_tiles + j, 0, i % t_tiles)))
            else:
                out_shapes.append(jax.ShapeDtypeStruct((n_rows, n_cols), dt))
                out_specs.append(pl.BlockSpec((tm, tn), lambda i, j: (i, j)))

    return pl.pallas_call(
        functools.partial(_proj_kernel, kinds=kinds, scale=scale),
        grid=(n_rows // tm, n_cols // tn),
        in_specs=in_specs,
        out_specs=out_specs,
        out_shape=out_shapes,
        compiler_params=pltpu.CompilerParams(
            dimension_semantics=("parallel", "parallel"),
            vmem_limit_bytes=VMEM_LIMIT_BYTES),
        name=name,
    )(*args)


def _norm_kernel(x_ref, nw_ref, h_ref):
    h_ref[...] = _rms_rows(x_ref[...], nw_ref[...]).astype(h_ref.dtype)


def _norm(x, norm_w, *, tm=1024):
    n_rows, d = x.shape
    tm = min(tm, n_rows)
    assert n_rows % tm == 0
    return pl.pallas_call(
        _norm_kernel,
        grid=(n_rows // tm,),
        in_specs=[pl.BlockSpec((tm, d), lambda i: (i, 0)),
                  pl.BlockSpec((1, d), lambda i: (0, 0))],
        out_specs=pl.BlockSpec((tm, d), lambda i: (i, 0)),
        out_shape=jax.ShapeDtypeStruct((n_rows, d), BF16),
        compiler_params=pltpu.CompilerParams(
            dimension_semantics=("parallel",),
            vmem_limit_bytes=VMEM_LIMIT_BYTES),
        name="input_norm",
    )(x, norm_w.reshape(1, d))


def _out_proj_kernel(o_ref, w_ref, x_ref, *refs):
    n_next = (len(refs) - 1) // 2
    nw_refs, y_ref, h_refs = refs[:n_next], refs[n_next], refs[n_next + 1:]
    y = x_ref[...] + _dot(o_ref[...], w_ref[...])
    y_ref[...] = y
    if n_next:
        yn = y * lax.rsqrt(jnp.mean(y * y, axis=-1, keepdims=True) + EPS)
        for nw_ref, h_ref in zip(nw_refs, h_refs):
            h_ref[...] = (yn * nw_ref[...]).astype(h_ref.dtype)


def _out_proj(o, w3d, layer, x, next_norm_ws, *, tm=512):
    n_rows, d = x.shape
    tm = min(tm, n_rows)
    assert n_rows % tm == 0
    rows = pl.BlockSpec((tm, d), lambda i: (i, 0))
    gain = pl.BlockSpec((1, d), lambda i: (0, 0))
    n_next = len(next_norm_ws)
    outs = pl.pallas_call(
        _out_proj_kernel,
        grid=(n_rows // tm,),
        in_specs=[rows, pl.BlockSpec((None, d, d), functools.partial(lambda i, l: (l, 0, 0), l=layer)), rows]
                 + [gain] * n_next,
        out_specs=[rows] * (1 + n_next),
        out_shape=[jax.ShapeDtypeStruct((n_rows, d), F32)]
                  + [jax.ShapeDtypeStruct((n_rows, d), BF16)] * n_next,
        compiler_params=pltpu.CompilerParams(
            dimension_semantics=("parallel",),
            vmem_limit_bytes=VMEM_LIMIT_BYTES),
        name="out_proj",
    )(o, w3d, x, *[w.reshape(1, d) for w in next_norm_ws])
    return outs[0], list(outs[1:])


def _excl_prefix_rows(x):
    pos = lax.broadcasted_iota(jnp.int32, x.shape, 0)
    inc = x
    shift = 1
    while shift < x.shape[0]:
        inc = inc + jnp.where(pos >= shift, pltpu.roll(inc, shift, axis=0), 0.0)
        shift *= 2
    return inc - x


def _level_mid(bs2, lvl):
    n = bs2.shape[0]
    group = 2 << lvl
    if group >= VREG_ROWS:
        tiles = [jnp.broadcast_to(bs2[(t0 // group) * group + group // 2:][:1], (VREG_ROWS, HEAD_DIM))
                 for t0 in range(0, n, VREG_ROWS)]
        return jnp.concatenate(tiles, axis=0)
    x3 = bs2.reshape(n // VREG_ROWS, VREG_ROWS, HEAD_DIM)
    pos = lax.broadcasted_iota(jnp.int32, x3.shape, 1)
    out = None
    for g0 in range(0, VREG_ROWS, group):
        piece = jnp.broadcast_to(x3[:, g0 + group // 2:g0 + group // 2 + 1, :], x3.shape)
        out = piece if out is None else jnp.where(pos >= g0, piece, out)
    return out.reshape(n, HEAD_DIM)


def _rows_of(x, idx):
    return jnp.concatenate([x[i:i + 1] for i in idx], axis=0)


def _hgrn_chunk(q, k, w2, v, st, fac_ref, ck_ref, consts):
    lane8, sub_row, lvl_masks, same_blk, expand_r, jrow = consts
    c = q.shape[0]
    n_sub = c // SUB_BLOCK
    n_lvl = n_sub.bit_length() - 1

    gt2 = _rows_of(w2, [j * SUB_BLOCK + SUB_BLOCK - 1 for j in range(n_sub)])
    bs2 = _excl_prefix_rows(gt2)
    be2 = bs2 + gt2
    tot2 = be2[n_sub - 1:n_sub]
    fac_ref[0] = gt2
    fac_ref[1] = jnp.exp2(bs2)
    fac_ref[2] = jnp.exp2(tot2 - be2)
    for lvl in range(n_lvl):
        mid = _level_mid(bs2, lvl)
        upper = ((jrow >> lvl) & 1) == 1
        fac_ref[3 + 2 * lvl] = jnp.exp2(jnp.where(upper, bs2 - mid, -jnp.inf))
        fac_ref[4 + 2 * lvl] = jnp.exp2(jnp.where(upper, -jnp.inf, mid - be2))
    ck_ref[...] = jnp.log2(k) - w2

    qs_parts, kbar_parts, a_parts = [], [], []
    ql_parts = [[] for _ in range(n_lvl)]
    kl_parts = [[] for _ in range(n_lvl)]
    for j in range(n_sub):
        r0 = j * SUB_BLOCK
        sl = slice(r0, r0 + SUB_BLOCK)
        w_j, q_j = w2[sl], q[sl]
        qt = q_j * jnp.exp2(w_j)
        kh = k[sl] * jnp.exp2(fac_ref[0, j:j + 1, :] - w_j)
        qs_parts.append(qt * fac_ref[1, j:j + 1, :])
        kbar_parts.append(kh * fac_ref[2, j:j + 1, :])
        for lvl in range(n_lvl):
            ql_parts[lvl].append(qt * fac_ref[3 + 2 * lvl, j:j + 1, :])
            kl_parts[lvl].append(kh * fac_ref[4 + 2 * lvl, j:j + 1, :])

        for t0 in range(0, SUB_BLOCK, VREG_ROWS):
            w_t, q_t = w_j[t0:t0 + VREG_ROWS], q_j[t0:t0 + VREG_ROWS]
            a_t = jnp.zeros((VREG_ROWS, HEAD_DIM), F32)
            for s in range(t0 + VREG_ROWS):
                cs = ck_ref[r0 + s:r0 + s + 1, :]
                col = jnp.sum(q_t * jnp.exp2(w_t + cs), axis=-1, keepdims=True)
                a_t = jnp.where(lane8 == s, col, a_t)
            a_parts.append(jnp.where(lane8 <= sub_row + t0, a_t, 0.0))

    cat = lambda parts: jnp.concatenate(parts, axis=0)
    o = _dot_nt(cat(qs_parts).astype(BF16), st.astype(BF16))
    a = _dot_nt(cat(ql_parts[n_lvl - 1]).astype(BF16), cat(kl_parts[n_lvl - 1]).astype(BF16))
    for lvl in range(n_lvl - 2, -1, -1):
        a_l = _dot_nt(cat(ql_parts[lvl]).astype(BF16), cat(kl_parts[lvl]).astype(BF16))
        a = jnp.where(lvl_masks[lvl], a_l, a)
    a_diag = _dot(cat(a_parts).astype(BF16), expand_r)
    a = jnp.where(same_blk, a_diag, a)
    o = o + _dot(a.astype(BF16), v)
    st_new = st * jnp.exp2(tot2) + _dot_tn(v, cat(kbar_parts).astype(BF16))
    return o, st_new


def _hgrn_kernel(q_ref, w2_ref, k_ref, v_ref, sg_ref, onw_ref, o_ref, st_ref, fac_ref, ck_ref, *, chunk, heads):
    tb = q_ref.shape[0]
    n_chunks = tb // chunk
    n_sub = chunk // SUB_BLOCK
    n_lvl = n_sub.bit_length() - 1

    @pl.when(pl.program_id(2) == 0)
    def _():
        st_ref[...] = jnp.zeros_like(st_ref)

    r = lax.broadcasted_iota(jnp.int32, (chunk, chunk), 0) // SUB_BLOCK
    cidx = lax.broadcasted_iota(jnp.int32, (chunk, chunk), 1)
    lane_blk = cidx // SUB_BLOCK
    same_blk = r == lane_blk
    lvl_masks = [(r >> (lvl + 1)) == (lane_blk >> (lvl + 1)) for lvl in range(n_lvl)]
    lane8 = lax.broadcasted_iota(jnp.int32, (VREG_ROWS, HEAD_DIM), 1)
    sub_row = lax.broadcasted_iota(jnp.int32, (VREG_ROWS, HEAD_DIM), 0)
    er = lax.broadcasted_iota(jnp.int32, (HEAD_DIM, chunk), 0)
    ec = lax.broadcasted_iota(jnp.int32, (HEAD_DIM, chunk), 1)
    expand_r = (ec % SUB_BLOCK == er).astype(BF16)
    jrow = lax.broadcasted_iota(jnp.int32, (n_sub, HEAD_DIM), 0)
    consts = (lane8, sub_row, lvl_masks, same_blk, expand_r, jrow)

    for ci in range(n_chunks):
        rows = slice(ci * chunk, (ci + 1) * chunk)
        for h in range(heads):
            lanes = slice(h * HEAD_DIM, (h + 1) * HEAD_DIM)
            q = q_ref[rows, lanes].astype(F32)
            k = k_ref[rows, lanes].astype(F32)
            o, st_new = _hgrn_chunk(q, k, w2_ref[rows, lanes], v_ref[rows, lanes], st_ref[h],
                                    fac_ref.at[ci * heads + h], ck_ref.at[ci * heads + h], consts)
            st_ref[h] = st_new
            ms = jnp.mean(o * o, axis=-1, keepdims=True)
            y = o * lax.rsqrt(ms + EPS) * onw_ref[:, lanes]
            o_ref[rows, lanes] = (y * sg_ref[rows, lanes].astype(F32)).astype(o_ref.dtype)


def _hgrn_mix(q, w2, k, v, sg, out_norm_w, *, batch, seq, chunk=128, heads=4, tb=512):
    n_rows, d = q.shape
    n_heads = d // HEAD_DIM
    heads = min(heads, n_heads)
    tb = min(tb, seq)
    chunk = min(chunk, tb)
    n_sub = chunk // SUB_BLOCK
    assert seq % tb == 0 and tb % chunk == 0 and n_heads % heads == 0
    assert chunk % SUB_BLOCK == 0 and n_sub & (n_sub - 1) == 0
    nt = seq // tb
    n_fac = 3 + 2 * (n_sub.bit_length() - 1)
    blk = pl.BlockSpec((tb, heads * HEAD_DIM), lambda b, h, t: (b * nt + t, h))
    return pl.pallas_call(
        functools.partial(_hgrn_kernel, chunk=chunk, heads=heads),
        grid=(batch, n_heads // heads, nt),
        in_specs=[blk, blk, blk, blk, blk,
                  pl.BlockSpec((1, heads * HEAD_DIM), lambda b, h, t: (0, h))],
        out_specs=blk,
        out_shape=jax.ShapeDtypeStruct((n_rows, d), BF16),
        scratch_shapes=[pltpu.VMEM((heads, HEAD_DIM, HEAD_DIM), F32),
                        pltpu.VMEM((tb // chunk * heads, n_fac, n_sub, HEAD_DIM), F32),
                        pltpu.VMEM((tb // chunk * heads, chunk, HEAD_DIM), F32)],
        compiler_params=pltpu.CompilerParams(
            dimension_semantics=("parallel", "parallel", "arbitrary"),
            vmem_limit_bytes=VMEM_LIMIT_BYTES),
        name="hgrn_mix",
    )(q, w2, k, v, sg, out_norm_w.reshape(1, d))


def _fgate_kernel(h_ref, wf_ref, bias_ref, f_ref, fk_ref, fq_ref, carry_ref):
    @pl.when(pl.program_id(1) == 0)
    def _():
        carry_ref[...] = jnp.zeros_like(carry_ref)

    z = _dot(h_ref[...], wf_ref[...]) + bias_ref[...]
    ls = jnp.minimum(z, 0.0) - jnp.log1p(jnp.exp(-jnp.abs(z)))
    tt = z.shape[0]
    r = lax.broadcasted_iota(jnp.int32, (tt, tt), 0)
    c = lax.broadcasted_iota(jnp.int32, (tt, tt), 1)
    tri = (c <= r).astype(BF16)
    hi, mid, lo = _split3(ls)
    cum = _dot(tri, hi) + _dot(tri, mid) + _dot(tri, lo)
    f = cum + carry_ref[...]
    f_ref[...] = f
    carry_ref[...] = f[tt - 1:tt]
    terms = _split3(f * (-LOG2E))
    for i, term in enumerate(terms):
        fk_ref[:, i * HEAD_DIM:(i + 1) * HEAD_DIM] = term
    lane = lax.broadcasted_iota(jnp.int32, f.shape, 1)
    g = FOX_BIAS_GROUP
    t_hi, t_mid, t_lo = [-t.astype(F32) for t in terms]
    fq = jnp.where(lane < g, t_hi,
                   jnp.where(lane < 2 * g, pltpu.roll(t_mid, g, axis=1),
                             jnp.where(lane < 3 * g, pltpu.roll(t_lo, 2 * g, axis=1),
                                       jnp.where(lane < 3 * g + 3, 1.0, 0.0))))
    fq_ref[...] = fq.astype(BF16)


def _fgate(h, wf, bias, *, batch, seq, tt=256):
    n_rows, d = h.shape
    tt = min(tt, seq)
    nt = seq // tt
    return pl.pallas_call(
        _fgate_kernel,
        grid=(batch, nt),
        in_specs=[pl.BlockSpec((tt, d), lambda b, t: (b * nt + t, 0)),
                  pl.BlockSpec((d, HEAD_DIM), lambda b, t: (0, 0)),
                  pl.BlockSpec((1, HEAD_DIM), lambda b, t: (0, 0))],
        out_specs=[pl.BlockSpec((tt, HEAD_DIM), lambda b, t: (b * nt + t, 0)),
                   pl.BlockSpec((tt, 3 * HEAD_DIM), lambda b, t: (b * nt + t, 0)),
                   pl.BlockSpec((tt, HEAD_DIM), lambda b, t: (b * nt + t, 0))],
        out_shape=[jax.ShapeDtypeStruct((n_rows, HEAD_DIM), F32),
                   jax.ShapeDtypeStruct((n_rows, 3 * HEAD_DIM), BF16),
                   jax.ShapeDtypeStruct((n_rows, HEAD_DIM), BF16)],
        scratch_shapes=[pltpu.VMEM((1, HEAD_DIM), F32)],
        compiler_params=pltpu.CompilerParams(
            dimension_semantics=("parallel", "arbitrary"),
            vmem_limit_bytes=VMEM_LIMIT_BYTES),
        name="fox_forget_gate",
    )(h, wf, bias)


def _fox_scores(qa_ref, kt_ref, s_ref, slot, h, cols):
    s_ref[slot] = _dot(qa_ref[h], kt_ref[h, :, cols])


def _fox_softmax(s_ref, p_ref, m_ref, l_ref, alpha_ref, slot, h, diag, first):
    tq, tk = s_ref.shape[1], s_ref.shape[2]
    for r in range(tq // FOX_ROWS):
        row0 = r * FOX_ROWS
        rows = pl.ds(row0, FOX_ROWS)
        n_chunks = (row0 + FOX_ROWS - 1) // HEAD_DIM + 1 if diag else tk // HEAD_DIM
        chunks = [s_ref[slot, rows, c * HEAD_DIM:(c + 1) * HEAD_DIM] for c in range(n_chunks)]
        if diag:
            last = n_chunks - 1
            rr = lax.broadcasted_iota(jnp.int32, (FOX_ROWS, HEAD_DIM), 0) + row0
            cc = lax.broadcasted_iota(jnp.int32, (FOX_ROWS, HEAD_DIM), 1) + last * HEAD_DIM
            chunks[last] = jnp.where(rr >= cc, chunks[last], -jnp.inf)
        mx = chunks[0]
        for ch in chunks[1:]:
            mx = jnp.maximum(mx, ch)
        m_new = jnp.max(mx, axis=-1, keepdims=True)
        if first:
            m_new = jnp.broadcast_to(m_new, (FOX_ROWS, HEAD_DIM))
        else:
            m_prev = m_ref[h, rows, :]
            m_new = jnp.maximum(m_prev, m_new)
            alpha = jnp.exp2(m_prev - m_new)
        psum = None
        for c, ch in enumerate(chunks):
            part = jnp.exp2(ch - m_new)
            psum = part if psum is None else psum + part
            p_ref[h, rows, c * HEAD_DIM:(c + 1) * HEAD_DIM] = part.astype(BF16)
        if n_chunks * HEAD_DIM < tk:
            p_ref[h, rows, n_chunks * HEAD_DIM:] = jnp.zeros((FOX_ROWS, tk - n_chunks * HEAD_DIM), BF16)
        row_sum = jnp.broadcast_to(jnp.sum(psum, axis=-1, keepdims=True), (FOX_ROWS, HEAD_DIM))
        if first:
            l_ref[h, rows, :] = row_sum
        else:
            l_ref[h, rows, :] = alpha * l_ref[h, rows, :] + row_sum
            alpha_ref[h, rows, :] = alpha
        m_ref[h, rows, :] = m_new


def _fox_values(p_ref, v_ref, alpha_ref, acc_ref, h, cols, first):
    lanes = slice(h * HEAD_DIM, (h + 1) * HEAD_DIM)
    pv = _dot(p_ref[h], v_ref[cols, lanes])
    acc_ref[h] = pv if first else alpha_ref[h] * acc_ref[h] + pv


def _fox_kernel(q_ref, ktr_ref, fk_ref, v_ref, fq_ref, fstart_ref, fend_ref, qkb_ref, sg_ref, onw_ref, o_ref,
                kt_ref, qa_ref, s_ref, p_ref, m_ref, l_ref, alpha_ref, acc_ref, *, heads):
    hp = pl.program_id(1)
    qi = pl.program_id(2)
    tq = q_ref.shape[0]

    @pl.when(qi == 0)
    def _():
        n_sel = 3 * FOX_BIAS_GROUP
        n_bias = fk_ref.shape[1]
        seq = kt_ref.shape[2]
        kt_ref[:, :HEAD_DIM, :] = ktr_ref[...]
        sel_row = lax.broadcasted_iota(jnp.int32, (n_sel, seq), 0) % FOX_BIAS_GROUP
        for h in range(heads):
            kt_ref[h, HEAD_DIM:HEAD_DIM + n_sel, :] = jnp.where(sel_row == hp * heads + h, 1.0, 0.0).astype(BF16)
        kt_ref[:, HEAD_DIM + n_sel:HEAD_DIM + n_sel + n_bias, :] = fk_ref[...]
        kt_ref[:, HEAD_DIM + n_sel + n_bias:, :] = jnp.zeros((heads, HEAD_DIM - n_sel - n_bias, seq), BF16)
    for h in range(heads):
        qa_ref[h, :, :HEAD_DIM] = q_ref[:, h * HEAD_DIM:(h + 1) * HEAD_DIM]
        qa_ref[h, :, HEAD_DIM:] = fq_ref[...]

    def block(kb, diag):
        cols = pl.ds(pl.multiple_of(kb * tq, tq), tq)
        for h in range(heads):
            _fox_scores(qa_ref, kt_ref, s_ref, h, h, cols)
        for h in range(heads):
            _fox_softmax(s_ref, p_ref, m_ref, l_ref, alpha_ref, h, h, diag, diag)
        for h in range(heads):
            _fox_values(p_ref, v_ref, alpha_ref, acc_ref, h, cols, diag)

    block(qi, True)
    lane8 = lax.broadcasted_iota(jnp.int32, fend_ref.shape, 1)
    kb8 = lax.broadcasted_iota(jnp.int32, fend_ref.shape, 0)
    gap = 2.0 * qkb_ref[...] + LOG2E * (fstart_ref[pl.ds(qi, 1), :] - fend_ref[...])
    mine = (lane8 >= hp * heads) & (lane8 < (hp + 1) * heads) & (kb8 < qi)
    live = jnp.where(mine & (gap >= -FOX_SKIP_BITS), 1.0, 0.0)
    n_live = jnp.sum(jnp.max(live, axis=1, keepdims=True)).astype(jnp.int32)

    def body(j, carry):
        block(qi - 1 - j, False)
        return carry

    lax.fori_loop(0, n_live, body, 0)

    for h in range(heads):
        lanes = slice(h * HEAD_DIM, (h + 1) * HEAD_DIM)
        o = acc_ref[h] / l_ref[h]
        ms = jnp.mean(o * o, axis=-1, keepdims=True)
        y = o * lax.rsqrt(ms + EPS) * onw_ref[:, lanes]
        o_ref[:, lanes] = (y * sg_ref[:, lanes].astype(F32)).astype(o_ref.dtype)


def _fox_mix(q, kt, fk_rows, v, fcol, fq_cols, qk_bound, sg, out_norm_w, *, batch, seq):
    n_rows, d = q.shape
    n_heads = d // HEAD_DIM
    heads = min(FOX_HEADS, n_heads)
    tq = min(FOX_TQ, seq)
    assert seq % tq == 0 and tq % FOX_ROWS == 0 and tq % HEAD_DIM == 0 and n_heads % heads == 0
    assert n_heads <= FOX_BIAS_GROUP
    nq = seq // tq
    n_pairs = n_heads // heads
    qblk = pl.BlockSpec((tq, heads * HEAD_DIM), lambda b, h, i: (b * nq + i, h))
    fblocks = fcol.reshape(batch, nq, tq, HEAD_DIM)
    fstart, fend = fblocks[:, :, 0, :], fblocks[:, :, tq - 1, :]
    fedge = pl.BlockSpec((None, nq, HEAD_DIM), lambda b, h, i: (b, 0, 0))
    qkb_row = jnp.full((1, HEAD_DIM), qk_bound, F32)
    return pl.pallas_call(
        functools.partial(_fox_kernel, heads=heads),
        grid=(batch, n_pairs, nq),
        in_specs=[qblk,
                  pl.BlockSpec((heads, HEAD_DIM, seq), lambda b, h, i: (b * n_pairs + h, 0, 0)),
                  pl.BlockSpec((heads, fk_rows.shape[1], seq), lambda b, h, i: (b * n_pairs + h, 0, 0)),
                  pl.BlockSpec((seq, heads * HEAD_DIM), lambda b, h, i: (b, h)),
                  pl.BlockSpec((tq, HEAD_DIM), lambda b, h, i: (b * nq + i, 0)),
                  fedge, fedge,
                  pl.BlockSpec((1, HEAD_DIM), lambda b, h, i: (0, 0)),
                  qblk,
                  pl.BlockSpec((1, heads * HEAD_DIM), lambda b, h, i: (0, h))],
        out_specs=qblk,
        out_shape=jax.ShapeDtypeStruct((n_rows, d), BF16),
        scratch_shapes=[pltpu.VMEM((heads, 2 * HEAD_DIM, seq), BF16),
                        pltpu.VMEM((heads, tq, 2 * HEAD_DIM), BF16),
                        pltpu.VMEM((heads, tq, tq), F32),
                        pltpu.VMEM((heads, tq, tq), BF16),
                        pltpu.VMEM((heads, tq, HEAD_DIM), F32),
                        pltpu.VMEM((heads, tq, HEAD_DIM), F32),
                        pltpu.VMEM((heads, tq, HEAD_DIM), F32),
                        pltpu.VMEM((heads, tq, HEAD_DIM), F32)],
        compiler_params=pltpu.CompilerParams(
            dimension_semantics=("parallel", "parallel", "arbitrary"),
            vmem_limit_bytes=VMEM_LIMIT_BYTES),
        name="fox_attention",
    )(q, kt, fk_rows, v, fq_cols, fstart, fend, qkb_row, sg, out_norm_w.reshape(1, d))


def _fox_bias_rows(fk3, *, batch, seq, n_heads):
    rows = fk3.reshape(batch, seq, 3, HEAD_DIM)[:, :, :, :n_heads].transpose(0, 3, 2, 1)
    rows = jnp.pad(rows, ((0, 0), (0, 0), (0, BF16_SUBLANES - 3), (0, 0)))
    return rows.reshape(batch * n_heads, BF16_SUBLANES, seq)


def kernel(x, a_norm_w, a_w_in, a_lb_logits, a_out_norm_w, a_w_out, kv_norm_w, kv_w, kv_f_bias, k_norm_w,
           b_norm_w, b_w_in, b_q_norm_w, b_out_norm_w, b_w_out):
    batch, seq, d = x.shape
    n_heads = d // HEAD_DIM
    n_a = a_w_in.shape[0]
    n_b = b_w_in.shape[0]
    xr = x.reshape(batch * seq, d)

    a_w_in_b = a_w_in.astype(BF16)
    a_w_out_b = a_w_out.astype(BF16)
    b_w_in_b = b_w_in.astype(BF16)
    b_w_out_b = b_w_out.astype(BF16)
    kv_w_b = kv_w[:, :2 * d].astype(BF16)[None]
    wf_b = jnp.pad(kv_w[:, 2 * d:], ((0, 0), (0, HEAD_DIM - n_heads))).astype(BF16)
    f_bias = jnp.pad(kv_f_bias.astype(F32), (0, HEAD_DIM - n_heads)).reshape(1, HEAD_DIM)

    lb_all = jnp.cumsum(jax.nn.softmax(a_lb_logits.astype(F32), axis=0), axis=0)
    lb_all = lb_all - lb_all[0:1]
    ones_row = jnp.ones((1, d), F32)

    h = _norm(xr, a_norm_w[0])
    for layer in range(n_a):
        q, w2, k, v, sg = _proj(
            h,
            [(a_w_in_b, layer, 0, ones_row, "cast", (BF16,)),
             (a_w_in_b, layer, d, lb_all[layer].reshape(1, d), "hgate", (F32, BF16)),
             (a_w_in_b, layer, 2 * d, ones_row, "cast", (BF16,)),
             (a_w_in_b, layer, 3 * d, ones_row, "silu", (BF16,))],
            name="hgrn_in_proj", seq=seq, tm=1024, tn=256)
        og = _hgrn_mix(q, w2, k, v, sg, a_out_norm_w[layer], batch=batch, seq=seq)
        if layer + 1 < n_a:
            xr, (h,) = _out_proj(og, a_w_out_b, layer, xr, [a_norm_w[layer + 1]])
        else:
            xr, (h_kv, h) = _out_proj(og, a_w_out_b, layer, xr, [kv_norm_w, b_norm_w[0]])

    k_norm_row = jnp.tile(k_norm_w.astype(F32), n_heads).reshape(1, d)
    kt, vv = _proj(
        h_kv,
        [(kv_w_b, 0, 0, k_norm_row, "headnorm_t", (BF16,)),
         (kv_w_b, 0, d, ones_row, "cast", (BF16,))],
        name="fox_kv_proj", seq=seq)
    fcol, fk3, fq_cols = _fgate(h_kv, wf_b, f_bias, batch=batch, seq=seq)
    fk_rows = _fox_bias_rows(fk3, batch=batch, seq=seq, n_heads=n_heads)

    for j in range(n_b):
        q_norm_row = jnp.tile(b_q_norm_w[j].astype(F32), n_heads).reshape(1, d)
        q, sg = _proj(
            h,
            [(b_w_in_b, j, 0, q_norm_row, "headnorm", (BF16,)),
             (b_w_in_b, j, d, ones_row, "silu", (BF16,))],
            scale=HEAD_DIM ** -0.5 * LOG2E, name="fox_in_proj", seq=seq)
        qk_bound = (HEAD_DIM * jnp.max(jnp.abs(b_q_norm_w[j])) * jnp.max(jnp.abs(k_norm_w))
                    * (HEAD_DIM ** -0.5 * LOG2E * 1.02))
        og = _fox_mix(q, kt, fk_rows, vv, fcol, fq_cols, qk_bound, sg, b_out_norm_w[j], batch=batch, seq=seq)
        if j + 1 < n_b:
            xr, (h,) = _out_proj(og, b_w_out_b, j, xr, [b_norm_w[j + 1]])
        else:
            xr, _ = _out_proj(og, b_w_out_b, j, xr, [])

    return xr.reshape(batch, seq, d)
```

```python
import functools

import jax
import jax.numpy as jnp
from jax import lax
from jax.experimental import pallas as pl
from jax.experimental.pallas import tpu as pltpu

HEAD_DIM = 128
VREG_ROWS = 8
SUB_BLOCK = 8
BF16_SUBLANES = 16
EPS = 1e-6
VMEM_LIMIT_BYTES = 56 * 1024 * 1024
LOG2E = 1.4426950408889634
FOX_TQ = 512
FOX_ROWS = 64
FOX_HEADS = 4
FOX_BIAS_GROUP = 32
FOX_SKIP_BITS = 152.0

F32 = jnp.float32
BF16 = jnp.bfloat16


def _dot(a, b):
    return jnp.dot(a, b, preferred_element_type=F32)


def _dot_nt(a, b):
    return lax.dot_general(a, b, (((1,), (1,)), ((), ())), preferred_element_type=F32)


def _dot_tn(a, b):
    return lax.dot_general(a, b, (((0,), (0,)), ((), ())), preferred_element_type=F32)


def _split3(x):
    hi = x.astype(BF16)
    r1 = x - hi.astype(F32)
    mid = r1.astype(BF16)
    lo = (r1 - mid.astype(F32)).astype(BF16)
    return hi, mid, lo


def _scan_sub_block(x):
    n, c = x.shape
    x3 = x.reshape(n // VREG_ROWS, VREG_ROWS, c)
    pos = lax.broadcasted_iota(jnp.int32, x3.shape, 1)
    shift = 1
    while shift < VREG_ROWS:
        x3 = x3 + jnp.where(pos >= shift, pltpu.roll(x3, shift, axis=1), 0.0)
        shift *= 2
    if SUB_BLOCK == 2 * VREG_ROWS:
        tile = lax.broadcasted_iota(jnp.int32, x3.shape, 0)
        carry = jnp.roll(jnp.broadcast_to(x3[:, VREG_ROWS - 1:, :], x3.shape), 1, axis=0)
        x3 = x3 + jnp.where(tile % 2 == 1, carry, 0.0)
    return x3.reshape(n, c)


def _rms_rows(x, w):
    ms = jnp.mean(x * x, axis=-1, keepdims=True)
    return x * lax.rsqrt(ms + EPS) * w


def _proj_kernel(*refs, kinds, scale):
    n = len(kinds)
    h_ref = refs[0]
    w_refs = refs[1:1 + n]
    aux_refs = refs[1 + n:1 + 2 * n]
    out_refs = refs[1 + 2 * n:-1]
    wb_ref = refs[-1]

    @pl.when(pl.program_id(1) == 0)
    def _():
        for s in range(n):
            wb_ref[s] = w_refs[s][...].astype(BF16)

    h = h_ref[...]
    oi = 0
    for s, kind in enumerate(kinds):
        acc = _dot(h, wb_ref[s])
        aux = aux_refs[s][...]
        if kind == "cast":
            out_refs[oi][...] = acc.astype(out_refs[oi].dtype)
            oi += 1
        elif kind == "silu":
            out_refs[oi][...] = (acc / (1.0 + jnp.exp(-acc))).astype(out_refs[oi].dtype)
            oi += 1
        elif kind in ("headnorm", "headnorm_t"):
            tn = acc.shape[1]
            for c in range(tn // HEAD_DIM):
                sl = slice(c * HEAD_DIM, (c + 1) * HEAD_DIM)
                a = acc[:, sl]
                ms = jnp.mean(a * a, axis=-1, keepdims=True)
                y = a * lax.rsqrt(ms + EPS) * aux[:, sl]
                if scale != 1.0:
                    y = y * scale
                if kind == "headnorm_t":
                    out_refs[oi][c] = y.T.astype(out_refs[oi].dtype)
                else:
                    out_refs[oi][:, sl] = y.astype(out_refs[oi].dtype)
            oi += 1
        elif kind == "hgate":
            lb = aux
            e = jnp.exp2(jnp.abs(acc) * (-LOG2E))
            one_e = 1.0 + e
            log2_sig = jnp.minimum(acc, 0.0) * LOG2E - jnp.log2(one_e)
            a = jnp.log2(lb)
            c = jnp.log2(1.0 - lb) + log2_sig
            g2 = jnp.maximum(a, c) + jnp.log2(1.0 + jnp.exp2(-jnp.abs(a - c)))
            r = 1.0 / one_e
            sig_neg = jnp.where(acc >= 0.0, e * r, r)
            out_refs[oi][...] = _scan_sub_block(g2).astype(out_refs[oi].dtype)
            out_refs[oi + 1][...] = ((1.0 - lb) * sig_neg).astype(out_refs[oi + 1].dtype)
            oi += 2
        else:
            raise ValueError(kind)


def _proj(h, streams, *, name, seq, scale=1.0, tm=1024, tn=512):
    n_rows, d = h.shape
    n_cols = streams[0][3].shape[1]
    tm = min(tm, seq)
    tn = min(tn, n_cols)
    assert seq % tm == 0 and n_rows % seq == 0 and n_cols % tn == 0 and tm % SUB_BLOCK == 0
    kinds = tuple(s[4] for s in streams)
    t_tiles = seq // tm
    heads_per_tile = tn // HEAD_DIM
    n_head_tiles = n_cols // tn

    in_specs = [pl.BlockSpec((tm, d), lambda j, i: (i, 0))]
    args = [h]
    for (w, layer, off, aux, kind, _) in streams:
        assert off % tn == 0
        in_specs.append(pl.BlockSpec((None, d, tn),
                                     functools.partial(lambda j, i, l, o: (l, 0, j + o), l=layer, o=off // tn)))
        args.append(w)
    for (w, layer, off, aux, kind, _) in streams:
        in_specs.append(pl.BlockSpec((1, tn), lambda j, i: (0, j)))
        args.append(aux)
    out_shapes, out_specs = [], []
    for (w, layer, off, aux, kind, dts) in streams:
        for dt in dts:
            if kind == "headnorm_t":
                out_shapes.append(jax.ShapeDtypeStruct((n_rows // seq * n_cols // HEAD_DIM, HEAD_DIM, seq), dt))
                out_specs.append(pl.BlockSpec(
                    (heads_per_tile, HEAD_DIM, tm),
                    lambda j, i: ((i // t_tiles) * n_head_tiles + j, 0, i % t_tiles)))
            else:
                out_shapes.append(jax.ShapeDtypeStruct((n_rows, n_cols), dt))
                out_specs.append(pl.BlockSpec((tm, tn), lambda j, i: (i, j)))

    return pl.pallas_call(
        functools.partial(_proj_kernel, kinds=kinds, scale=scale),
        grid=(n_cols // tn, n_rows // tm),
        in_specs=in_specs,
        out_specs=out_specs,
        out_shape=out_shapes,
        scratch_shapes=[pltpu.VMEM((len(streams), d, tn), BF16)],
        compiler_params=pltpu.CompilerParams(
            dimension_semantics=("parallel", "arbitrary"),
            vmem_limit_bytes=VMEM_LIMIT_BYTES),
        name=name,
    )(*args)


def _norm_kernel(x_ref, nw_ref, h_ref):
    h_ref[...] = _rms_rows(x_ref[...], nw_ref[...]).astype(h_ref.dtype)


def _norm(x, norm_w, *, tm=1024):
    n_rows, d = x.shape
    tm = min(tm, n_rows)
    assert n_rows % tm == 0
    return pl.pallas_call(
        _norm_kernel,
        grid=(n_rows // tm,),
        in_specs=[pl.BlockSpec((tm, d), lambda i: (i, 0)),
                  pl.BlockSpec((1, d), lambda i: (0, 0))],
        out_specs=pl.BlockSpec((tm, d), lambda i: (i, 0)),
        out_shape=jax.ShapeDtypeStruct((n_rows, d), BF16),
        compiler_params=pltpu.CompilerParams(
            dimension_semantics=("parallel",),
            vmem_limit_bytes=VMEM_LIMIT_BYTES),
        name="input_norm",
    )(x, norm_w.reshape(1, d))


def _out_proj_kernel(o_ref, w_ref, x_ref, *refs):
    n_next = (len(refs) - 1) // 2
    nw_refs, y_ref, h_refs = refs[:n_next], refs[n_next], refs[n_next + 1:]
    y = x_ref[...] + _dot(o_ref[...], w_ref[...])
    y_ref[...] = y
    if n_next:
        yn = y * lax.rsqrt(jnp.mean(y * y, axis=-1, keepdims=True) + EPS)
        for nw_ref, h_ref in zip(nw_refs, h_refs):
            h_ref[...] = (yn * nw_ref[...]).astype(h_ref.dtype)


def _out_proj(o, w3d, layer, x, next_norm_ws, *, tm=512):
    n_rows, d = x.shape
    tm = min(tm, n_rows)
    assert n_rows % tm == 0
    rows = pl.BlockSpec((tm, d), lambda i: (i, 0))
    gain = pl.BlockSpec((1, d), lambda i: (0, 0))
    n_next = len(next_norm_ws)
    outs = pl.pallas_call(
        _out_proj_kernel,
        grid=(n_rows // tm,),
        in_specs=[rows, pl.BlockSpec((None, d, d), functools.partial(lambda i, l: (l, 0, 0), l=layer)), rows]
                 + [gain] * n_next,
        out_specs=[rows] * (1 + n_next),
        out_shape=[jax.ShapeDtypeStruct((n_rows, d), F32)]
                  + [jax.ShapeDtypeStruct((n_rows, d), BF16)] * n_next,
        compiler_params=pltpu.CompilerParams(
            dimension_semantics=("parallel",),
            vmem_limit_bytes=VMEM_LIMIT_BYTES),
        name="out_proj",
    )(o, w3d, x, *[w.reshape(1, d) for w in next_norm_ws])
    return outs[0], list(outs[1:])


def _excl_prefix_rows(x):
    pos = lax.broadcasted_iota(jnp.int32, x.shape, 0)
    inc = x
    shift = 1
    while shift < x.shape[0]:
        inc = inc + jnp.where(pos >= shift, pltpu.roll(inc, shift, axis=0), 0.0)
        shift *= 2
    return inc - x


def _level_mid(bs2, lvl):
    n = bs2.shape[0]
    group = 2 << lvl
    if group >= VREG_ROWS:
        tiles = [jnp.broadcast_to(bs2[(t0 // group) * group + group // 2:][:1], (VREG_ROWS, HEAD_DIM))
                 for t0 in range(0, n, VREG_ROWS)]
        return jnp.concatenate(tiles, axis=0)
    x3 = bs2.reshape(n // VREG_ROWS, VREG_ROWS, HEAD_DIM)
    pos = lax.broadcasted_iota(jnp.int32, x3.shape, 1)
    out = None
    for g0 in range(0, VREG_ROWS, group):
        piece = jnp.broadcast_to(x3[:, g0 + group // 2:g0 + group // 2 + 1, :], x3.shape)
        out = piece if out is None else jnp.where(pos >= g0, piece, out)
    return out.reshape(n, HEAD_DIM)


def _rows_of(x, idx):
    return jnp.concatenate([x[i:i + 1] for i in idx], axis=0)


def _hgrn_chunk(q, k, w2, v, st, fac_ref, ck_ref, consts):
    lane8, sub_row, lvl_masks, same_blk, expand_r, jrow = consts
    c = q.shape[0]
    n_sub = c // SUB_BLOCK
    n_lvl = n_sub.bit_length() - 1

    gt2 = _rows_of(w2, [j * SUB_BLOCK + SUB_BLOCK - 1 for j in range(n_sub)])
    bs2 = _excl_prefix_rows(gt2)
    be2 = bs2 + gt2
    tot2 = be2[n_sub - 1:n_sub]
    fac_ref[0] = gt2
    fac_ref[1] = jnp.exp2(bs2)
    fac_ref[2] = jnp.exp2(tot2 - be2)
    for lvl in range(n_lvl):
        mid = _level_mid(bs2, lvl)
        upper = ((jrow >> lvl) & 1) == 1
        fac_ref[3 + 2 * lvl] = jnp.exp2(jnp.where(upper, bs2 - mid, -jnp.inf))
        fac_ref[4 + 2 * lvl] = jnp.exp2(jnp.where(upper, -jnp.inf, mid - be2))
    ck_ref[...] = jnp.log2(k) - w2

    qs_parts, kbar_parts, a_parts = [], [], []
    ql_parts = [[] for _ in range(n_lvl)]
    kl_parts = [[] for _ in range(n_lvl)]
    for j in range(n_sub):
        r0 = j * SUB_BLOCK
        sl = slice(r0, r0 + SUB_BLOCK)
        w_j, q_j = w2[sl], q[sl]
        qt = q_j * jnp.exp2(w_j)
        kh = k[sl] * jnp.exp2(fac_ref[0, j:j + 1, :] - w_j)
        qs_parts.append(qt * fac_ref[1, j:j + 1, :])
        kbar_parts.append(kh * fac_ref[2, j:j + 1, :])
        for lvl in range(n_lvl):
            ql_parts[lvl].append(qt * fac_ref[3 + 2 * lvl, j:j + 1, :])
            kl_parts[lvl].append(kh * fac_ref[4 + 2 * lvl, j:j + 1, :])

        for t0 in range(0, SUB_BLOCK, VREG_ROWS):
            w_t, q_t = w_j[t0:t0 + VREG_ROWS], q_j[t0:t0 + VREG_ROWS]
            a_t = jnp.zeros((VREG_ROWS, HEAD_DIM), F32)
            for s in range(t0 + VREG_ROWS):
                cs = ck_ref[r0 + s:r0 + s + 1, :]
                col = jnp.sum(q_t * jnp.exp2(w_t + cs), axis=-1, keepdims=True)
                a_t = jnp.where(lane8 == s, col, a_t)
            a_parts.append(jnp.where(lane8 <= sub_row + t0, a_t, 0.0))

    cat = lambda parts: jnp.concatenate(parts, axis=0)
    o = _dot_nt(cat(qs_parts).astype(BF16), st.astype(BF16))
    a = _dot_nt(cat(ql_parts[n_lvl - 1]).astype(BF16), cat(kl_parts[n_lvl - 1]).astype(BF16))
    for lvl in range(n_lvl - 2, -1, -1):
        a_l = _dot_nt(cat(ql_parts[lvl]).astype(BF16), cat(kl_parts[lvl]).astype(BF16))
        a = jnp.where(lvl_masks[lvl], a_l, a)
    a_diag = _dot(cat(a_parts).astype(BF16), expand_r)
    a = jnp.where(same_blk, a_diag, a)
    o = o + _dot(a.astype(BF16), v)
    st_new = st * jnp.exp2(tot2) + _dot_tn(v, cat(kbar_parts).astype(BF16))
    return o, st_new


def _hgrn_kernel(q_ref, w2_ref, k_ref, v_ref, sg_ref, onw_ref, o_ref, st_ref, fac_ref, ck_ref, *, chunk, heads):
    tb = q_ref.shape[0]
    n_chunks = tb // chunk
    n_sub = chunk // SUB_BLOCK
    n_lvl = n_sub.bit_length() - 1

    @pl.when(pl.program_id(2) == 0)
    def _():
        st_ref[...] = jnp.zeros_like(st_ref)

    r = lax.broadcasted_iota(jnp.int32, (chunk, chunk), 0) // SUB_BLOCK
    cidx = lax.broadcasted_iota(jnp.int32, (chunk, chunk), 1)
    lane_blk = cidx // SUB_BLOCK
    same_blk = r == lane_blk
    lvl_masks = [(r >> (lvl + 1)) == (lane_blk >> (lvl + 1)) for lvl in range(n_lvl)]
    lane8 = lax.broadcasted_iota(jnp.int32, (VREG_ROWS, HEAD_DIM), 1)
    sub_row = lax.broadcasted_iota(jnp.int32, (VREG_ROWS, HEAD_DIM), 0)
    er = lax.broadcasted_iota(jnp.int32, (HEAD_DIM, chunk), 0)
    ec = lax.broadcasted_iota(jnp.int32, (HEAD_DIM, chunk), 1)
    expand_r = (ec % SUB_BLOCK == er).astype(BF16)
    jrow = lax.broadcasted_iota(jnp.int32, (n_sub, HEAD_DIM), 0)
    consts = (lane8, sub_row, lvl_masks, same_blk, expand_r, jrow)

    for ci in range(n_chunks):
        rows = slice(ci * chunk, (ci + 1) * chunk)
        for h in range(heads):
            lanes = slice(h * HEAD_DIM, (h + 1) * HEAD_DIM)
            q = q_ref[rows, lanes].astype(F32)
            k = k_ref[rows, lanes].astype(F32)
            o, st_new = _hgrn_chunk(q, k, w2_ref[rows, lanes], v_ref[rows, lanes], st_ref[h],
                                    fac_ref.at[ci * heads + h], ck_ref.at[ci * heads + h], consts)
            st_ref[h] = st_new
            ms = jnp.mean(o * o, axis=-1, keepdims=True)
            y = o * lax.rsqrt(ms + EPS) * onw_ref[:, lanes]
            o_ref[rows, lanes] = (y * sg_ref[rows, lanes].astype(F32)).astype(o_ref.dtype)


def _hgrn_mix(q, w2, k, v, sg, out_norm_w, *, batch, seq, chunk=128, heads=4, tb=512):
    n_rows, d = q.shape
    n_heads = d // HEAD_DIM
    heads = min(heads, n_heads)
    tb = min(tb, seq)
    chunk = min(chunk, tb)
    n_sub = chunk // SUB_BLOCK
    assert seq % tb == 0 and tb % chunk == 0 and n_heads % heads == 0
    assert chunk % SUB_BLOCK == 0 and n_sub & (n_sub - 1) == 0
    nt = seq // tb
    n_fac = 3 + 2 * (n_sub.bit_length() - 1)
    blk = pl.BlockSpec((tb, heads * HEAD_DIM), lambda b, h, t: (b * nt + t, h))
    return pl.pallas_call(
        functools.partial(_hgrn_kernel, chunk=chunk, heads=heads),
        grid=(batch, n_heads // heads, nt),
        in_specs=[blk, blk, blk, blk, blk,
                  pl.BlockSpec((1, heads * HEAD_DIM), lambda b, h, t: (0, h))],
        out_specs=blk,
        out_shape=jax.ShapeDtypeStruct((n_rows, d), BF16),
        scratch_shapes=[pltpu.VMEM((heads, HEAD_DIM, HEAD_DIM), F32),
                        pltpu.VMEM((tb // chunk * heads, n_fac, n_sub, HEAD_DIM), F32),
                        pltpu.VMEM((tb // chunk * heads, chunk, HEAD_DIM), F32)],
        compiler_params=pltpu.CompilerParams(
            dimension_semantics=("parallel", "parallel", "arbitrary"),
            vmem_limit_bytes=VMEM_LIMIT_BYTES),
        name="hgrn_mix",
    )(q, w2, k, v, sg, out_norm_w.reshape(1, d))


def _fgate_kernel(h_ref, wf_ref, bias_ref, f_ref, fk_ref, fq_ref, carry_ref):
    @pl.when(pl.program_id(1) == 0)
    def _():
        carry_ref[...] = jnp.zeros_like(carry_ref)

    z = _dot(h_ref[...], wf_ref[...]) + bias_ref[...]
    ls = jnp.minimum(z, 0.0) - jnp.log1p(jnp.exp(-jnp.abs(z)))
    tt = z.shape[0]
    r = lax.broadcasted_iota(jnp.int32, (tt, tt), 0)
    c = lax.broadcasted_iota(jnp.int32, (tt, tt), 1)
    tri = (c <= r).astype(BF16)
    hi, mid, lo = _split3(ls)
    cum = _dot(tri, hi) + _dot(tri, mid) + _dot(tri, lo)
    f = cum + carry_ref[...]
    f_ref[...] = f
    carry_ref[...] = f[tt - 1:tt]
    terms = _split3(f * (-LOG2E))
    for i, term in enumerate(terms):
        fk_ref[:, i * HEAD_DIM:(i + 1) * HEAD_DIM] = term
    lane = lax.broadcasted_iota(jnp.int32, f.shape, 1)
    g = FOX_BIAS_GROUP
    t_hi, t_mid, t_lo = [-t.astype(F32) for t in terms]
    fq = jnp.where(lane < g, t_hi,
                   jnp.where(lane < 2 * g, pltpu.roll(t_mid, g, axis=1),
                             jnp.where(lane < 3 * g, pltpu.roll(t_lo, 2 * g, axis=1),
                                       jnp.where(lane < 3 * g + 3, 1.0, 0.0))))
    fq_ref[...] = fq.astype(BF16)


def _fgate(h, wf, bias, *, batch, seq, tt=256):
    n_rows, d = h.shape
    tt = min(tt, seq)
    nt = seq // tt
    return pl.pallas_call(
        _fgate_kernel,
        grid=(batch, nt),
        in_specs=[pl.BlockSpec((tt, d), lambda b, t: (b * nt + t, 0)),
                  pl.BlockSpec((d, HEAD_DIM), lambda b, t: (0, 0)),
                  pl.BlockSpec((1, HEAD_DIM), lambda b, t: (0, 0))],
        out_specs=[pl.BlockSpec((tt, HEAD_DIM), lambda b, t: (b * nt + t, 0)),
                   pl.BlockSpec((tt, 3 * HEAD_DIM), lambda b, t: (b * nt + t, 0)),
                   pl.BlockSpec((tt, HEAD_DIM), lambda b, t: (b * nt + t, 0))],
        out_shape=[jax.ShapeDtypeStruct((n_rows, HEAD_DIM), F32),
                   jax.ShapeDtypeStruct((n_rows, 3 * HEAD_DIM), BF16),
                   jax.ShapeDtypeStruct((n_rows, HEAD_DIM), BF16)],
        scratch_shapes=[pltpu.VMEM((1, HEAD_DIM), F32)],
        compiler_params=pltpu.CompilerParams(
            dimension_semantics=("parallel", "arbitrary"),
            vmem_limit_bytes=VMEM_LIMIT_BYTES),
        name="fox_forget_gate",
    )(h, wf, bias)


def _fox_scores(qa_ref, kt_ref, s_ref, slot, h, cols):
    s_ref[slot] = _dot(qa_ref[h], kt_ref[h, :, cols])


def _fox_softmax(s_ref, p_ref, m_ref, l_ref, alpha_ref, slot, h, diag, first):
    tq, tk = s_ref.shape[1], s_ref.shape[2]
    for r in range(tq // FOX_ROWS):
        row0 = r * FOX_ROWS
        rows = pl.ds(row0, FOX_ROWS)
        n_chunks = (row0 + FOX_ROWS - 1) // HEAD_DIM + 1 if diag else tk // HEAD_DIM
        chunks = [s_ref[slot, rows, c * HEAD_DIM:(c + 1) * HEAD_DIM] for c in range(n_chunks)]
        if diag:
            last = n_chunks - 1
            rr = lax.broadcasted_iota(jnp.int32, (FOX_ROWS, HEAD_DIM), 0) + row0
            cc = lax.broadcasted_iota(jnp.int32, (FOX_ROWS, HEAD_DIM), 1) + last * HEAD_DIM
            chunks[last] = jnp.where(rr >= cc, chunks[last], -jnp.inf)
        mx = chunks[0]
        for ch in chunks[1:]:
            mx = jnp.maximum(mx, ch)
        m_new = jnp.max(mx, axis=-1, keepdims=True)
        if first:
            m_new = jnp.broadcast_to(m_new, (FOX_ROWS, HEAD_DIM))
        else:
            m_prev = m_ref[h, rows, :]
            m_new = jnp.maximum(m_prev, m_new)
            alpha = jnp.exp2(m_prev - m_new)
        psum = None
        for c, ch in enumerate(chunks):
            part = jnp.exp2(ch - m_new)
            psum = part if psum is None else psum + part
            p_ref[h, rows, c * HEAD_DIM:(c + 1) * HEAD_DIM] = part.astype(BF16)
        if n_chunks * HEAD_DIM < tk:
            p_ref[h, rows, n_chunks * HEAD_DIM:] = jnp.zeros((FOX_ROWS, tk - n_chunks * HEAD_DIM), BF16)
        row_sum = jnp.broadcast_to(jnp.sum(psum, axis=-1, keepdims=True), (FOX_ROWS, HEAD_DIM))
        if first:
            l_ref[h, rows, :] = row_sum
        else:
            l_ref[h, rows, :] = alpha * l_ref[h, rows, :] + row_sum
            alpha_ref[h, rows, :] = alpha
        m_ref[h, rows, :] = m_new


def _fox_values(p_ref, v_ref, alpha_ref, acc_ref, h, cols, first):
    lanes = slice(h * HEAD_DIM, (h + 1) * HEAD_DIM)
    pv = _dot(p_ref[h], v_ref[cols, lanes])
    acc_ref[h] = pv if first else alpha_ref[h] * acc_ref[h] + pv


def _fox_kernel(q_ref, ktr_ref, fk_ref, v_ref, fq_ref, fstart_ref, fend_ref, qkb_ref, sg_ref, onw_ref, o_ref,
                kt_ref, qa_ref, s_ref, p_ref, m_ref, l_ref, alpha_ref, acc_ref, *, heads):
    hp = pl.program_id(1)
    qi = pl.program_id(2)
    tq = q_ref.shape[0]

    @pl.when(qi == 0)
    def _():
        n_sel = 3 * FOX_BIAS_GROUP
        n_bias = fk_ref.shape[1]
        seq = kt_ref.shape[2]
        kt_ref[:, :HEAD_DIM, :] = ktr_ref[...]
        sel_row = lax.broadcasted_iota(jnp.int32, (n_sel, seq), 0) % FOX_BIAS_GROUP
        for h in range(heads):
            kt_ref[h, HEAD_DIM:HEAD_DIM + n_sel, :] = jnp.where(sel_row == hp * heads + h, 1.0, 0.0).astype(BF16)
        kt_ref[:, HEAD_DIM + n_sel:HEAD_DIM + n_sel + n_bias, :] = fk_ref[...]
        kt_ref[:, HEAD_DIM + n_sel + n_bias:, :] = jnp.zeros((heads, HEAD_DIM - n_sel - n_bias, seq), BF16)
    for h in range(heads):
        qa_ref[h, :, :HEAD_DIM] = q_ref[:, h * HEAD_DIM:(h + 1) * HEAD_DIM]
        qa_ref[h, :, HEAD_DIM:] = fq_ref[...]

    def block(kb, diag):
        cols = pl.ds(pl.multiple_of(kb * tq, tq), tq)
        for h in range(heads):
            _fox_scores(qa_ref, kt_ref, s_ref, h, h, cols)
        for h in range(heads):
            _fox_softmax(s_ref, p_ref, m_ref, l_ref, alpha_ref, h, h, diag, diag)
        for h in range(heads):
            _fox_values(p_ref, v_ref, alpha_ref, acc_ref, h, cols, diag)

    block(qi, True)
    lane8 = lax.broadcasted_iota(jnp.int32, fend_ref.shape, 1)
    kb8 = lax.broadcasted_iota(jnp.int32, fend_ref.shape, 0)
    gap = 2.0 * qkb_ref[...] + LOG2E * (fstart_ref[pl.ds(qi, 1), :] - fend_ref[...])
    mine = (lane8 >= hp * heads) & (lane8 < (hp + 1) * heads) & (kb8 < qi)
    live = jnp.where(mine & (gap >= -FOX_SKIP_BITS), 1.0, 0.0)
    n_live = jnp.sum(jnp.max(live, axis=1, keepdims=True)).astype(jnp.int32)

    def body(j, carry):
        block(qi - 1 - j, False)
        return carry

    lax.fori_loop(0, n_live, body, 0)

    for h in range(heads):
        lanes = slice(h * HEAD_DIM, (h + 1) * HEAD_DIM)
        o = acc_ref[h] / l_ref[h]
        ms = jnp.mean(o * o, axis=-1, keepdims=True)
        y = o * lax.rsqrt(ms + EPS) * onw_ref[:, lanes]
        o_ref[:, lanes] = (y * sg_ref[:, lanes].astype(F32)).astype(o_ref.dtype)


def _fox_mix(q, kt, fk_rows, v, fcol, fq_cols, qk_bound, sg, out_norm_w, *, batch, seq):
    n_rows, d = q.shape
    n_heads = d // HEAD_DIM
    heads = min(FOX_HEADS, n_heads)
    tq = min(FOX_TQ, seq)
    assert seq % tq == 0 and tq % FOX_ROWS == 0 and tq % HEAD_DIM == 0 and n_heads % heads == 0
    assert n_heads <= FOX_BIAS_GROUP
    nq = seq // tq
    n_pairs = n_heads // heads
    qblk = pl.BlockSpec((tq, heads * HEAD_DIM), lambda b, h, i: (b * nq + i, h))
    fblocks = fcol.reshape(batch, nq, tq, HEAD_DIM)
    fstart, fend = fblocks[:, :, 0, :], fblocks[:, :, tq - 1, :]
    fedge = pl.BlockSpec((None, nq, HEAD_DIM), lambda b, h, i: (b, 0, 0))
    qkb_row = jnp.full((1, HEAD_DIM), qk_bound, F32)
    return pl.pallas_call(
        functools.partial(_fox_kernel, heads=heads),
        grid=(batch, n_pairs, nq),
        in_specs=[qblk,
                  pl.BlockSpec((heads, HEAD_DIM, seq), lambda b, h, i: (b * n_pairs + h, 0, 0)),
                  pl.BlockSpec((heads, fk_rows.shape[1], seq), lambda b, h, i: (b * n_pairs + h, 0, 0)),
                  pl.BlockSpec((seq, heads * HEAD_DIM), lambda b, h, i: (b, h)),
                  pl.BlockSpec((tq, HEAD_DIM), lambda b, h, i: (b * nq + i, 0)),
                  fedge, fedge,
                  pl.BlockSpec((1, HEAD_DIM), lambda b, h, i: (0, 0)),
                  qblk,
                  pl.BlockSpec((1, heads * HEAD_DIM), lambda b, h, i: (0, h))],
        out_specs=qblk,
        out_shape=jax.ShapeDtypeStruct((n_rows, d), BF16),
        scratch_shapes=[pltpu.VMEM((heads, 2 * HEAD_DIM, seq), BF16),
                        pltpu.VMEM((heads, tq, 2 * HEAD_DIM), BF16),
                        pltpu.VMEM((heads, tq, tq), F32),
                        pltpu.VMEM((heads, tq, tq), BF16),
                        pltpu.VMEM((heads, tq, HEAD_DIM), F32),
                        pltpu.VMEM((heads, tq, HEAD_DIM), F32),
                        pltpu.VMEM((heads, tq, HEAD_DIM), F32),
                        pltpu.VMEM((heads, tq, HEAD_DIM), F32)],
        compiler_params=pltpu.CompilerParams(
            dimension_semantics=("parallel", "parallel", "arbitrary"),
            vmem_limit_bytes=VMEM_LIMIT_BYTES),
        name="fox_attention",
    )(q, kt, fk_rows, v, fq_cols, fstart, fend, qkb_row, sg, out_norm_w.reshape(1, d))


def _fox_bias_rows(fk3, *, batch, seq, n_heads):
    rows = fk3.reshape(batch, seq, 3, HEAD_DIM)[:, :, :, :n_heads].transpose(0, 3, 2, 1)
    rows = jnp.pad(rows, ((0, 0), (0, 0), (0, BF16_SUBLANES - 3), (0, 0)))
    return rows.reshape(batch * n_heads, BF16_SUBLANES, seq)


def kernel(x, a_norm_w, a_w_in, a_lb_logits, a_out_norm_w, a_w_out, kv_norm_w, kv_w, kv_f_bias, k_norm_w,
           b_norm_w, b_w_in, b_q_norm_w, b_out_norm_w, b_w_out):
    batch, seq, d = x.shape
    n_heads = d // HEAD_DIM
    n_a = a_w_in.shape[0]
    n_b = b_w_in.shape[0]
    xr = x.reshape(batch * seq, d)

    a_w_out_b = a_w_out.astype(BF16)
    b_w_out_b = b_w_out.astype(BF16)
    kv_w3 = kv_w[None]
    wf_b = jnp.pad(kv_w[:, 2 * d:], ((0, 0), (0, HEAD_DIM - n_heads))).astype(BF16)
    f_bias = jnp.pad(kv_f_bias.astype(F32), (0, HEAD_DIM - n_heads)).reshape(1, HEAD_DIM)

    lb_all = jnp.cumsum(jax.nn.softmax(a_lb_logits.astype(F32), axis=0), axis=0)
    lb_all = lb_all - lb_all[0:1]
    ones_row = jnp.ones((1, d), F32)

    h = _norm(xr, a_norm_w[0])
    for layer in range(n_a):
        q, w2, k, v, sg = _proj(
            h,
            [(a_w_in, layer, 0, ones_row, "cast", (BF16,)),
             (a_w_in, layer, d, lb_all[layer].reshape(1, d), "hgate", (F32, BF16)),
             (a_w_in, layer, 2 * d, ones_row, "cast", (BF16,)),
             (a_w_in, layer, 3 * d, ones_row, "silu", (BF16,))],
            name="hgrn_in_proj", seq=seq, tm=1024, tn=256)
        og = _hgrn_mix(q, w2, k, v, sg, a_out_norm_w[layer], batch=batch, seq=seq)
        if layer + 1 < n_a:
            xr, (h,) = _out_proj(og, a_w_out_b, layer, xr, [a_norm_w[layer + 1]])
        else:
            xr, (h_kv, h) = _out_proj(og, a_w_out_b, layer, xr, [kv_norm_w, b_norm_w[0]])

    k_norm_row = jnp.tile(k_norm_w.astype(F32), n_heads).reshape(1, d)
    kt, vv = _proj(
        h_kv,
        [(kv_w3, 0, 0, k_norm_row, "headnorm_t", (BF16,)),
         (kv_w3, 0, d, ones_row, "cast", (BF16,))],
        name="fox_kv_proj", seq=seq)
    fcol, fk3, fq_cols = _fgate(h_kv, wf_b, f_bias, batch=batch, seq=seq)
    fk_rows = _fox_bias_rows(fk3, batch=batch, seq=seq, n_heads=n_heads)

    for j in range(n_b):
        q_norm_row = jnp.tile(b_q_norm_w[j].astype(F32), n_heads).reshape(1, d)
        q, sg = _proj(
            h,
            [(b_w_in, j, 0, q_norm_row, "headnorm", (BF16,)),
             (b_w_in, j, d, ones_row, "silu", (BF16,))],
            scale=HEAD_DIM ** -0.5 * LOG2E, name="fox_in_proj", seq=seq)
        qk_bound = (HEAD_DIM * jnp.max(jnp.abs(b_q_norm_w[j])) * jnp.max(jnp.abs(k_norm_w))
                    * (HEAD_DIM ** -0.5 * LOG2E * 1.02))
        og = _fox_mix(q, kt, fk_rows, vv, fcol, fq_cols, qk_bound, sg, b_out_norm_w[j], batch=batch, seq=seq)
        if j + 1 < n_b:
            xr, (h,) = _out_proj(og, b_w_out_b, j, xr, [b_norm_w[j + 1]])
        else:
            xr, _ = _out_proj(og, b_w_out_b, j, xr, [])

    return xr.reshape(batch, seq, d)
```

```python
import functools

import jax
import jax.numpy as jnp
from jax import lax
from jax.experimental import pallas as pl
from jax.experimental.pallas import tpu as pltpu

HEAD_DIM = 128
VREG_ROWS = 8
SUB_BLOCK = 8
BF16_SUBLANES = 16
EPS = 1e-6
VMEM_LIMIT_BYTES = 56 * 1024 * 1024
LOG2E = 1.4426950408889634
FOX_TQ = 512
FOX_ROWS = 64
FOX_HEADS = 4
FOX_BIAS_GROUP = 32
FOX_SKIP_BITS = 152.0

F32 = jnp.float32
BF16 = jnp.bfloat16


def _dot(a, b):
    return jnp.dot(a, b, preferred_element_type=F32)


def _dot_nt(a, b):
    return lax.dot_general(a, b, (((1,), (1,)), ((), ())), preferred_element_type=F32)


def _dot_tn(a, b):
    return lax.dot_general(a, b, (((0,), (0,)), ((), ())), preferred_element_type=F32)


def _split3(x):
    hi = x.astype(BF16)
    r1 = x - hi.astype(F32)
    mid = r1.astype(BF16)
    lo = (r1 - mid.astype(F32)).astype(BF16)
    return hi, mid, lo


def _scan_sub_block(x):
    n, c = x.shape
    x3 = x.reshape(n // VREG_ROWS, VREG_ROWS, c)
    pos = lax.broadcasted_iota(jnp.int32, x3.shape, 1)
    shift = 1
    while shift < VREG_ROWS:
        x3 = x3 + jnp.where(pos >= shift, pltpu.roll(x3, shift, axis=1), 0.0)
        shift *= 2
    if SUB_BLOCK == 2 * VREG_ROWS:
        tile = lax.broadcasted_iota(jnp.int32, x3.shape, 0)
        carry = jnp.roll(jnp.broadcast_to(x3[:, VREG_ROWS - 1:, :], x3.shape), 1, axis=0)
        x3 = x3 + jnp.where(tile % 2 == 1, carry, 0.0)
    return x3.reshape(n, c)


def _rms_rows(x, w):
    ms = jnp.mean(x * x, axis=-1, keepdims=True)
    return x * lax.rsqrt(ms + EPS) * w


def _proj_kernel(*refs, kinds, scale):
    n = len(kinds)
    h_ref = refs[0]
    w_refs = refs[1:1 + n]
    aux_refs = refs[1 + n:1 + 2 * n]
    out_refs = refs[1 + 2 * n:-1]
    wb_ref = refs[-1]

    @pl.when(pl.program_id(1) == 0)
    def _():
        for s in range(n):
            wb_ref[s] = w_refs[s][...].astype(BF16)

    h = h_ref[...]
    oi = 0
    for s, kind in enumerate(kinds):
        acc = _dot(h, wb_ref[s])
        aux = aux_refs[s][...]
        if kind == "cast":
            out_refs[oi][...] = acc.astype(out_refs[oi].dtype)
            oi += 1
        elif kind == "silu":
            out_refs[oi][...] = (acc / (1.0 + jnp.exp(-acc))).astype(out_refs[oi].dtype)
            oi += 1
        elif kind in ("headnorm", "headnorm_t"):
            tn = acc.shape[1]
            for c in range(tn // HEAD_DIM):
                sl = slice(c * HEAD_DIM, (c + 1) * HEAD_DIM)
                a = acc[:, sl]
                ms = jnp.mean(a * a, axis=-1, keepdims=True)
                y = a * lax.rsqrt(ms + EPS) * aux[:, sl]
                if scale != 1.0:
                    y = y * scale
                if kind == "headnorm_t":
                    out_refs[oi][c] = y.T.astype(out_refs[oi].dtype)
                else:
                    out_refs[oi][:, sl] = y.astype(out_refs[oi].dtype)
            oi += 1
        elif kind == "hgate":
            lb = aux
            e = jnp.exp2(jnp.abs(acc) * (-LOG2E))
            one_e = 1.0 + e
            log2_sig = jnp.minimum(acc, 0.0) * LOG2E - jnp.log2(one_e)
            a = jnp.log2(lb)
            c = jnp.log2(1.0 - lb) + log2_sig
            g2 = jnp.maximum(a, c) + jnp.log2(1.0 + jnp.exp2(-jnp.abs(a - c)))
            r = 1.0 / one_e
            sig_neg = jnp.where(acc >= 0.0, e * r, r)
            out_refs[oi][...] = _scan_sub_block(g2).astype(out_refs[oi].dtype)
            out_refs[oi + 1][...] = ((1.0 - lb) * sig_neg).astype(out_refs[oi + 1].dtype)
            oi += 2
        else:
            raise ValueError(kind)


def _proj(h, streams, *, name, seq, scale=1.0, tm=1024, tn=512):
    n_rows, d = h.shape
    n_cols = streams[0][3].shape[1]
    tm = min(tm, seq)
    tn = min(tn, n_cols)
    assert seq % tm == 0 and n_rows % seq == 0 and n_cols % tn == 0 and tm % SUB_BLOCK == 0
    kinds = tuple(s[4] for s in streams)
    t_tiles = seq // tm
    heads_per_tile = tn // HEAD_DIM
    n_head_tiles = n_cols // tn

    in_specs = [pl.BlockSpec((tm, d), lambda j, i: (i, 0))]
    args = [h]
    for (w, layer, off, aux, kind, _) in streams:
        assert off % tn == 0
        if w.ndim == 2:
            in_specs.append(pl.BlockSpec((d, tn), functools.partial(lambda j, i, o: (0, j + o), o=off // tn)))
        else:
            in_specs.append(pl.BlockSpec((None, d, tn),
                                         functools.partial(lambda j, i, l, o: (l, 0, j + o), l=layer, o=off // tn)))
        args.append(w)
    for (w, layer, off, aux, kind, _) in streams:
        in_specs.append(pl.BlockSpec((1, tn), lambda j, i: (0, j)))
        args.append(aux)
    out_shapes, out_specs = [], []
    for (w, layer, off, aux, kind, dts) in streams:
        for dt in dts:
            if kind == "headnorm_t":
                out_shapes.append(jax.ShapeDtypeStruct((n_rows // seq * n_cols // HEAD_DIM, HEAD_DIM, seq), dt))
                out_specs.append(pl.BlockSpec(
                    (heads_per_tile, HEAD_DIM, tm),
                    lambda j, i: ((i // t_tiles) * n_head_tiles + j, 0, i % t_tiles)))
            else:
                out_shapes.append(jax.ShapeDtypeStruct((n_rows, n_cols), dt))
                out_specs.append(pl.BlockSpec((tm, tn), lambda j, i: (i, j)))

    return pl.pallas_call(
        functools.partial(_proj_kernel, kinds=kinds, scale=scale),
        grid=(n_cols // tn, n_rows // tm),
        in_specs=in_specs,
        out_specs=out_specs,
        out_shape=out_shapes,
        scratch_shapes=[pltpu.VMEM((len(streams), d, tn), BF16)],
        compiler_params=pltpu.CompilerParams(
            dimension_semantics=("parallel", "arbitrary"),
            vmem_limit_bytes=VMEM_LIMIT_BYTES),
        name=name,
    )(*args)


def _norm_kernel(x_ref, nw_ref, h_ref):
    h_ref[...] = _rms_rows(x_ref[...], nw_ref[...]).astype(h_ref.dtype)


def _norm(x, norm_w, *, tm=1024):
    n_rows, d = x.shape
    tm = min(tm, n_rows)
    assert n_rows % tm == 0
    return pl.pallas_call(
        _norm_kernel,
        grid=(n_rows // tm,),
        in_specs=[pl.BlockSpec((tm, d), lambda i: (i, 0)),
                  pl.BlockSpec((1, d), lambda i: (0, 0))],
        out_specs=pl.BlockSpec((tm, d), lambda i: (i, 0)),
        out_shape=jax.ShapeDtypeStruct((n_rows, d), BF16),
        compiler_params=pltpu.CompilerParams(
            dimension_semantics=("parallel",),
            vmem_limit_bytes=VMEM_LIMIT_BYTES),
        name="input_norm",
    )(x, norm_w.reshape(1, d))


def _out_proj_kernel(o_ref, w_ref, x_ref, *refs):
    n_next = (len(refs) - 1) // 2
    nw_refs, y_ref, h_refs = refs[:n_next], refs[n_next], refs[n_next + 1:]
    y = x_ref[...] + _dot(o_ref[...], w_ref[...])
    y_ref[...] = y
    if n_next:
        yn = y * lax.rsqrt(jnp.mean(y * y, axis=-1, keepdims=True) + EPS)
        for nw_ref, h_ref in zip(nw_refs, h_refs):
            h_ref[...] = (yn * nw_ref[...]).astype(h_ref.dtype)


def _out_proj(o, w3d, layer, x, next_norm_ws, *, tm=512):
    n_rows, d = x.shape
    tm = min(tm, n_rows)
    assert n_rows % tm == 0
    rows = pl.BlockSpec((tm, d), lambda i: (i, 0))
    gain = pl.BlockSpec((1, d), lambda i: (0, 0))
    n_next = len(next_norm_ws)
    outs = pl.pallas_call(
        _out_proj_kernel,
        grid=(n_rows // tm,),
        in_specs=[rows, pl.BlockSpec((None, d, d), functools.partial(lambda i, l: (l, 0, 0), l=layer)), rows]
                 + [gain] * n_next,
        out_specs=[rows] * (1 + n_next),
        out_shape=[jax.ShapeDtypeStruct((n_rows, d), F32)]
                  + [jax.ShapeDtypeStruct((n_rows, d), BF16)] * n_next,
        compiler_params=pltpu.CompilerParams(
            dimension_semantics=("parallel",),
            vmem_limit_bytes=VMEM_LIMIT_BYTES),
        name="out_proj",
    )(o, w3d, x, *[w.reshape(1, d) for w in next_norm_ws])
    return outs[0], list(outs[1:])


def _excl_prefix_rows(x):
    pos = lax.broadcasted_iota(jnp.int32, x.shape, 0)
    inc = x
    shift = 1
    while shift < x.shape[0]:
        inc = inc + jnp.where(pos >= shift, pltpu.roll(inc, shift, axis=0), 0.0)
        shift *= 2
    return inc - x


def _level_mid(bs2, lvl):
    n = bs2.shape[0]
    group = 2 << lvl
    if group >= VREG_ROWS:
        tiles = [jnp.broadcast_to(bs2[(t0 // group) * group + group // 2:][:1], (VREG_ROWS, HEAD_DIM))
                 for t0 in range(0, n, VREG_ROWS)]
        return jnp.concatenate(tiles, axis=0)
    x3 = bs2.reshape(n // VREG_ROWS, VREG_ROWS, HEAD_DIM)
    pos = lax.broadcasted_iota(jnp.int32, x3.shape, 1)
    out = None
    for g0 in range(0, VREG_ROWS, group):
        piece = jnp.broadcast_to(x3[:, g0 + group // 2:g0 + group // 2 + 1, :], x3.shape)
        out = piece if out is None else jnp.where(pos >= g0, piece, out)
    return out.reshape(n, HEAD_DIM)


def _rows_of(x, idx):
    return jnp.concatenate([x[i:i + 1] for i in idx], axis=0)


def _hgrn_chunk(q, k, w2, v, st, fac_ref, ck_ref, consts):
    lane8, sub_row, lvl_masks, same_blk, expand_r, jrow = consts
    c = q.shape[0]
    n_sub = c // SUB_BLOCK
    n_lvl = n_sub.bit_length() - 1

    gt2 = _rows_of(w2, [j * SUB_BLOCK + SUB_BLOCK - 1 for j in range(n_sub)])
    bs2 = _excl_prefix_rows(gt2)
    be2 = bs2 + gt2
    tot2 = be2[n_sub - 1:n_sub]
    fac_ref[0] = gt2
    fac_ref[1] = jnp.exp2(bs2)
    fac_ref[2] = jnp.exp2(tot2 - be2)
    for lvl in range(n_lvl):
        mid = _level_mid(bs2, lvl)
        upper = ((jrow >> lvl) & 1) == 1
        fac_ref[3 + 2 * lvl] = jnp.exp2(jnp.where(upper, bs2 - mid, -jnp.inf))
        fac_ref[4 + 2 * lvl] = jnp.exp2(jnp.where(upper, -jnp.inf, mid - be2))
    ck_ref[...] = jnp.log2(k) - w2

    qs_parts, kbar_parts, a_parts = [], [], []
    ql_parts = [[] for _ in range(n_lvl)]
    kl_parts = [[] for _ in range(n_lvl)]
    for j in range(n_sub):
        r0 = j * SUB_BLOCK
        sl = slice(r0, r0 + SUB_BLOCK)
        w_j, q_j = w2[sl], q[sl]
        qt = q_j * jnp.exp2(w_j)
        kh = k[sl] * jnp.exp2(fac_ref[0, j:j + 1, :] - w_j)
        qs_parts.append(qt * fac_ref[1, j:j + 1, :])
        kbar_parts.append(kh * fac_ref[2, j:j + 1, :])
        for lvl in range(n_lvl):
            ql_parts[lvl].append(qt * fac_ref[3 + 2 * lvl, j:j + 1, :])
            kl_parts[lvl].append(kh * fac_ref[4 + 2 * lvl, j:j + 1, :])

        for t0 in range(0, SUB_BLOCK, VREG_ROWS):
            w_t, q_t = w_j[t0:t0 + VREG_ROWS], q_j[t0:t0 + VREG_ROWS]
            a_t = jnp.zeros((VREG_ROWS, HEAD_DIM), F32)
            for s in range(t0 + VREG_ROWS):
                cs = ck_ref[r0 + s:r0 + s + 1, :]
                col = jnp.sum(q_t * jnp.exp2(w_t + cs), axis=-1, keepdims=True)
                a_t = jnp.where(lane8 == s, col, a_t)
            a_parts.append(jnp.where(lane8 <= sub_row + t0, a_t, 0.0))

    cat = lambda parts: jnp.concatenate(parts, axis=0)
    o = _dot_nt(cat(qs_parts).astype(BF16), st.astype(BF16))
    a = _dot_nt(cat(ql_parts[n_lvl - 1]).astype(BF16), cat(kl_parts[n_lvl - 1]).astype(BF16))
    for lvl in range(n_lvl - 2, -1, -1):
        a_l = _dot_nt(cat(ql_parts[lvl]).astype(BF16), cat(kl_parts[lvl]).astype(BF16))
        a = jnp.where(lvl_masks[lvl], a_l, a)
    a_diag = _dot(cat(a_parts).astype(BF16), expand_r)
    a = jnp.where(same_blk, a_diag, a)
    o = o + _dot(a.astype(BF16), v)
    st_new = st * jnp.exp2(tot2) + _dot_tn(v, cat(kbar_parts).astype(BF16))
    return o, st_new


def _hgrn_kernel(q_ref, w2_ref, k_ref, v_ref, sg_ref, onw_ref, o_ref, st_ref, fac_ref, ck_ref, *, chunk, heads):
    tb = q_ref.shape[0]
    n_chunks = tb // chunk
    n_sub = chunk // SUB_BLOCK
    n_lvl = n_sub.bit_length() - 1

    @pl.when(pl.program_id(2) == 0)
    def _():
        st_ref[...] = jnp.zeros_like(st_ref)

    r = lax.broadcasted_iota(jnp.int32, (chunk, chunk), 0) // SUB_BLOCK
    cidx = lax.broadcasted_iota(jnp.int32, (chunk, chunk), 1)
    lane_blk = cidx // SUB_BLOCK
    same_blk = r == lane_blk
    lvl_masks = [(r >> (lvl + 1)) == (lane_blk >> (lvl + 1)) for lvl in range(n_lvl)]
    lane8 = lax.broadcasted_iota(jnp.int32, (VREG_ROWS, HEAD_DIM), 1)
    sub_row = lax.broadcasted_iota(jnp.int32, (VREG_ROWS, HEAD_DIM), 0)
    er = lax.broadcasted_iota(jnp.int32, (HEAD_DIM, chunk), 0)
    ec = lax.broadcasted_iota(jnp.int32, (HEAD_DIM, chunk), 1)
    expand_r = (ec % SUB_BLOCK == er).astype(BF16)
    jrow = lax.broadcasted_iota(jnp.int32, (n_sub, HEAD_DIM), 0)
    consts = (lane8, sub_row, lvl_masks, same_blk, expand_r, jrow)

    for ci in range(n_chunks):
        rows = slice(ci * chunk, (ci + 1) * chunk)
        for h in range(heads):
            lanes = slice(h * HEAD_DIM, (h + 1) * HEAD_DIM)
            q = q_ref[rows, lanes].astype(F32)
            k = k_ref[rows, lanes].astype(F32)
            o, st_new = _hgrn_chunk(q, k, w2_ref[rows, lanes], v_ref[rows, lanes], st_ref[h],
                                    fac_ref.at[ci * heads + h], ck_ref.at[ci * heads + h], consts)
            st_ref[h] = st_new
            ms = jnp.mean(o * o, axis=-1, keepdims=True)
            y = o * lax.rsqrt(ms + EPS) * onw_ref[:, lanes]
            o_ref[rows, lanes] = (y * sg_ref[rows, lanes].astype(F32)).astype(o_ref.dtype)


def _hgrn_mix(q, w2, k, v, sg, out_norm_w, *, batch, seq, chunk=128, heads=4, tb=512):
    n_rows, d = q.shape
    n_heads = d // HEAD_DIM
    heads = min(heads, n_heads)
    tb = min(tb, seq)
    chunk = min(chunk, tb)
    n_sub = chunk // SUB_BLOCK
    assert seq % tb == 0 and tb % chunk == 0 and n_heads % heads == 0
    assert chunk % SUB_BLOCK == 0 and n_sub & (n_sub - 1) == 0
    nt = seq // tb
    n_fac = 3 + 2 * (n_sub.bit_length() - 1)
    blk = pl.BlockSpec((tb, heads * HEAD_DIM), lambda b, h, t: (b * nt + t, h))
    return pl.pallas_call(
        functools.partial(_hgrn_kernel, chunk=chunk, heads=heads),
        grid=(batch, n_heads // heads, nt),
        in_specs=[blk, blk, blk, blk, blk,
                  pl.BlockSpec((1, heads * HEAD_DIM), lambda b, h, t: (0, h))],
        out_specs=blk,
        out_shape=jax.ShapeDtypeStruct((n_rows, d), BF16),
        scratch_shapes=[pltpu.VMEM((heads, HEAD_DIM, HEAD_DIM), F32),
                        pltpu.VMEM((tb // chunk * heads, n_fac, n_sub, HEAD_DIM), F32),
                        pltpu.VMEM((tb // chunk * heads, chunk, HEAD_DIM), F32)],
        compiler_params=pltpu.CompilerParams(
            dimension_semantics=("parallel", "parallel", "arbitrary"),
            vmem_limit_bytes=VMEM_LIMIT_BYTES),
        name="hgrn_mix",
    )(q, w2, k, v, sg, out_norm_w.reshape(1, d))


def _fgate_kernel(h_ref, wf_ref, bias_ref, f_ref, fk_ref, fq_ref, carry_ref):
    @pl.when(pl.program_id(1) == 0)
    def _():
        carry_ref[...] = jnp.zeros_like(carry_ref)

    z = _dot(h_ref[...], wf_ref[...]) + bias_ref[...]
    ls = jnp.minimum(z, 0.0) - jnp.log1p(jnp.exp(-jnp.abs(z)))
    tt = z.shape[0]
    r = lax.broadcasted_iota(jnp.int32, (tt, tt), 0)
    c = lax.broadcasted_iota(jnp.int32, (tt, tt), 1)
    tri = (c <= r).astype(BF16)
    hi, mid, lo = _split3(ls)
    cum = _dot(tri, hi) + _dot(tri, mid) + _dot(tri, lo)
    f = cum + carry_ref[...]
    f_ref[...] = f
    carry_ref[...] = f[tt - 1:tt]
    terms = _split3(f * (-LOG2E))
    for i, term in enumerate(terms):
        fk_ref[i] = term.astype(F32).T.astype(BF16)
    lane = lax.broadcasted_iota(jnp.int32, f.shape, 1)
    g = FOX_BIAS_GROUP
    t_hi, t_mid, t_lo = [-t.astype(F32) for t in terms]
    fq = jnp.where(lane < g, t_hi,
                   jnp.where(lane < 2 * g, pltpu.roll(t_mid, g, axis=1),
                             jnp.where(lane < 3 * g, pltpu.roll(t_lo, 2 * g, axis=1),
                                       jnp.where(lane < 3 * g + 3, 1.0, 0.0))))
    fq_ref[...] = fq.astype(BF16)


def _fgate(h, wf, bias, *, batch, seq, tt=256):
    n_rows, d = h.shape
    tt = min(tt, seq)
    nt = seq // tt
    return pl.pallas_call(
        _fgate_kernel,
        grid=(batch, nt),
        in_specs=[pl.BlockSpec((tt, d), lambda b, t: (b * nt + t, 0)),
                  pl.BlockSpec((d, HEAD_DIM), lambda b, t: (0, 0)),
                  pl.BlockSpec((1, HEAD_DIM), lambda b, t: (0, 0))],
        out_specs=[pl.BlockSpec((tt, HEAD_DIM), lambda b, t: (b * nt + t, 0)),
                   pl.BlockSpec((None, 3, HEAD_DIM, tt), lambda b, t: (b, 0, 0, t)),
                   pl.BlockSpec((tt, HEAD_DIM), lambda b, t: (b * nt + t, 0))],
        out_shape=[jax.ShapeDtypeStruct((n_rows, HEAD_DIM), F32),
                   jax.ShapeDtypeStruct((batch, 3, HEAD_DIM, seq), BF16),
                   jax.ShapeDtypeStruct((n_rows, HEAD_DIM), BF16)],
        scratch_shapes=[pltpu.VMEM((1, HEAD_DIM), F32)],
        compiler_params=pltpu.CompilerParams(
            dimension_semantics=("parallel", "arbitrary"),
            vmem_limit_bytes=VMEM_LIMIT_BYTES),
        name="fox_forget_gate",
    )(h, wf, bias)


def _fox_scores(qa_ref, kt_ref, s_ref, slot, h, cols):
    s_ref[slot] = _dot(qa_ref[h], kt_ref[h, :, cols])


def _fox_softmax(s_ref, p_ref, m_ref, l_ref, alpha_ref, slot, h, diag, first):
    tq, tk = s_ref.shape[1], s_ref.shape[2]
    for r in range(tq // FOX_ROWS):
        row0 = r * FOX_ROWS
        rows = pl.ds(row0, FOX_ROWS)
        n_chunks = (row0 + FOX_ROWS - 1) // HEAD_DIM + 1 if diag else tk // HEAD_DIM
        chunks = [s_ref[slot, rows, c * HEAD_DIM:(c + 1) * HEAD_DIM] for c in range(n_chunks)]
        if diag:
            last = n_chunks - 1
            rr = lax.broadcasted_iota(jnp.int32, (FOX_ROWS, HEAD_DIM), 0) + row0
            cc = lax.broadcasted_iota(jnp.int32, (FOX_ROWS, HEAD_DIM), 1) + last * HEAD_DIM
            chunks[last] = jnp.where(rr >= cc, chunks[last], -jnp.inf)
        mx = chunks[0]
        for ch in chunks[1:]:
            mx = jnp.maximum(mx, ch)
        m_new = jnp.max(mx, axis=-1, keepdims=True)
        if first:
            m_new = jnp.broadcast_to(m_new, (FOX_ROWS, HEAD_DIM))
        else:
            m_prev = m_ref[h, rows, :]
            m_new = jnp.maximum(m_prev, m_new)
            alpha = jnp.exp2(m_prev - m_new)
        psum = None
        for c, ch in enumerate(chunks):
            part = jnp.exp2(ch - m_new)
            psum = part if psum is None else psum + part
            p_ref[h, rows, c * HEAD_DIM:(c + 1) * HEAD_DIM] = part.astype(BF16)
        if n_chunks * HEAD_DIM < tk:
            p_ref[h, rows, n_chunks * HEAD_DIM:] = jnp.zeros((FOX_ROWS, tk - n_chunks * HEAD_DIM), BF16)
        row_sum = jnp.broadcast_to(jnp.sum(psum, axis=-1, keepdims=True), (FOX_ROWS, HEAD_DIM))
        if first:
            l_ref[h, rows, :] = row_sum
        else:
            l_ref[h, rows, :] = alpha * l_ref[h, rows, :] + row_sum
            alpha_ref[h, rows, :] = alpha
        m_ref[h, rows, :] = m_new


def _fox_values(p_ref, v_ref, alpha_ref, acc_ref, h, cols, first):
    lanes = slice(h * HEAD_DIM, (h + 1) * HEAD_DIM)
    pv = _dot(p_ref[h], v_ref[cols, lanes])
    acc_ref[h] = pv if first else alpha_ref[h] * acc_ref[h] + pv


def _fox_kernel(q_ref, ktr_ref, fk_ref, v_ref, fq_ref, fstart_ref, fend_ref, qkb_ref, sg_ref, onw_ref, o_ref,
                kt_ref, qa_ref, s_ref, p_ref, m_ref, l_ref, alpha_ref, acc_ref, *, heads):
    hp = pl.program_id(1)
    qi = pl.program_id(2)
    tq = q_ref.shape[0]

    @pl.when(qi == 0)
    def _():
        n_sel = 3 * FOX_BIAS_GROUP
        n_bias = fk_ref.shape[1]
        seq = kt_ref.shape[2]
        kt_ref[:, :HEAD_DIM, :] = ktr_ref[...]
        sel_row = lax.broadcasted_iota(jnp.int32, (n_sel, seq), 0) % FOX_BIAS_GROUP
        for h in range(heads):
            kt_ref[h, HEAD_DIM:HEAD_DIM + n_sel, :] = jnp.where(sel_row == hp * heads + h, 1.0, 0.0).astype(BF16)
        kt_ref[:, HEAD_DIM + n_sel:HEAD_DIM + n_sel + n_bias, :] = fk_ref[...]
        kt_ref[:, HEAD_DIM + n_sel + n_bias:, :] = jnp.zeros((heads, HEAD_DIM - n_sel - n_bias, seq), BF16)
    for h in range(heads):
        qa_ref[h, :, :HEAD_DIM] = q_ref[:, h * HEAD_DIM:(h + 1) * HEAD_DIM]
        qa_ref[h, :, HEAD_DIM:] = fq_ref[...]

    def block(kb, diag):
        cols = pl.ds(pl.multiple_of(kb * tq, tq), tq)
        for h in range(heads):
            _fox_scores(qa_ref, kt_ref, s_ref, h, h, cols)
        for h in range(heads):
            _fox_softmax(s_ref, p_ref, m_ref, l_ref, alpha_ref, h, h, diag, diag)
        for h in range(heads):
            _fox_values(p_ref, v_ref, alpha_ref, acc_ref, h, cols, diag)

    block(qi, True)
    lane8 = lax.broadcasted_iota(jnp.int32, fend_ref.shape, 1)
    kb8 = lax.broadcasted_iota(jnp.int32, fend_ref.shape, 0)
    gap = 2.0 * qkb_ref[...] + LOG2E * (fstart_ref[pl.ds(qi, 1), :] - fend_ref[...])
    mine = (lane8 >= hp * heads) & (lane8 < (hp + 1) * heads) & (kb8 < qi)
    live = jnp.where(mine & (gap >= -FOX_SKIP_BITS), 1.0, 0.0)
    n_live = jnp.sum(jnp.max(live, axis=1, keepdims=True)).astype(jnp.int32)

    def body(j, carry):
        block(qi - 1 - j, False)
        return carry

    lax.fori_loop(0, n_live, body, 0)

    for h in range(heads):
        lanes = slice(h * HEAD_DIM, (h + 1) * HEAD_DIM)
        o = acc_ref[h] / l_ref[h]
        ms = jnp.mean(o * o, axis=-1, keepdims=True)
        y = o * lax.rsqrt(ms + EPS) * onw_ref[:, lanes]
        o_ref[:, lanes] = (y * sg_ref[:, lanes].astype(F32)).astype(o_ref.dtype)


def _fox_mix(q, kt, fk_rows, v, fcol, fq_cols, qk_bound, sg, out_norm_w, *, batch, seq):
    n_rows, d = q.shape
    n_heads = d // HEAD_DIM
    heads = min(FOX_HEADS, n_heads)
    tq = min(FOX_TQ, seq)
    assert seq % tq == 0 and tq % FOX_ROWS == 0 and tq % HEAD_DIM == 0 and n_heads % heads == 0
    assert n_heads <= FOX_BIAS_GROUP
    nq = seq // tq
    n_pairs = n_heads // heads
    qblk = pl.BlockSpec((tq, heads * HEAD_DIM), lambda b, h, i: (b * nq + i, h))
    fblocks = fcol.reshape(batch, nq, tq, HEAD_DIM)
    fstart, fend = fblocks[:, :, 0, :], fblocks[:, :, tq - 1, :]
    fedge = pl.BlockSpec((None, nq, HEAD_DIM), lambda b, h, i: (b, 0, 0))
    qkb_row = jnp.full((1, HEAD_DIM), qk_bound, F32)
    return pl.pallas_call(
        functools.partial(_fox_kernel, heads=heads),
        grid=(batch, n_pairs, nq),
        in_specs=[qblk,
                  pl.BlockSpec((heads, HEAD_DIM, seq), lambda b, h, i: (b * n_pairs + h, 0, 0)),
                  pl.BlockSpec((heads, fk_rows.shape[1], seq), lambda b, h, i: (b * n_pairs + h, 0, 0)),
                  pl.BlockSpec((seq, heads * HEAD_DIM), lambda b, h, i: (b, h)),
                  pl.BlockSpec((tq, HEAD_DIM), lambda b, h, i: (b * nq + i, 0)),
                  fedge, fedge,
                  pl.BlockSpec((1, HEAD_DIM), lambda b, h, i: (0, 0)),
                  qblk,
                  pl.BlockSpec((1, heads * HEAD_DIM), lambda b, h, i: (0, h))],
        out_specs=qblk,
        out_shape=jax.ShapeDtypeStruct((n_rows, d), BF16),
        scratch_shapes=[pltpu.VMEM((heads, 2 * HEAD_DIM, seq), BF16),
                        pltpu.VMEM((heads, tq, 2 * HEAD_DIM), BF16),
                        pltpu.VMEM((heads, tq, tq), F32),
                        pltpu.VMEM((heads, tq, tq), BF16),
                        pltpu.VMEM((heads, tq, HEAD_DIM), F32),
                        pltpu.VMEM((heads, tq, HEAD_DIM), F32),
                        pltpu.VMEM((heads, tq, HEAD_DIM), F32),
                        pltpu.VMEM((heads, tq, HEAD_DIM), F32)],
        compiler_params=pltpu.CompilerParams(
            dimension_semantics=("parallel", "parallel", "arbitrary"),
            vmem_limit_bytes=VMEM_LIMIT_BYTES),
        name="fox_attention",
    )(q, kt, fk_rows, v, fq_cols, fstart, fend, qkb_row, sg, out_norm_w.reshape(1, d))


def _fox_bias_rows(fk3, *, batch, seq, n_heads):
    rows = fk3[:, :, :n_heads, :].transpose(0, 2, 1, 3)
    rows = jnp.pad(rows, ((0, 0), (0, 0), (0, BF16_SUBLANES - 3), (0, 0)))
    return rows.reshape(batch * n_heads, BF16_SUBLANES, seq)


def kernel(x, a_norm_w, a_w_in, a_lb_logits, a_out_norm_w, a_w_out, kv_norm_w, kv_w, kv_f_bias, k_norm_w,
           b_norm_w, b_w_in, b_q_norm_w, b_out_norm_w, b_w_out):
    batch, seq, d = x.shape
    n_heads = d // HEAD_DIM
    n_a = a_w_in.shape[0]
    n_b = b_w_in.shape[0]
    xr = x.reshape(batch * seq, d)

    a_w_out_b = a_w_out.astype(BF16)
    b_w_out_b = b_w_out.astype(BF16)
    wf_b = jnp.pad(kv_w[:, 2 * d:], ((0, 0), (0, HEAD_DIM - n_heads))).astype(BF16)
    f_bias = jnp.pad(kv_f_bias.astype(F32), (0, HEAD_DIM - n_heads)).reshape(1, HEAD_DIM)

    lb_all = jnp.cumsum(jax.nn.softmax(a_lb_logits.astype(F32), axis=0), axis=0)
    lb_all = lb_all - lb_all[0:1]
    ones_row = jnp.ones((1, d), F32)

    h = _norm(xr, a_norm_w[0])
    for layer in range(n_a):
        q, w2, k, v, sg = _proj(
            h,
            [(a_w_in, layer, 0, ones_row, "cast", (BF16,)),
             (a_w_in, layer, d, lb_all[layer].reshape(1, d), "hgate", (F32, BF16)),
             (a_w_in, layer, 2 * d, ones_row, "cast", (BF16,)),
             (a_w_in, layer, 3 * d, ones_row, "silu", (BF16,))],
            name="hgrn_in_proj", seq=seq, tm=1024, tn=256)
        og = _hgrn_mix(q, w2, k, v, sg, a_out_norm_w[layer], batch=batch, seq=seq)
        if layer + 1 < n_a:
            xr, (h,) = _out_proj(og, a_w_out_b, layer, xr, [a_norm_w[layer + 1]])
        else:
            xr, (h_kv, h) = _out_proj(og, a_w_out_b, layer, xr, [kv_norm_w, b_norm_w[0]])

    k_norm_row = jnp.tile(k_norm_w.astype(F32), n_heads).reshape(1, d)
    kt, vv = _proj(
        h_kv,
        [(kv_w, 0, 0, k_norm_row, "headnorm_t", (BF16,)),
         (kv_w, 0, d, ones_row, "cast", (BF16,))],
        name="fox_kv_proj", seq=seq)
    fcol, fk3, fq_cols = _fgate(h_kv, wf_b, f_bias, batch=batch, seq=seq)
    fk_rows = _fox_bias_rows(fk3, batch=batch, seq=seq, n_heads=n_heads)

    for j in range(n_b):
        q_norm_row = jnp.tile(b_q_norm_w[j].astype(F32), n_heads).reshape(1, d)
        q, sg = _proj(
            h,
            [(b_w_in, j, 0, q_norm_row, "headnorm", (BF16,)),
             (b_w_in, j, d, ones_row, "silu", (BF16,))],
            scale=HEAD_DIM ** -0.5 * LOG2E, name="fox_in_proj", seq=seq)
        qk_bound = (HEAD_DIM * jnp.max(jnp.abs(b_q_norm_w[j])) * jnp.max(jnp.abs(k_norm_w))
                    * (HEAD_DIM ** -0.5 * LOG2E * 1.02))
        og = _fox_mix(q, kt, fk_rows, vv, fcol, fq_cols, qk_bound, sg, b_out_norm_w[j], batch=batch, seq=seq)
        if j + 1 < n_b:
            xr, (h,) = _out_proj(og, b_w_out_b, j, xr, [b_norm_w[j + 1]])
        else:
            xr, _ = _out_proj(og, b_w_out_b, j, xr, [])

    return xr.reshape(batch, seq, d)
```

```python
import functools

import jax
import jax.numpy as jnp
from jax import lax
from jax.experimental import pallas as pl
from jax.experimental.pallas import tpu as pltpu

HEAD_DIM = 128
VREG_ROWS = 8
SUB_BLOCK = 8
BF16_SUBLANES = 16
EPS = 1e-6
VMEM_LIMIT_BYTES = 56 * 1024 * 1024
LOG2E = 1.4426950408889634
FOX_TQ = 512
FOX_ROWS = 64
FOX_HEADS = 4
FOX_BIAS_GROUP = 32
FOX_SKIP_BITS = 152.0

F32 = jnp.float32
BF16 = jnp.bfloat16


def _dot(a, b):
    return jnp.dot(a, b, preferred_element_type=F32)


def _dot_nt(a, b):
    return lax.dot_general(a, b, (((1,), (1,)), ((), ())), preferred_element_type=F32)


def _dot_tn(a, b):
    return lax.dot_general(a, b, (((0,), (0,)), ((), ())), preferred_element_type=F32)


def _split3(x):
    hi = x.astype(BF16)
    r1 = x - hi.astype(F32)
    mid = r1.astype(BF16)
    lo = (r1 - mid.astype(F32)).astype(BF16)
    return hi, mid, lo


def _scan_sub_block(x):
    n, c = x.shape
    x3 = x.reshape(n // VREG_ROWS, VREG_ROWS, c)
    pos = lax.broadcasted_iota(jnp.int32, x3.shape, 1)
    shift = 1
    while shift < VREG_ROWS:
        x3 = x3 + jnp.where(pos >= shift, pltpu.roll(x3, shift, axis=1), 0.0)
        shift *= 2
    if SUB_BLOCK == 2 * VREG_ROWS:
        tile = lax.broadcasted_iota(jnp.int32, x3.shape, 0)
        carry = jnp.roll(jnp.broadcast_to(x3[:, VREG_ROWS - 1:, :], x3.shape), 1, axis=0)
        x3 = x3 + jnp.where(tile % 2 == 1, carry, 0.0)
    return x3.reshape(n, c)


def _rms_rows(x, w):
    ms = jnp.mean(x * x, axis=-1, keepdims=True)
    return x * lax.rsqrt(ms + EPS) * w


def _proj_kernel(*refs, kinds, scale):
    n = len(kinds)
    h_ref = refs[0]
    w_refs = refs[1:1 + n]
    aux_refs = refs[1 + n:1 + 2 * n]
    out_refs = refs[1 + 2 * n:-1]
    wb_ref = refs[-1]

    @pl.when(pl.program_id(1) == 0)
    def _():
        for s in range(n):
            wb_ref[s] = w_refs[s][...].astype(BF16)

    h = h_ref[...]
    oi = 0
    for s, kind in enumerate(kinds):
        acc = _dot(h, wb_ref[s])
        aux = aux_refs[s][...]
        if kind == "cast":
            out_refs[oi][...] = acc.astype(out_refs[oi].dtype)
            oi += 1
        elif kind == "silu":
            out_refs[oi][...] = (acc / (1.0 + jnp.exp(-acc))).astype(out_refs[oi].dtype)
            oi += 1
        elif kind in ("headnorm", "headnorm_t"):
            tn = acc.shape[1]
            for c in range(tn // HEAD_DIM):
                sl = slice(c * HEAD_DIM, (c + 1) * HEAD_DIM)
                a = acc[:, sl]
                ms = jnp.mean(a * a, axis=-1, keepdims=True)
                y = a * lax.rsqrt(ms + EPS) * aux[:, sl]
                if scale != 1.0:
                    y = y * scale
                if kind == "headnorm_t":
                    out_refs[oi][c] = y.T.astype(out_refs[oi].dtype)
                else:
                    out_refs[oi][:, sl] = y.astype(out_refs[oi].dtype)
            oi += 1
        elif kind == "hgate":
            lb = aux
            e = jnp.exp2(jnp.abs(acc) * (-LOG2E))
            one_e = 1.0 + e
            log2_sig = jnp.minimum(acc, 0.0) * LOG2E - jnp.log2(one_e)
            a = jnp.log2(lb)
            c = jnp.log2(1.0 - lb) + log2_sig
            g2 = jnp.maximum(a, c) + jnp.log2(1.0 + jnp.exp2(-jnp.abs(a - c)))
            r = 1.0 / one_e
            sig_neg = jnp.where(acc >= 0.0, e * r, r)
            out_refs[oi][...] = _scan_sub_block(g2).astype(out_refs[oi].dtype)
            out_refs[oi + 1][...] = ((1.0 - lb) * sig_neg).astype(out_refs[oi + 1].dtype)
            oi += 2
        else:
            raise ValueError(kind)


def _proj(h, streams, *, name, seq, scale=1.0, tm=1024, tn=512):
    n_rows, d = h.shape
    n_cols = streams[0][3].shape[1]
    tm = min(tm, seq)
    tn = min(tn, n_cols)
    assert seq % tm == 0 and n_rows % seq == 0 and n_cols % tn == 0 and tm % SUB_BLOCK == 0
    kinds = tuple(s[4] for s in streams)
    t_tiles = seq // tm
    heads_per_tile = tn // HEAD_DIM
    n_head_tiles = n_cols // tn

    in_specs = [pl.BlockSpec((tm, d), lambda j, i: (i, 0))]
    args = [h]
    for (w, layer, off, aux, kind, _) in streams:
        assert off % tn == 0
        if w.ndim == 2:
            in_specs.append(pl.BlockSpec((d, tn), functools.partial(lambda j, i, o: (0, j + o), o=off // tn)))
        else:
            in_specs.append(pl.BlockSpec((None, d, tn),
                                         functools.partial(lambda j, i, l, o: (l, 0, j + o), l=layer, o=off // tn)))
        args.append(w)
    for (w, layer, off, aux, kind, _) in streams:
        in_specs.append(pl.BlockSpec((1, tn), lambda j, i: (0, j)))
        args.append(aux)
    out_shapes, out_specs = [], []
    for (w, layer, off, aux, kind, dts) in streams:
        for dt in dts:
            if kind == "headnorm_t":
                out_shapes.append(jax.ShapeDtypeStruct((n_rows // seq * n_cols // HEAD_DIM, HEAD_DIM, seq), dt))
                out_specs.append(pl.BlockSpec(
                    (heads_per_tile, HEAD_DIM, tm),
                    lambda j, i: ((i // t_tiles) * n_head_tiles + j, 0, i % t_tiles)))
            else:
                out_shapes.append(jax.ShapeDtypeStruct((n_rows, n_cols), dt))
                out_specs.append(pl.BlockSpec((tm, tn), lambda j, i: (i, j)))

    return pl.pallas_call(
        functools.partial(_proj_kernel, kinds=kinds, scale=scale),
        grid=(n_cols // tn, n_rows // tm),
        in_specs=in_specs,
        out_specs=out_specs,
        out_shape=out_shapes,
        scratch_shapes=[pltpu.VMEM((len(streams), d, tn), BF16)],
        compiler_params=pltpu.CompilerParams(
            dimension_semantics=("parallel", "arbitrary"),
            vmem_limit_bytes=VMEM_LIMIT_BYTES),
        name=name,
    )(*args)


def _norm_kernel(x_ref, nw_ref, h_ref):
    h_ref[...] = _rms_rows(x_ref[...], nw_ref[...]).astype(h_ref.dtype)


def _norm(x, norm_w, *, tm=1024):
    n_rows, d = x.shape
    tm = min(tm, n_rows)
    assert n_rows % tm == 0
    return pl.pallas_call(
        _norm_kernel,
        grid=(n_rows // tm,),
        in_specs=[pl.BlockSpec((tm, d), lambda i: (i, 0)),
                  pl.BlockSpec((1, d), lambda i: (0, 0))],
        out_specs=pl.BlockSpec((tm, d), lambda i: (i, 0)),
        out_shape=jax.ShapeDtypeStruct((n_rows, d), BF16),
        compiler_params=pltpu.CompilerParams(
            dimension_semantics=("parallel",),
            vmem_limit_bytes=VMEM_LIMIT_BYTES),
        name="input_norm",
    )(x, norm_w.reshape(1, d))


def _out_proj_kernel(o_ref, sg_ref, onw_ref, w_ref, x_ref, *refs):
    n_next = (len(refs) - 1) // 2
    nw_refs, y_ref, h_refs = refs[:n_next], refs[n_next], refs[n_next + 1:]
    gated = []
    for c in range(o_ref.shape[1] // HEAD_DIM):
        lanes = slice(c * HEAD_DIM, (c + 1) * HEAD_DIM)
        o = o_ref[:, lanes].astype(F32)
        on = o * lax.rsqrt(jnp.mean(o * o, axis=-1, keepdims=True) + EPS) * onw_ref[:, lanes]
        gated.append((on * sg_ref[:, lanes].astype(F32)).astype(BF16))
    y = x_ref[...] + _dot(jnp.concatenate(gated, axis=1), w_ref[...])
    y_ref[...] = y
    if n_next:
        yn = y * lax.rsqrt(jnp.mean(y * y, axis=-1, keepdims=True) + EPS)
        for nw_ref, h_ref in zip(nw_refs, h_refs):
            h_ref[...] = (yn * nw_ref[...]).astype(h_ref.dtype)


def _out_proj(o, sg, out_norm_w, w3d, layer, x, next_norm_ws, *, tm=512):
    n_rows, d = x.shape
    tm = min(tm, n_rows)
    assert n_rows % tm == 0
    rows = pl.BlockSpec((tm, d), lambda i: (i, 0))
    gain = pl.BlockSpec((1, d), lambda i: (0, 0))
    n_next = len(next_norm_ws)
    outs = pl.pallas_call(
        _out_proj_kernel,
        grid=(n_rows // tm,),
        in_specs=[rows, rows, gain,
                  pl.BlockSpec((None, d, d), functools.partial(lambda i, l: (l, 0, 0), l=layer),
                               pipeline_mode=pl.Buffered(1)),
                  rows] + [gain] * n_next,
        out_specs=[rows] * (1 + n_next),
        out_shape=[jax.ShapeDtypeStruct((n_rows, d), F32)]
                  + [jax.ShapeDtypeStruct((n_rows, d), BF16)] * n_next,
        compiler_params=pltpu.CompilerParams(
            dimension_semantics=("parallel",),
            vmem_limit_bytes=VMEM_LIMIT_BYTES),
        name="out_proj",
    )(o, sg, out_norm_w.reshape(1, d), w3d, x, *[w.reshape(1, d) for w in next_norm_ws])
    return outs[0], list(outs[1:])


def _excl_prefix_rows(x):
    pos = lax.broadcasted_iota(jnp.int32, x.shape, 0)
    inc = x
    shift = 1
    while shift < x.shape[0]:
        inc = inc + jnp.where(pos >= shift, pltpu.roll(inc, shift, axis=0), 0.0)
        shift *= 2
    return inc - x


def _level_mid(bs2, lvl):
    n = bs2.shape[0]
    group = 2 << lvl
    if group >= VREG_ROWS:
        tiles = [jnp.broadcast_to(bs2[(t0 // group) * group + group // 2:][:1], (VREG_ROWS, HEAD_DIM))
                 for t0 in range(0, n, VREG_ROWS)]
        return jnp.concatenate(tiles, axis=0)
    x3 = bs2.reshape(n // VREG_ROWS, VREG_ROWS, HEAD_DIM)
    pos = lax.broadcasted_iota(jnp.int32, x3.shape, 1)
    out = None
    for g0 in range(0, VREG_ROWS, group):
        piece = jnp.broadcast_to(x3[:, g0 + group // 2:g0 + group // 2 + 1, :], x3.shape)
        out = piece if out is None else jnp.where(pos >= g0, piece, out)
    return out.reshape(n, HEAD_DIM)


def _rows_of(x, idx):
    return jnp.concatenate([x[i:i + 1] for i in idx], axis=0)


def _hgrn_chunk(q, k, w2, v, st, fac_ref, ck_ref, consts):
    lane8, sub_row, lvl_masks, same_blk, expand_r, jrow = consts
    c = q.shape[0]
    n_sub = c // SUB_BLOCK
    n_lvl = n_sub.bit_length() - 1

    gt2 = _rows_of(w2, [j * SUB_BLOCK + SUB_BLOCK - 1 for j in range(n_sub)])
    bs2 = _excl_prefix_rows(gt2)
    be2 = bs2 + gt2
    tot2 = be2[n_sub - 1:n_sub]
    fac_ref[0] = gt2
    fac_ref[1] = jnp.exp2(bs2)
    fac_ref[2] = jnp.exp2(tot2 - be2)
    for lvl in range(n_lvl):
        mid = _level_mid(bs2, lvl)
        upper = ((jrow >> lvl) & 1) == 1
        fac_ref[3 + 2 * lvl] = jnp.exp2(jnp.where(upper, bs2 - mid, -jnp.inf))
        fac_ref[4 + 2 * lvl] = jnp.exp2(jnp.where(upper, -jnp.inf, mid - be2))
    ck_ref[...] = jnp.log2(k) - w2

    qs_parts, kbar_parts, a_parts = [], [], []
    ql_parts = [[] for _ in range(n_lvl)]
    kl_parts = [[] for _ in range(n_lvl)]
    for j in range(n_sub):
        r0 = j * SUB_BLOCK
        sl = slice(r0, r0 + SUB_BLOCK)
        w_j, q_j = w2[sl], q[sl]
        qt = q_j * jnp.exp2(w_j)
        kh = k[sl] * jnp.exp2(fac_ref[0, j:j + 1, :] - w_j)
        qs_parts.append(qt * fac_ref[1, j:j + 1, :])
        kbar_parts.append(kh * fac_ref[2, j:j + 1, :])
        for lvl in range(n_lvl):
            ql_parts[lvl].append(qt * fac_ref[3 + 2 * lvl, j:j + 1, :])
            kl_parts[lvl].append(kh * fac_ref[4 + 2 * lvl, j:j + 1, :])

        for t0 in range(0, SUB_BLOCK, VREG_ROWS):
            w_t, q_t = w_j[t0:t0 + VREG_ROWS], q_j[t0:t0 + VREG_ROWS]
            a_t = jnp.zeros((VREG_ROWS, HEAD_DIM), F32)
            for s in range(t0 + VREG_ROWS):
                cs = ck_ref[r0 + s:r0 + s + 1, :]
                col = jnp.sum(q_t * jnp.exp2(w_t + cs), axis=-1, keepdims=True)
                a_t = jnp.where(lane8 == s, col, a_t)
            a_parts.append(jnp.where(lane8 <= sub_row + t0, a_t, 0.0))

    cat = lambda parts: jnp.concatenate(parts, axis=0)
    o = _dot_nt(cat(qs_parts).astype(BF16), st.astype(BF16))
    a = _dot_nt(cat(ql_parts[n_lvl - 1]).astype(BF16), cat(kl_parts[n_lvl - 1]).astype(BF16))
    for lvl in range(n_lvl - 2, -1, -1):
        a_l = _dot_nt(cat(ql_parts[lvl]).astype(BF16), cat(kl_parts[lvl]).astype(BF16))
        a = jnp.where(lvl_masks[lvl], a_l, a)
    a_diag = _dot(cat(a_parts).astype(BF16), expand_r)
    a = jnp.where(same_blk, a_diag, a)
    o = o + _dot(a.astype(BF16), v)
    st_new = st * jnp.exp2(tot2) + _dot_tn(v, cat(kbar_parts).astype(BF16))
    return o, st_new


def _hgrn_kernel(q_ref, w2_ref, k_ref, v_ref, o_ref, st_ref, fac_ref, ck_ref, *, chunk, heads):
    tb = q_ref.shape[0]
    n_chunks = tb // chunk
    n_sub = chunk // SUB_BLOCK
    n_lvl = n_sub.bit_length() - 1

    @pl.when(pl.program_id(2) == 0)
    def _():
        st_ref[...] = jnp.zeros_like(st_ref)

    r = lax.broadcasted_iota(jnp.int32, (chunk, chunk), 0) // SUB_BLOCK
    cidx = lax.broadcasted_iota(jnp.int32, (chunk, chunk), 1)
    lane_blk = cidx // SUB_BLOCK
    same_blk = r == lane_blk
    lvl_masks = [(r >> (lvl + 1)) == (lane_blk >> (lvl + 1)) for lvl in range(n_lvl)]
    lane8 = lax.broadcasted_iota(jnp.int32, (VREG_ROWS, HEAD_DIM), 1)
    sub_row = lax.broadcasted_iota(jnp.int32, (VREG_ROWS, HEAD_DIM), 0)
    er = lax.broadcasted_iota(jnp.int32, (HEAD_DIM, chunk), 0)
    ec = lax.broadcasted_iota(jnp.int32, (HEAD_DIM, chunk), 1)
    expand_r = (ec % SUB_BLOCK == er).astype(BF16)
    jrow = lax.broadcasted_iota(jnp.int32, (n_sub, HEAD_DIM), 0)
    consts = (lane8, sub_row, lvl_masks, same_blk, expand_r, jrow)

    for ci in range(n_chunks):
        rows = slice(ci * chunk, (ci + 1) * chunk)
        for h in range(heads):
            lanes = slice(h * HEAD_DIM, (h + 1) * HEAD_DIM)
            q = q_ref[rows, lanes].astype(F32)
            k = k_ref[rows, lanes].astype(F32)
            o, st_new = _hgrn_chunk(q, k, w2_ref[rows, lanes], v_ref[rows, lanes], st_ref[h],
                                    fac_ref.at[ci * heads + h], ck_ref.at[ci * heads + h], consts)
            st_ref[h] = st_new
            o_ref[rows, lanes] = o.astype(o_ref.dtype)


def _hgrn_mix(q, w2, k, v, *, batch, seq, chunk=128, heads=4, tb=512):
    n_rows, d = q.shape
    n_heads = d // HEAD_DIM
    heads = min(heads, n_heads)
    tb = min(tb, seq)
    chunk = min(chunk, tb)
    n_sub = chunk // SUB_BLOCK
    assert seq % tb == 0 and tb % chunk == 0 and n_heads % heads == 0
    assert chunk % SUB_BLOCK == 0 and n_sub & (n_sub - 1) == 0
    nt = seq // tb
    n_fac = 3 + 2 * (n_sub.bit_length() - 1)
    blk = pl.BlockSpec((tb, heads * HEAD_DIM), lambda b, h, t: (b * nt + t, h))
    return pl.pallas_call(
        functools.partial(_hgrn_kernel, chunk=chunk, heads=heads),
        grid=(batch, n_heads // heads, nt),
        in_specs=[blk, blk, blk, blk],
        out_specs=blk,
        out_shape=jax.ShapeDtypeStruct((n_rows, d), BF16),
        scratch_shapes=[pltpu.VMEM((heads, HEAD_DIM, HEAD_DIM), F32),
                        pltpu.VMEM((tb // chunk * heads, n_fac, n_sub, HEAD_DIM), F32),
                        pltpu.VMEM((tb // chunk * heads, chunk, HEAD_DIM), F32)],
        compiler_params=pltpu.CompilerParams(
            dimension_semantics=("parallel", "parallel", "arbitrary"),
            vmem_limit_bytes=VMEM_LIMIT_BYTES),
        name="hgrn_mix",
    )(q, w2, k, v)


def _fgate_kernel(h_ref, wf_ref, bias_ref, f_ref, fk_ref, fq_ref, carry_ref):
    @pl.when(pl.program_id(1) == 0)
    def _():
        carry_ref[...] = jnp.zeros_like(carry_ref)

    z = _dot(h_ref[...], wf_ref[...]) + bias_ref[...]
    ls = jnp.minimum(z, 0.0) - jnp.log1p(jnp.exp(-jnp.abs(z)))
    tt = z.shape[0]
    r = lax.broadcasted_iota(jnp.int32, (tt, tt), 0)
    c = lax.broadcasted_iota(jnp.int32, (tt, tt), 1)
    tri = (c <= r).astype(BF16)
    hi, mid, lo = _split3(ls)
    cum = _dot(tri, hi) + _dot(tri, mid) + _dot(tri, lo)
    f = cum + carry_ref[...]
    f_ref[...] = f
    carry_ref[...] = f[tt - 1:tt]
    terms = _split3(f * (-LOG2E))
    for i, term in enumerate(terms):
        fk_ref[i] = term.astype(F32).T.astype(BF16)
    lane = lax.broadcasted_iota(jnp.int32, f.shape, 1)
    g = FOX_BIAS_GROUP
    t_hi, t_mid, t_lo = [-t.astype(F32) for t in terms]
    fq = jnp.where(lane < g, t_hi,
                   jnp.where(lane < 2 * g, pltpu.roll(t_mid, g, axis=1),
                             jnp.where(lane < 3 * g, pltpu.roll(t_lo, 2 * g, axis=1),
                                       jnp.where(lane < 3 * g + 3, 1.0, 0.0))))
    fq_ref[...] = fq.astype(BF16)


def _fgate(h, wf, bias, *, batch, seq, tt=256):
    n_rows, d = h.shape
    tt = min(tt, seq)
    nt = seq // tt
    return pl.pallas_call(
        _fgate_kernel,
        grid=(batch, nt),
        in_specs=[pl.BlockSpec((tt, d), lambda b, t: (b * nt + t, 0)),
                  pl.BlockSpec((d, HEAD_DIM), lambda b, t: (0, 0)),
                  pl.BlockSpec((1, HEAD_DIM), lambda b, t: (0, 0))],
        out_specs=[pl.BlockSpec((tt, HEAD_DIM), lambda b, t: (b * nt + t, 0)),
                   pl.BlockSpec((None, 3, HEAD_DIM, tt), lambda b, t: (b, 0, 0, t)),
                   pl.BlockSpec((tt, HEAD_DIM), lambda b, t: (b * nt + t, 0))],
        out_shape=[jax.ShapeDtypeStruct((n_rows, HEAD_DIM), F32),
                   jax.ShapeDtypeStruct((batch, 3, HEAD_DIM, seq), BF16),
                   jax.ShapeDtypeStruct((n_rows, HEAD_DIM), BF16)],
        scratch_shapes=[pltpu.VMEM((1, HEAD_DIM), F32)],
        compiler_params=pltpu.CompilerParams(
            dimension_semantics=("parallel", "arbitrary"),
            vmem_limit_bytes=VMEM_LIMIT_BYTES),
        name="fox_forget_gate",
    )(h, wf, bias)


def _fox_scores(qa_ref, kt_ref, s_ref, slot, h, cols):
    s_ref[slot] = _dot(qa_ref[h], kt_ref[h, :, cols])


def _fox_softmax(s_ref, p_ref, m_ref, l_ref, alpha_ref, slot, h, diag, first):
    tq, tk = s_ref.shape[1], s_ref.shape[2]
    for r in range(tq // FOX_ROWS):
        row0 = r * FOX_ROWS
        rows = pl.ds(row0, FOX_ROWS)
        n_chunks = (row0 + FOX_ROWS - 1) // HEAD_DIM + 1 if diag else tk // HEAD_DIM
        chunks = [s_ref[slot, rows, c * HEAD_DIM:(c + 1) * HEAD_DIM] for c in range(n_chunks)]
        if diag:
            last = n_chunks - 1
            rr = lax.broadcasted_iota(jnp.int32, (FOX_ROWS, HEAD_DIM), 0) + row0
            cc = lax.broadcasted_iota(jnp.int32, (FOX_ROWS, HEAD_DIM), 1) + last * HEAD_DIM
            chunks[last] = jnp.where(rr >= cc, chunks[last], -jnp.inf)
        mx = chunks[0]
        for ch in chunks[1:]:
            mx = jnp.maximum(mx, ch)
        m_new = jnp.max(mx, axis=-1, keepdims=True)
        if first:
            m_new = jnp.broadcast_to(m_new, (FOX_ROWS, HEAD_DIM))
        else:
            m_prev = m_ref[h, rows, :]
            m_new = jnp.maximum(m_prev, m_new)
            alpha = jnp.exp2(m_prev - m_new)
        psum = None
        for c, ch in enumerate(chunks):
            part = jnp.exp2(ch - m_new)
            psum = part if psum is None else psum + part
            p_ref[h, rows, c * HEAD_DIM:(c + 1) * HEAD_DIM] = part.astype(BF16)
        if n_chunks * HEAD_DIM < tk:
            p_ref[h, rows, n_chunks * HEAD_DIM:] = jnp.zeros((FOX_ROWS, tk - n_chunks * HEAD_DIM), BF16)
        row_sum = jnp.broadcast_to(jnp.sum(psum, axis=-1, keepdims=True), (FOX_ROWS, HEAD_DIM))
        if first:
            l_ref[h, rows, :] = row_sum
        else:
            l_ref[h, rows, :] = alpha * l_ref[h, rows, :] + row_sum
            alpha_ref[h, rows, :] = alpha
        m_ref[h, rows, :] = m_new


def _fox_values(p_ref, v_ref, alpha_ref, acc_ref, h, cols, first):
    lanes = slice(h * HEAD_DIM, (h + 1) * HEAD_DIM)
    pv = _dot(p_ref[h], v_ref[cols, lanes])
    acc_ref[h] = pv if first else alpha_ref[h] * acc_ref[h] + pv


def _fox_kernel(q_ref, ktr_ref, fk_ref, v_ref, fq_ref, fstart_ref, fend_ref, qkb_ref, o_ref,
                kt_ref, qa_ref, s_ref, p_ref, m_ref, l_ref, alpha_ref, acc_ref, *, heads):
    hp = pl.program_id(1)
    qi = pl.program_id(2)
    tq = q_ref.shape[0]

    @pl.when(qi == 0)
    def _():
        n_sel = 3 * FOX_BIAS_GROUP
        n_bias = fk_ref.shape[1]
        seq = kt_ref.shape[2]
        kt_ref[:, :HEAD_DIM, :] = ktr_ref[...]
        sel_row = lax.broadcasted_iota(jnp.int32, (n_sel, seq), 0) % FOX_BIAS_GROUP
        for h in range(heads):
            kt_ref[h, HEAD_DIM:HEAD_DIM + n_sel, :] = jnp.where(sel_row == hp * heads + h, 1.0, 0.0).astype(BF16)
        kt_ref[:, HEAD_DIM + n_sel:HEAD_DIM + n_sel + n_bias, :] = fk_ref[...]
        kt_ref[:, HEAD_DIM + n_sel + n_bias:, :] = jnp.zeros((heads, HEAD_DIM - n_sel - n_bias, seq), BF16)
    for h in range(heads):
        qa_ref[h, :, :HEAD_DIM] = q_ref[:, h * HEAD_DIM:(h + 1) * HEAD_DIM]
        qa_ref[h, :, HEAD_DIM:] = fq_ref[...]

    def block(kb, diag):
        cols = pl.ds(pl.multiple_of(kb * tq, tq), tq)
        for h in range(heads):
            _fox_scores(qa_ref, kt_ref, s_ref, h, h, cols)
        for h in range(heads):
            _fox_softmax(s_ref, p_ref, m_ref, l_ref, alpha_ref, h, h, diag, diag)
        for h in range(heads):
            _fox_values(p_ref, v_ref, alpha_ref, acc_ref, h, cols, diag)

    block(qi, True)
    lane8 = lax.broadcasted_iota(jnp.int32, fend_ref.shape, 1)
    kb8 = lax.broadcasted_iota(jnp.int32, fend_ref.shape, 0)
    gap = 2.0 * qkb_ref[...] + LOG2E * (fstart_ref[pl.ds(qi, 1), :] - fend_ref[...])
    mine = (lane8 >= hp * heads) & (lane8 < (hp + 1) * heads) & (kb8 < qi)
    live = jnp.where(mine & (gap >= -FOX_SKIP_BITS), 1.0, 0.0)
    n_live = jnp.sum(jnp.max(live, axis=1, keepdims=True)).astype(jnp.int32)

    def body(j, carry):
        block(qi - 1 - j, False)
        return carry

    lax.fori_loop(0, n_live, body, 0)

    for h in range(heads):
        lanes = slice(h * HEAD_DIM, (h + 1) * HEAD_DIM)
        o_ref[:, lanes] = (acc_ref[h] / l_ref[h]).astype(o_ref.dtype)


def _fox_mix(q, kt, fk_rows, v, fcol, fq_cols, qk_bound, *, batch, seq):
    n_rows, d = q.shape
    n_heads = d // HEAD_DIM
    heads = min(FOX_HEADS, n_heads)
    tq = min(FOX_TQ, seq)
    assert seq % tq == 0 and tq % FOX_ROWS == 0 and tq % HEAD_DIM == 0 and n_heads % heads == 0
    assert n_heads <= FOX_BIAS_GROUP
    nq = seq // tq
    n_pairs = n_heads // heads
    qblk = pl.BlockSpec((tq, heads * HEAD_DIM), lambda b, h, i: (b * nq + i, h))
    fblocks = fcol.reshape(batch, nq, tq, HEAD_DIM)
    fstart, fend = fblocks[:, :, 0, :], fblocks[:, :, tq - 1, :]
    fedge = pl.BlockSpec((None, nq, HEAD_DIM), lambda b, h, i: (b, 0, 0))
    qkb_row = jnp.full((1, HEAD_DIM), qk_bound, F32)
    return pl.pallas_call(
        functools.partial(_fox_kernel, heads=heads),
        grid=(batch, n_pairs, nq),
        in_specs=[qblk,
                  pl.BlockSpec((heads, HEAD_DIM, seq), lambda b, h, i: (b * n_pairs + h, 0, 0)),
                  pl.BlockSpec((heads, fk_rows.shape[1], seq), lambda b, h, i: (b * n_pairs + h, 0, 0)),
                  pl.BlockSpec((seq, heads * HEAD_DIM), lambda b, h, i: (b, h)),
                  pl.BlockSpec((tq, HEAD_DIM), lambda b, h, i: (b * nq + i, 0)),
                  fedge, fedge,
                  pl.BlockSpec((1, HEAD_DIM), lambda b, h, i: (0, 0))],
        out_specs=qblk,
        out_shape=jax.ShapeDtypeStruct((n_rows, d), BF16),
        scratch_shapes=[pltpu.VMEM((heads, 2 * HEAD_DIM, seq), BF16),
                        pltpu.VMEM((heads, tq, 2 * HEAD_DIM), BF16),
                        pltpu.VMEM((heads, tq, tq), F32),
                        pltpu.VMEM((heads, tq, tq), BF16),
                        pltpu.VMEM((heads, tq, HEAD_DIM), F32),
                        pltpu.VMEM((heads, tq, HEAD_DIM), F32),
                        pltpu.VMEM((heads, tq, HEAD_DIM), F32),
                        pltpu.VMEM((heads, tq, HEAD_DIM), F32)],
        compiler_params=pltpu.CompilerParams(
            dimension_semantics=("parallel", "parallel", "arbitrary"),
            vmem_limit_bytes=VMEM_LIMIT_BYTES),
        name="fox_attention",
    )(q, kt, fk_rows, v, fq_cols, fstart, fend, qkb_row)


def _fox_bias_rows(fk3, *, batch, seq, n_heads):
    rows = fk3[:, :, :n_heads, :].transpose(0, 2, 1, 3)
    rows = jnp.pad(rows, ((0, 0), (0, 0), (0, BF16_SUBLANES - 3), (0, 0)))
    return rows.reshape(batch * n_heads, BF16_SUBLANES, seq)


def kernel(x, a_norm_w, a_w_in, a_lb_logits, a_out_norm_w, a_w_out, kv_norm_w, kv_w, kv_f_bias, k_norm_w,
           b_norm_w, b_w_in, b_q_norm_w, b_out_norm_w, b_w_out):
    batch, seq, d = x.shape
    n_heads = d // HEAD_DIM
    n_a = a_w_in.shape[0]
    n_b = b_w_in.shape[0]
    xr = x.reshape(batch * seq, d)

    a_w_out_b = a_w_out.astype(BF16)
    b_w_out_b = b_w_out.astype(BF16)
    wf_b = jnp.pad(kv_w[:, 2 * d:], ((0, 0), (0, HEAD_DIM - n_heads))).astype(BF16)
    f_bias = jnp.pad(kv_f_bias.astype(F32), (0, HEAD_DIM - n_heads)).reshape(1, HEAD_DIM)

    lb_all = jnp.cumsum(jax.nn.softmax(a_lb_logits.astype(F32), axis=0), axis=0)
    lb_all = lb_all - lb_all[0:1]
    ones_row = jnp.ones((1, d), F32)

    h = _norm(xr, a_norm_w[0])
    for layer in range(n_a):
        q, w2, k, v, sg = _proj(
            h,
            [(a_w_in, layer, 0, ones_row, "cast", (BF16,)),
             (a_w_in, layer, d, lb_all[layer].reshape(1, d), "hgate", (F32, BF16)),
             (a_w_in, layer, 2 * d, ones_row, "cast", (BF16,)),
             (a_w_in, layer, 3 * d, ones_row, "silu", (BF16,))],
            name="hgrn_in_proj", seq=seq, tm=1024, tn=256)
        og = _hgrn_mix(q, w2, k, v, batch=batch, seq=seq)
        if layer + 1 < n_a:
            xr, (h,) = _out_proj(og, sg, a_out_norm_w[layer], a_w_out_b, layer, xr, [a_norm_w[layer + 1]])
        else:
            xr, (h_kv, h) = _out_proj(og, sg, a_out_norm_w[layer], a_w_out_b, layer, xr, [kv_norm_w, b_norm_w[0]])

    k_norm_row = jnp.tile(k_norm_w.astype(F32), n_heads).reshape(1, d)
    kt, vv = _proj(
        h_kv,
        [(kv_w, 0, 0, k_norm_row, "headnorm_t", (BF16,)),
         (kv_w, 0, d, ones_row, "cast", (BF16,))],
        name="fox_kv_proj", seq=seq)
    fcol, fk3, fq_cols = _fgate(h_kv, wf_b, f_bias, batch=batch, seq=seq)
    fk_rows = _fox_bias_rows(fk3, batch=batch, seq=seq, n_heads=n_heads)

    for j in range(n_b):
        q_norm_row = jnp.tile(b_q_norm_w[j].astype(F32), n_heads).reshape(1, d)
        q, sg = _proj(
            h,
            [(b_w_in, j, 0, q_norm_row, "headnorm", (BF16,)),
             (b_w_in, j, d, ones_row, "silu", (BF16,))],
            scale=HEAD_DIM ** -0.5 * LOG2E, name="fox_in_proj", seq=seq)
        qk_bound = (HEAD_DIM * jnp.max(jnp.abs(b_q_norm_w[j])) * jnp.max(jnp.abs(k_norm_w))
                    * (HEAD_DIM ** -0.5 * LOG2E * 1.02))
        og = _fox_mix(q, kt, fk_rows, vv, fcol, fq_cols, qk_bound, batch=batch, seq=seq)
        if j + 1 < n_b:
            xr, (h,) = _out_proj(og, sg, b_out_norm_w[j], b_w_out_b, j, xr, [b_norm_w[j + 1]])
        else:
            xr, _ = _out_proj(og, sg, b_out_norm_w[j], b_w_out_b, j, xr, [])

    return xr.reshape(batch, seq, d)
```

```python
import functools

import jax
import jax.numpy as jnp
from jax import lax
from jax.experimental import pallas as pl
from jax.experimental.pallas import tpu as pltpu

HEAD_DIM = 128
VREG_ROWS = 8
SUB_BLOCK = 8
BF16_SUBLANES = 16
EPS = 1e-6
VMEM_LIMIT_BYTES = 56 * 1024 * 1024
LOG2E = 1.4426950408889634
FOX_TQ = 512
FOX_ROWS = 64
FOX_HEADS = 4
FOX_BIAS_GROUP = 32
FOX_SKIP_BITS = 152.0

F32 = jnp.float32
BF16 = jnp.bfloat16


def _dot(a, b):
    return jnp.dot(a, b, preferred_element_type=F32)


def _dot_nt(a, b):
    return lax.dot_general(a, b, (((1,), (1,)), ((), ())), preferred_element_type=F32)


def _dot_tn(a, b):
    return lax.dot_general(a, b, (((0,), (0,)), ((), ())), preferred_element_type=F32)


def _split3(x):
    hi = x.astype(BF16)
    r1 = x - hi.astype(F32)
    mid = r1.astype(BF16)
    lo = (r1 - mid.astype(F32)).astype(BF16)
    return hi, mid, lo


def _scan_sub_block(x):
    n, c = x.shape
    x3 = x.reshape(n // VREG_ROWS, VREG_ROWS, c)
    pos = lax.broadcasted_iota(jnp.int32, x3.shape, 1)
    shift = 1
    while shift < VREG_ROWS:
        x3 = x3 + jnp.where(pos >= shift, pltpu.roll(x3, shift, axis=1), 0.0)
        shift *= 2
    if SUB_BLOCK == 2 * VREG_ROWS:
        tile = lax.broadcasted_iota(jnp.int32, x3.shape, 0)
        carry = jnp.roll(jnp.broadcast_to(x3[:, VREG_ROWS - 1:, :], x3.shape), 1, axis=0)
        x3 = x3 + jnp.where(tile % 2 == 1, carry, 0.0)
    return x3.reshape(n, c)


def _rms_rows(x, w):
    ms = jnp.mean(x * x, axis=-1, keepdims=True)
    return x * lax.rsqrt(ms + EPS) * w


def _proj_kernel(*refs, kinds, scale):
    n = len(kinds)
    h_ref = refs[0]
    w_refs = refs[1:1 + n]
    aux_refs = refs[1 + n:1 + 2 * n]
    out_refs = refs[1 + 2 * n:-1]
    wb_ref = refs[-1]

    @pl.when(pl.program_id(1) == 0)
    def _():
        for s in range(n):
            wb_ref[s] = w_refs[s][...].astype(BF16)

    h = h_ref[...]
    oi = 0
    for s, kind in enumerate(kinds):
        acc = _dot(h, wb_ref[s])
        aux = aux_refs[s][...]
        if kind == "cast":
            out_refs[oi][...] = acc.astype(out_refs[oi].dtype)
            oi += 1
        elif kind == "silu":
            out_refs[oi][...] = (acc / (1.0 + jnp.exp(-acc))).astype(out_refs[oi].dtype)
            oi += 1
        elif kind in ("headnorm", "headnorm_t"):
            tn = acc.shape[1]
            for c in range(tn // HEAD_DIM):
                sl = slice(c * HEAD_DIM, (c + 1) * HEAD_DIM)
                a = acc[:, sl]
                ms = jnp.mean(a * a, axis=-1, keepdims=True)
                y = a * lax.rsqrt(ms + EPS) * aux[:, sl]
                if scale != 1.0:
                    y = y * scale
                if kind == "headnorm_t":
                    out_refs[oi][c] = y.T.astype(out_refs[oi].dtype)
                else:
                    out_refs[oi][:, sl] = y.astype(out_refs[oi].dtype)
            oi += 1
        elif kind == "hgate":
            lb = aux
            e = jnp.exp2(jnp.abs(acc) * (-LOG2E))
            one_e = 1.0 + e
            log2_sig = jnp.minimum(acc, 0.0) * LOG2E - jnp.log2(one_e)
            a = jnp.log2(lb)
            c = jnp.log2(1.0 - lb) + log2_sig
            g2 = jnp.maximum(a, c) + jnp.log2(1.0 + jnp.exp2(-jnp.abs(a - c)))
            r = 1.0 / one_e
            sig_neg = jnp.where(acc >= 0.0, e * r, r)
            out_refs[oi][...] = _scan_sub_block(g2).astype(out_refs[oi].dtype)
            out_refs[oi + 1][...] = ((1.0 - lb) * sig_neg).astype(out_refs[oi + 1].dtype)
            oi += 2
        else:
            raise ValueError(kind)


def _proj(h, streams, *, name, seq, scale=1.0, tm=1024, tn=512):
    n_rows, d = h.shape
    n_cols = streams[0][3].shape[1]
    tm = min(tm, seq)
    tn = min(tn, n_cols)
    assert seq % tm == 0 and n_rows % seq == 0 and n_cols % tn == 0 and tm % SUB_BLOCK == 0
    kinds = tuple(s[4] for s in streams)
    t_tiles = seq // tm
    heads_per_tile = tn // HEAD_DIM
    n_head_tiles = n_cols // tn

    in_specs = [pl.BlockSpec((tm, d), lambda j, i: (i, 0))]
    args = [h]
    for (w, layer, off, aux, kind, _) in streams:
        assert off % tn == 0
        if w.ndim == 2:
            in_specs.append(pl.BlockSpec((d, tn), functools.partial(lambda j, i, o: (0, j + o), o=off // tn)))
        else:
            in_specs.append(pl.BlockSpec((None, d, tn),
                                         functools.partial(lambda j, i, l, o: (l, 0, j + o), l=layer, o=off // tn)))
        args.append(w)
    for (w, layer, off, aux, kind, _) in streams:
        in_specs.append(pl.BlockSpec((1, tn), lambda j, i: (0, j)))
        args.append(aux)
    out_shapes, out_specs = [], []
    for (w, layer, off, aux, kind, dts) in streams:
        for dt in dts:
            if kind == "headnorm_t":
                out_shapes.append(jax.ShapeDtypeStruct((n_rows // seq * n_cols // HEAD_DIM, HEAD_DIM, seq), dt))
                out_specs.append(pl.BlockSpec(
                    (heads_per_tile, HEAD_DIM, tm),
                    lambda j, i: ((i // t_tiles) * n_head_tiles + j, 0, i % t_tiles)))
            else:
                out_shapes.append(jax.ShapeDtypeStruct((n_rows, n_cols), dt))
                out_specs.append(pl.BlockSpec((tm, tn), lambda j, i: (i, j)))

    return pl.pallas_call(
        functools.partial(_proj_kernel, kinds=kinds, scale=scale),
        grid=(n_cols // tn, n_rows // tm),
        in_specs=in_specs,
        out_specs=out_specs,
        out_shape=out_shapes,
        scratch_shapes=[pltpu.VMEM((len(streams), d, tn), BF16)],
        compiler_params=pltpu.CompilerParams(
            dimension_semantics=("parallel", "arbitrary"),
            vmem_limit_bytes=VMEM_LIMIT_BYTES),
        name=name,
    )(*args)


def _norm_kernel(x_ref, nw_ref, h_ref):
    h_ref[...] = _rms_rows(x_ref[...], nw_ref[...]).astype(h_ref.dtype)


def _norm(x, norm_w, *, tm=1024):
    n_rows, d = x.shape
    tm = min(tm, n_rows)
    assert n_rows % tm == 0
    return pl.pallas_call(
        _norm_kernel,
        grid=(n_rows // tm,),
        in_specs=[pl.BlockSpec((tm, d), lambda i: (i, 0)),
                  pl.BlockSpec((1, d), lambda i: (0, 0))],
        out_specs=pl.BlockSpec((tm, d), lambda i: (i, 0)),
        out_shape=jax.ShapeDtypeStruct((n_rows, d), BF16),
        compiler_params=pltpu.CompilerParams(
            dimension_semantics=("parallel",),
            vmem_limit_bytes=VMEM_LIMIT_BYTES),
        name="input_norm",
    )(x, norm_w.reshape(1, d))


def _out_proj_kernel(o_ref, sg_ref, onw_ref, w_ref, x_ref, *refs):
    n_next = (len(refs) - 1) // 2
    nw_refs, y_ref, h_refs = refs[:n_next], refs[n_next], refs[n_next + 1:]
    gated = []
    for c in range(o_ref.shape[1] // HEAD_DIM):
        lanes = slice(c * HEAD_DIM, (c + 1) * HEAD_DIM)
        o = o_ref[:, lanes].astype(F32)
        on = o * lax.rsqrt(jnp.mean(o * o, axis=-1, keepdims=True) + EPS) * onw_ref[:, lanes]
        gated.append((on * sg_ref[:, lanes].astype(F32)).astype(BF16))
    y = x_ref[...] + _dot(jnp.concatenate(gated, axis=1), w_ref[...])
    y_ref[...] = y
    if n_next:
        yn = y * lax.rsqrt(jnp.mean(y * y, axis=-1, keepdims=True) + EPS)
        for nw_ref, h_ref in zip(nw_refs, h_refs):
            h_ref[...] = (yn * nw_ref[...]).astype(h_ref.dtype)


def _out_proj(o, sg, out_norm_w, w3d, layer, x, next_norm_ws, *, tm=512):
    n_rows, d = x.shape
    tm = min(tm, n_rows)
    assert n_rows % tm == 0
    rows = pl.BlockSpec((tm, d), lambda i: (i, 0))
    gain = pl.BlockSpec((1, d), lambda i: (0, 0))
    n_next = len(next_norm_ws)
    outs = pl.pallas_call(
        _out_proj_kernel,
        grid=(n_rows // tm,),
        in_specs=[rows, rows, gain,
                  pl.BlockSpec((None, d, d), functools.partial(lambda i, l: (l, 0, 0), l=layer),
                               pipeline_mode=pl.Buffered(1)),
                  rows] + [gain] * n_next,
        out_specs=[rows] * (1 + n_next),
        out_shape=[jax.ShapeDtypeStruct((n_rows, d), F32)]
                  + [jax.ShapeDtypeStruct((n_rows, d), BF16)] * n_next,
        compiler_params=pltpu.CompilerParams(
            dimension_semantics=("parallel",),
            vmem_limit_bytes=VMEM_LIMIT_BYTES),
        name="out_proj",
    )(o, sg, out_norm_w.reshape(1, d), w3d, x, *[w.reshape(1, d) for w in next_norm_ws])
    return outs[0], list(outs[1:])


def _excl_prefix_rows(x):
    pos = lax.broadcasted_iota(jnp.int32, x.shape, 0)
    inc = x
    shift = 1
    while shift < x.shape[0]:
        inc = inc + jnp.where(pos >= shift, pltpu.roll(inc, shift, axis=0), 0.0)
        shift *= 2
    return inc - x


def _level_mid(bs2, lvl):
    n = bs2.shape[0]
    group = 2 << lvl
    if group >= VREG_ROWS:
        tiles = [jnp.broadcast_to(bs2[(t0 // group) * group + group // 2:][:1], (VREG_ROWS, HEAD_DIM))
                 for t0 in range(0, n, VREG_ROWS)]
        return jnp.concatenate(tiles, axis=0)
    x3 = bs2.reshape(n // VREG_ROWS, VREG_ROWS, HEAD_DIM)
    pos = lax.broadcasted_iota(jnp.int32, x3.shape, 1)
    out = None
    for g0 in range(0, VREG_ROWS, group):
        piece = jnp.broadcast_to(x3[:, g0 + group // 2:g0 + group // 2 + 1, :], x3.shape)
        out = piece if out is None else jnp.where(pos >= g0, piece, out)
    return out.reshape(n, HEAD_DIM)


def _rows_of(x, idx):
    return jnp.concatenate([x[i:i + 1] for i in idx], axis=0)


def _hgrn_chunk(q, k, w2, v, st, fac_ref, ck_ref, consts):
    lane8, sub_row, lvl_masks, same_blk, expand_r, jrow = consts
    c = q.shape[0]
    n_sub = c // SUB_BLOCK
    n_lvl = n_sub.bit_length() - 1

    gt2 = _rows_of(w2, [j * SUB_BLOCK + SUB_BLOCK - 1 for j in range(n_sub)])
    bs2 = _excl_prefix_rows(gt2)
    be2 = bs2 + gt2
    tot2 = be2[n_sub - 1:n_sub]
    fac_ref[0] = gt2
    fac_ref[1] = jnp.exp2(bs2)
    fac_ref[2] = jnp.exp2(tot2 - be2)
    for lvl in range(n_lvl):
        mid = _level_mid(bs2, lvl)
        upper = ((jrow >> lvl) & 1) == 1
        fac_ref[3 + 2 * lvl] = jnp.exp2(jnp.where(upper, bs2 - mid, -jnp.inf))
        fac_ref[4 + 2 * lvl] = jnp.exp2(jnp.where(upper, -jnp.inf, mid - be2))
    ck_ref[...] = jnp.log2(k) - w2

    qs_parts, kbar_parts, a_parts = [], [], []
    ql_parts = [[] for _ in range(n_lvl)]
    kl_parts = [[] for _ in range(n_lvl)]
    for j in range(n_sub):
        r0 = j * SUB_BLOCK
        sl = slice(r0, r0 + SUB_BLOCK)
        w_j, q_j = w2[sl], q[sl]
        qt = q_j * jnp.exp2(w_j)
        kh = k[sl] * jnp.exp2(fac_ref[0, j:j + 1, :] - w_j)
        qs_parts.append(qt * fac_ref[1, j:j + 1, :])
        kbar_parts.append(kh * fac_ref[2, j:j + 1, :])
        for lvl in range(n_lvl):
            ql_parts[lvl].append(qt * fac_ref[3 + 2 * lvl, j:j + 1, :])
            kl_parts[lvl].append(kh * fac_ref[4 + 2 * lvl, j:j + 1, :])

        for t0 in range(0, SUB_BLOCK, VREG_ROWS):
            w_t, q_t = w_j[t0:t0 + VREG_ROWS], q_j[t0:t0 + VREG_ROWS]
            a_t = jnp.zeros((VREG_ROWS, HEAD_DIM), F32)
            for s in range(t0 + VREG_ROWS):
                cs = ck_ref[r0 + s:r0 + s + 1, :]
                col = jnp.sum(q_t * jnp.exp2(w_t + cs), axis=-1, keepdims=True)
                a_t = jnp.where(lane8 == s, col, a_t)
            a_parts.append(jnp.where(lane8 <= sub_row + t0, a_t, 0.0))

    cat = lambda parts: jnp.concatenate(parts, axis=0)
    o = _dot_nt(cat(qs_parts).astype(BF16), st.astype(BF16))
    a = _dot_nt(cat(ql_parts[n_lvl - 1]).astype(BF16), cat(kl_parts[n_lvl - 1]).astype(BF16))
    for lvl in range(n_lvl - 2, -1, -1):
        a_l = _dot_nt(cat(ql_parts[lvl]).astype(BF16), cat(kl_parts[lvl]).astype(BF16))
        a = jnp.where(lvl_masks[lvl], a_l, a)
    a_diag = _dot(cat(a_parts).astype(BF16), expand_r)
    a = jnp.where(same_blk, a_diag, a)
    o = o + _dot(a.astype(BF16), v)
    st_new = st * jnp.exp2(tot2) + _dot_tn(v, cat(kbar_parts).astype(BF16))
    return o, st_new


def _hgrn_kernel(q_ref, w2_ref, k_ref, v_ref, o_ref, st_ref, fac_ref, ck_ref, *, chunk, heads):
    tb = q_ref.shape[0]
    n_chunks = tb // chunk
    n_sub = chunk // SUB_BLOCK
    n_lvl = n_sub.bit_length() - 1

    @pl.when(pl.program_id(2) == 0)
    def _():
        st_ref[...] = jnp.zeros_like(st_ref)

    r = lax.broadcasted_iota(jnp.int32, (chunk, chunk), 0) // SUB_BLOCK
    cidx = lax.broadcasted_iota(jnp.int32, (chunk, chunk), 1)
    lane_blk = cidx // SUB_BLOCK
    same_blk = r == lane_blk
    lvl_masks = [(r >> (lvl + 1)) == (lane_blk >> (lvl + 1)) for lvl in range(n_lvl)]
    lane8 = lax.broadcasted_iota(jnp.int32, (VREG_ROWS, HEAD_DIM), 1)
    sub_row = lax.broadcasted_iota(jnp.int32, (VREG_ROWS, HEAD_DIM), 0)
    er = lax.broadcasted_iota(jnp.int32, (HEAD_DIM, chunk), 0)
    ec = lax.broadcasted_iota(jnp.int32, (HEAD_DIM, chunk), 1)
    expand_r = (ec % SUB_BLOCK == er).astype(BF16)
    jrow = lax.broadcasted_iota(jnp.int32, (n_sub, HEAD_DIM), 0)
    consts = (lane8, sub_row, lvl_masks, same_blk, expand_r, jrow)

    for ci in range(n_chunks):
        rows = slice(ci * chunk, (ci + 1) * chunk)
        for h in range(heads):
            lanes = slice(h * HEAD_DIM, (h + 1) * HEAD_DIM)
            q = q_ref[rows, lanes].astype(F32)
            k = k_ref[rows, lanes].astype(F32)
            o, st_new = _hgrn_chunk(q, k, w2_ref[rows, lanes], v_ref[rows, lanes], st_ref[h],
                                    fac_ref.at[ci * heads + h], ck_ref.at[ci * heads + h], consts)
            st_ref[h] = st_new
            o_ref[rows, lanes] = o.astype(o_ref.dtype)


def _hgrn_mix(q, w2, k, v, *, batch, seq, chunk=128, heads=8, tb=512):
    n_rows, d = q.shape
    n_heads = d // HEAD_DIM
    heads = min(heads, n_heads)
    tb = min(tb, seq)
    chunk = min(chunk, tb)
    n_sub = chunk // SUB_BLOCK
    assert seq % tb == 0 and tb % chunk == 0 and n_heads % heads == 0
    assert chunk % SUB_BLOCK == 0 and n_sub & (n_sub - 1) == 0
    nt = seq // tb
    n_fac = 3 + 2 * (n_sub.bit_length() - 1)
    blk = pl.BlockSpec((tb, heads * HEAD_DIM), lambda b, h, t: (b * nt + t, h))
    return pl.pallas_call(
        functools.partial(_hgrn_kernel, chunk=chunk, heads=heads),
        grid=(batch, n_heads // heads, nt),
        in_specs=[blk, blk, blk, blk],
        out_specs=blk,
        out_shape=jax.ShapeDtypeStruct((n_rows, d), BF16),
        scratch_shapes=[pltpu.VMEM((heads, HEAD_DIM, HEAD_DIM), F32),
                        pltpu.VMEM((tb // chunk * heads, n_fac, n_sub, HEAD_DIM), F32),
                        pltpu.VMEM((tb // chunk * heads, chunk, HEAD_DIM), F32)],
        compiler_params=pltpu.CompilerParams(
            dimension_semantics=("parallel", "parallel", "arbitrary"),
            vmem_limit_bytes=VMEM_LIMIT_BYTES),
        name="hgrn_mix",
    )(q, w2, k, v)


def _fgate_kernel(h_ref, wf_ref, bias_ref, f_ref, fk_ref, fq_ref, carry_ref):
    @pl.when(pl.program_id(1) == 0)
    def _():
        carry_ref[...] = jnp.zeros_like(carry_ref)

    z = _dot(h_ref[...], wf_ref[...]) + bias_ref[...]
    ls = jnp.minimum(z, 0.0) - jnp.log1p(jnp.exp(-jnp.abs(z)))
    tt = z.shape[0]
    r = lax.broadcasted_iota(jnp.int32, (tt, tt), 0)
    c = lax.broadcasted_iota(jnp.int32, (tt, tt), 1)
    tri = (c <= r).astype(BF16)
    hi, mid, lo = _split3(ls)
    cum = _dot(tri, hi) + _dot(tri, mid) + _dot(tri, lo)
    f = cum + carry_ref[...]
    f_ref[...] = f
    carry_ref[...] = f[tt - 1:tt]
    terms = _split3(f * (-LOG2E))
    for i, term in enumerate(terms):
        fk_ref[i] = term.astype(F32).T.astype(BF16)
    lane = lax.broadcasted_iota(jnp.int32, f.shape, 1)
    g = FOX_BIAS_GROUP
    t_hi, t_mid, t_lo = [-t.astype(F32) for t in terms]
    fq = jnp.where(lane < g, t_hi,
                   jnp.where(lane < 2 * g, pltpu.roll(t_mid, g, axis=1),
                             jnp.where(lane < 3 * g, pltpu.roll(t_lo, 2 * g, axis=1),
                                       jnp.where(lane < 3 * g + 3, 1.0, 0.0))))
    fq_ref[...] = fq.astype(BF16)


def _fgate(h, wf, bias, *, batch, seq, tt=256):
    n_rows, d = h.shape
    tt = min(tt, seq)
    nt = seq // tt
    return pl.pallas_call(
        _fgate_kernel,
        grid=(batch, nt),
        in_specs=[pl.BlockSpec((tt, d), lambda b, t: (b * nt + t, 0)),
                  pl.BlockSpec((d, HEAD_DIM), lambda b, t: (0, 0)),
                  pl.BlockSpec((1, HEAD_DIM), lambda b, t: (0, 0))],
        out_specs=[pl.BlockSpec((tt, HEAD_DIM), lambda b, t: (b * nt + t, 0)),
                   pl.BlockSpec((None, 3, HEAD_DIM, tt), lambda b, t: (b, 0, 0, t)),
                   pl.BlockSpec((tt, HEAD_DIM), lambda b, t: (b * nt + t, 0))],
        out_shape=[jax.ShapeDtypeStruct((n_rows, HEAD_DIM), F32),
                   jax.ShapeDtypeStruct((batch, 3, HEAD_DIM, seq), BF16),
                   jax.ShapeDtypeStruct((n_rows, HEAD_DIM), BF16)],
        scratch_shapes=[pltpu.VMEM((1, HEAD_DIM), F32)],
        compiler_params=pltpu.CompilerParams(
            dimension_semantics=("parallel", "arbitrary"),
            vmem_limit_bytes=VMEM_LIMIT_BYTES),
        name="fox_forget_gate",
    )(h, wf, bias)


def _fox_scores(qa_ref, kt_ref, s_ref, slot, h, cols):
    s_ref[slot] = _dot(qa_ref[h], kt_ref[h, :, cols])


def _fox_softmax(s_ref, p_ref, m_ref, l_ref, alpha_ref, slot, h, diag, first):
    tq, tk = s_ref.shape[1], s_ref.shape[2]
    for r in range(tq // FOX_ROWS):
        row0 = r * FOX_ROWS
        rows = pl.ds(row0, FOX_ROWS)
        n_chunks = (row0 + FOX_ROWS - 1) // HEAD_DIM + 1 if diag else tk // HEAD_DIM
        chunks = [s_ref[slot, rows, c * HEAD_DIM:(c + 1) * HEAD_DIM] for c in range(n_chunks)]
        if diag:
            last = n_chunks - 1
            rr = lax.broadcasted_iota(jnp.int32, (FOX_ROWS, HEAD_DIM), 0) + row0
            cc = lax.broadcasted_iota(jnp.int32, (FOX_ROWS, HEAD_DIM), 1) + last * HEAD_DIM
            chunks[last] = jnp.where(rr >= cc, chunks[last], -jnp.inf)
        mx = chunks[0]
        for ch in chunks[1:]:
            mx = jnp.maximum(mx, ch)
        m_new = jnp.max(mx, axis=-1, keepdims=True)
        if first:
            m_new = jnp.broadcast_to(m_new, (FOX_ROWS, HEAD_DIM))
        else:
            m_prev = m_ref[h, rows, :]
            m_new = jnp.maximum(m_prev, m_new)
            alpha = jnp.exp2(m_prev - m_new)
        psum = None
        for c, ch in enumerate(chunks):
            part = jnp.exp2(ch - m_new)
            psum = part if psum is None else psum + part
            p_ref[h, rows, c * HEAD_DIM:(c + 1) * HEAD_DIM] = part.astype(BF16)
        if n_chunks * HEAD_DIM < tk:
            p_ref[h, rows, n_chunks * HEAD_DIM:] = jnp.zeros((FOX_ROWS, tk - n_chunks * HEAD_DIM), BF16)
        row_sum = jnp.broadcast_to(jnp.sum(psum, axis=-1, keepdims=True), (FOX_ROWS, HEAD_DIM))
        if first:
            l_ref[h, rows, :] = row_sum
        else:
            l_ref[h, rows, :] = alpha * l_ref[h, rows, :] + row_sum
            alpha_ref[h, rows, :] = alpha
        m_ref[h, rows, :] = m_new


def _fox_values(p_ref, v_ref, alpha_ref, acc_ref, h, cols, first):
    lanes = slice(h * HEAD_DIM, (h + 1) * HEAD_DIM)
    pv = _dot(p_ref[h], v_ref[cols, lanes])
    acc_ref[h] = pv if first else alpha_ref[h] * acc_ref[h] + pv


def _fox_kernel(q_ref, ktr_ref, fk_ref, v_ref, fq_ref, fstart_ref, fend_ref, qkb_ref, o_ref,
                kt_ref, qa_ref, s_ref, p_ref, m_ref, l_ref, alpha_ref, acc_ref, *, heads):
    hp = pl.program_id(1)
    qi = pl.program_id(2)
    tq = q_ref.shape[0]

    @pl.when(qi == 0)
    def _():
        n_sel = 3 * FOX_BIAS_GROUP
        n_bias = fk_ref.shape[1]
        seq = kt_ref.shape[2]
        kt_ref[:, :HEAD_DIM, :] = ktr_ref[...]
        sel_row = lax.broadcasted_iota(jnp.int32, (n_sel, seq), 0) % FOX_BIAS_GROUP
        for h in range(heads):
            kt_ref[h, HEAD_DIM:HEAD_DIM + n_sel, :] = jnp.where(sel_row == hp * heads + h, 1.0, 0.0).astype(BF16)
        kt_ref[:, HEAD_DIM + n_sel:HEAD_DIM + n_sel + n_bias, :] = fk_ref[...]
        kt_ref[:, HEAD_DIM + n_sel + n_bias:, :] = jnp.zeros((heads, HEAD_DIM - n_sel - n_bias, seq), BF16)
    for h in range(heads):
        qa_ref[h, :, :HEAD_DIM] = q_ref[:, h * HEAD_DIM:(h + 1) * HEAD_DIM]
        qa_ref[h, :, HEAD_DIM:] = fq_ref[...]

    def block(kb, diag):
        cols = pl.ds(pl.multiple_of(kb * tq, tq), tq)
        for h in range(heads):
            _fox_scores(qa_ref, kt_ref, s_ref, h, h, cols)
        for h in range(heads):
            _fox_softmax(s_ref, p_ref, m_ref, l_ref, alpha_ref, h, h, diag, diag)
        for h in range(heads):
            _fox_values(p_ref, v_ref, alpha_ref, acc_ref, h, cols, diag)

    block(qi, True)
    lane8 = lax.broadcasted_iota(jnp.int32, fend_ref.shape, 1)
    kb8 = lax.broadcasted_iota(jnp.int32, fend_ref.shape, 0)
    gap = 2.0 * qkb_ref[...] + LOG2E * (fstart_ref[pl.ds(qi, 1), :] - fend_ref[...])
    mine = (lane8 >= hp * heads) & (lane8 < (hp + 1) * heads) & (kb8 < qi)
    live = jnp.where(mine & (gap >= -FOX_SKIP_BITS), 1.0, 0.0)
    n_live = jnp.sum(jnp.max(live, axis=1, keepdims=True)).astype(jnp.int32)

    def body(j, carry):
        block(qi - 1 - j, False)
        return carry

    lax.fori_loop(0, n_live, body, 0)

    for h in range(heads):
        lanes = slice(h * HEAD_DIM, (h + 1) * HEAD_DIM)
        o_ref[:, lanes] = (acc_ref[h] / l_ref[h]).astype(o_ref.dtype)


def _fox_mix(q, kt, fk_rows, v, fcol, fq_cols, qk_bound, *, batch, seq):
    n_rows, d = q.shape
    n_heads = d // HEAD_DIM
    heads = min(FOX_HEADS, n_heads)
    tq = min(FOX_TQ, seq)
    assert seq % tq == 0 and tq % FOX_ROWS == 0 and tq % HEAD_DIM == 0 and n_heads % heads == 0
    assert n_heads <= FOX_BIAS_GROUP
    nq = seq // tq
    n_pairs = n_heads // heads
    qblk = pl.BlockSpec((tq, heads * HEAD_DIM), lambda b, h, i: (b * nq + i, h))
    fblocks = fcol.reshape(batch, nq, tq, HEAD_DIM)
    fstart, fend = fblocks[:, :, 0, :], fblocks[:, :, tq - 1, :]
    fedge = pl.BlockSpec((None, nq, HEAD_DIM), lambda b, h, i: (b, 0, 0))
    qkb_row = jnp.full((1, HEAD_DIM), qk_bound, F32)
    return pl.pallas_call(
        functools.partial(_fox_kernel, heads=heads),
        grid=(batch, n_pairs, nq),
        in_specs=[qblk,
                  pl.BlockSpec((heads, HEAD_DIM, seq), lambda b, h, i: (b * n_pairs + h, 0, 0)),
                  pl.BlockSpec((heads, fk_rows.shape[1], seq), lambda b, h, i: (b * n_pairs + h, 0, 0)),
                  pl.BlockSpec((seq, heads * HEAD_DIM), lambda b, h, i: (b, h)),
                  pl.BlockSpec((tq, HEAD_DIM), lambda b, h, i: (b * nq + i, 0)),
                  fedge, fedge,
                  pl.BlockSpec((1, HEAD_DIM), lambda b, h, i: (0, 0))],
        out_specs=qblk,
        out_shape=jax.ShapeDtypeStruct((n_rows, d), BF16),
        scratch_shapes=[pltpu.VMEM((heads, 2 * HEAD_DIM, seq), BF16),
                        pltpu.VMEM((heads, tq, 2 * HEAD_DIM), BF16),
                        pltpu.VMEM((heads, tq, tq), F32),
                        pltpu.VMEM((heads, tq, tq), BF16),
                        pltpu.VMEM((heads, tq, HEAD_DIM), F32),
                        pltpu.VMEM((heads, tq, HEAD_DIM), F32),
                        pltpu.VMEM((heads, tq, HEAD_DIM), F32),
                        pltpu.VMEM((heads, tq, HEAD_DIM), F32)],
        compiler_params=pltpu.CompilerParams(
            dimension_semantics=("parallel", "parallel", "arbitrary"),
            vmem_limit_bytes=VMEM_LIMIT_BYTES),
        name="fox_attention",
    )(q, kt, fk_rows, v, fq_cols, fstart, fend, qkb_row)


def _fox_bias_rows(fk3, *, batch, seq, n_heads):
    rows = fk3[:, :, :n_heads, :].transpose(0, 2, 1, 3)
    rows = jnp.pad(rows, ((0, 0), (0, 0), (0, BF16_SUBLANES - 3), (0, 0)))
    return rows.reshape(batch * n_heads, BF16_SUBLANES, seq)


def kernel(x, a_norm_w, a_w_in, a_lb_logits, a_out_norm_w, a_w_out, kv_norm_w, kv_w, kv_f_bias, k_norm_w,
           b_norm_w, b_w_in, b_q_norm_w, b_out_norm_w, b_w_out):
    batch, seq, d = x.shape
    n_heads = d // HEAD_DIM
    n_a = a_w_in.shape[0]
    n_b = b_w_in.shape[0]
    xr = x.reshape(batch * seq, d)

    a_w_out_b = a_w_out.astype(BF16)
    b_w_out_b = b_w_out.astype(BF16)
    wf_b = jnp.pad(kv_w[:, 2 * d:], ((0, 0), (0, HEAD_DIM - n_heads))).astype(BF16)
    f_bias = jnp.pad(kv_f_bias.astype(F32), (0, HEAD_DIM - n_heads)).reshape(1, HEAD_DIM)

    lb_all = jnp.cumsum(jax.nn.softmax(a_lb_logits.astype(F32), axis=0), axis=0)
    lb_all = lb_all - lb_all[0:1]
    ones_row = jnp.ones((1, d), F32)

    h = _norm(xr, a_norm_w[0])
    for layer in range(n_a):
        q, w2, k, v, sg = _proj(
            h,
            [(a_w_in, layer, 0, ones_row, "cast", (BF16,)),
             (a_w_in, layer, d, lb_all[layer].reshape(1, d), "hgate", (F32, BF16)),
             (a_w_in, layer, 2 * d, ones_row, "cast", (BF16,)),
             (a_w_in, layer, 3 * d, ones_row, "silu", (BF16,))],
            name="hgrn_in_proj", seq=seq, tm=1024, tn=256)
        og = _hgrn_mix(q, w2, k, v, batch=batch, seq=seq)
        if layer + 1 < n_a:
            xr, (h,) = _out_proj(og, sg, a_out_norm_w[layer], a_w_out_b, layer, xr, [a_norm_w[layer + 1]])
        else:
            xr, (h_kv, h) = _out_proj(og, sg, a_out_norm_w[layer], a_w_out_b, layer, xr, [kv_norm_w, b_norm_w[0]])

    k_norm_row = jnp.tile(k_norm_w.astype(F32), n_heads).reshape(1, d)
    kt, vv = _proj(
        h_kv,
        [(kv_w, 0, 0, k_norm_row, "headnorm_t", (BF16,)),
         (kv_w, 0, d, ones_row, "cast", (BF16,))],
        name="fox_kv_proj", seq=seq)
    fcol, fk3, fq_cols = _fgate(h_kv, wf_b, f_bias, batch=batch, seq=seq)
    fk_rows = _fox_bias_rows(fk3, batch=batch, seq=seq, n_heads=n_heads)

    for j in range(n_b):
        q_norm_row = jnp.tile(b_q_norm_w[j].astype(F32), n_heads).reshape(1, d)
        q, sg = _proj(
            h,
            [(b_w_in, j, 0, q_norm_row, "headnorm", (BF16,)),
             (b_w_in, j, d, ones_row, "silu", (BF16,))],
            scale=HEAD_DIM ** -0.5 * LOG2E, name="fox_in_proj", seq=seq)
        qk_bound = (HEAD_DIM * jnp.max(jnp.abs(b_q_norm_w[j])) * jnp.max(jnp.abs(k_norm_w))
                    * (HEAD_DIM ** -0.5 * LOG2E * 1.02))
        og = _fox_mix(q, kt, fk_rows, vv, fcol, fq_cols, qk_bound, batch=batch, seq=seq)
        if j + 1 < n_b:
            xr, (h,) = _out_proj(og, sg, b_out_norm_w[j], b_w_out_b, j, xr, [b_norm_w[j + 1]])
        else:
            xr, _ = _out_proj(og, sg, b_out_norm_w[j], b_w_out_b, j, xr, [])

    return xr.reshape(batch, seq, d)
```
